```python
import math
import jax, jax.numpy as jnp
from jax import lax
import numpy as np

D_MODEL = 1024
BATCH = 8
SEQ = 2048
DEPTH = 4
DEC_BATCH = 32
DEC_SEQ = 4
PAST_LEN = 16384
PAGE_SIZE = 128

N_A = DEPTH // 2
N_B = DEPTH - N_A
D_CONV = 768
CONV_W = 3
N_MEM = 256
MEM_HEADS = 4
MEM_HD = 64
MEM_W = MEM_HEADS * MEM_HD
N_HEADS = 12
NOPE_DIM = 64
ROPE_DIM = 32
V_DIM = 64
KV_RANK = 256
Q_RANK = 384
ROPE_THETA = 10000.0
N_EXPERTS = 32
TOP_K = 4
D_FF = 1024
SWIGLU_LIMIT = 7.0
SWIGLU_ALPHA = 1.702
DN_ALPHA = (2 * DEPTH) ** 0.25
DN_BETA = (8 * DEPTH) ** -0.25
Q_BLOCK = 128
LN_EPS = 1e-5
RMS_EPS = 1e-6
POOL_NUM = 5
POOL_DEN = 4

kernel_name = 'yoco_shortconv_mla_moe_step'


def layer_norm(x, g, b):
    xf = x.astype(jnp.float32)
    mu = jnp.mean(xf, -1, keepdims=True)
    var = jnp.mean(jnp.square(xf - mu), -1, keepdims=True)
    return ((xf - mu) * lax.rsqrt(var + LN_EPS)).astype(x.dtype) * g + b


def rms_norm(x, g):
    xf = x.astype(jnp.float32)
    return (xf * lax.rsqrt(jnp.mean(xf * xf, -1, keepdims=True) + RMS_EPS)).astype(x.dtype) * g


def rope(x, pos):
    half = ROPE_DIM // 2
    freqs = ROPE_THETA ** (-jnp.arange(half, dtype=jnp.float32) / half)
    ang = pos.astype(jnp.float32)[:, None] * freqs[None, :]
    cos = jnp.cos(ang)[None, :, None, :].astype(x.dtype)
    sin = jnp.sin(ang)[None, :, None, :].astype(x.dtype)
    x1, x2 = x[..., :half], x[..., half:]
    return jnp.concatenate([x1 * cos - x2 * sin, x1 * sin + x2 * cos], -1)


def causal_conv(u_pad, w):
    s = u_pad.shape[1] - (CONV_W - 1)
    out = w[0] * u_pad[:, 0:s]
    for j in range(1, CONV_W):
        out = out + w[j] * u_pad[:, j:j + s]
    return out


def mem_attention(q, mk, mv):
    s = jnp.einsum('bshd,bmhd->bhsm', q, mk).astype(jnp.float32) * (MEM_HD ** -0.5)
    p = jax.nn.softmax(s, -1).astype(q.dtype)
    return jnp.einsum('bhsm,bmhd->bshd', p, mv)


def mla_attention(q_lat, q_rope, c_kv, k_rope, q_pos, k_pos):
    s = (jnp.einsum('bqhr,bkr->bhqk', q_lat, c_kv)
         + jnp.einsum('bqhe,bke->bhqk', q_rope, k_rope)).astype(jnp.float32)
    s = s * ((NOPE_DIM + ROPE_DIM) ** -0.5)
    s = jnp.where(k_pos[None, :] <= q_pos[:, None], s, -1e30)
    p = jax.nn.softmax(s, -1).astype(q_lat.dtype)
    return jnp.einsum('bhqk,bkr->bqhr', p, c_kv)


def mla_attention_blocked(q_lat, q_rope, c_kv, k_rope, q_pos, k_pos):
    b, s, h, r = q_lat.shape
    nb = s // Q_BLOCK
    ql = jnp.moveaxis(q_lat.reshape(b, nb, Q_BLOCK, h, r), 1, 0)
    qr = jnp.moveaxis(q_rope.reshape(b, nb, Q_BLOCK, h, ROPE_DIM), 1, 0)
    qp = q_pos.reshape(nb, Q_BLOCK)
    out = lax.map(lambda a: mla_attention(a[0], a[1], c_kv, k_rope, a[2], k_pos), (ql, qr, qp))
    return jnp.moveaxis(out, 0, 1).reshape(b, s, h, r)


def moe(x, w_r, b_r, w_gate, b_gate, w_up, b_up, w_down, b_down):
    shp = x.shape
    t = x.reshape(-1, shp[-1])
    logits = (t @ w_r + b_r).astype(jnp.float32)
    top_v, top_i = lax.top_k(logits, TOP_K)
    gates = jax.nn.softmax(top_v, -1)
    combine = jnp.sum(jax.nn.one_hot(top_i, N_EXPERTS, dtype=jnp.float32) * gates[..., None], axis=1)
    combine = combine.astype(x.dtype)

    def expert(acc, p):
        wg, bg, wu, bu, wd, bd, c = p
        g = jnp.minimum(t @ wg + bg, SWIGLU_LIMIT)
        u = jnp.clip(t @ wu + bu, -SWIGLU_LIMIT, SWIGLU_LIMIT)
        h = (u + 1) * (g * jax.nn.sigmoid(SWIGLU_ALPHA * g))
        return acc + c[:, None] * (h @ wd + bd), None

    out, _ = lax.scan(expert, jnp.zeros_like(t),
                      (w_gate, b_gate, w_up, b_up, w_down, b_down, combine.T))
    return out.reshape(shp)


def setup_inputs(seed: int = 0) -> dict:
    key = jax.random.key(seed)
    ks = iter(jax.random.split(key, 48))

    def nrm(shape, fan_in, scale=1.0):
        return jax.random.normal(next(ks), shape, jnp.float32) * (scale * fan_in ** -0.5)

    def near_one(shape):
        return 1.0 + 0.02 * jax.random.normal(next(ks), shape, jnp.float32)

    def small(shape, s=0.02):
        return s * jax.random.normal(next(ks), shape, jnp.float32)

    n_pages = PAST_LEN // PAGE_SIZE
    n_used = DEC_BATCH * n_pages
    n_pool = (n_used * POOL_NUM) // POOL_DEN
    d = {}
    d['x_prompt'] = jax.random.normal(next(ks), (BATCH, SEQ, D_MODEL), jnp.float32)
    d['x_sample'] = jax.random.normal(next(ks), (DEC_BATCH, DEC_SEQ, D_MODEL), jnp.float32)
    d['cache_kv_latent'] = jax.random.normal(next(ks), (n_pool, PAGE_SIZE, KV_RANK), jnp.float32)
    d['cache_k_rope'] = jax.random.normal(next(ks), (n_pool, PAGE_SIZE, ROPE_DIM), jnp.float32)
    d['cache_mem_k'] = jax.random.normal(next(ks), (DEPTH, DEC_BATCH, N_MEM, MEM_HEADS, MEM_HD), jnp.float32)
    d['cache_mem_v'] = 0.5 * jax.random.normal(next(ks), (DEPTH, DEC_BATCH, N_MEM, MEM_HEADS, MEM_HD), jnp.float32)
    d['state_conv'] = jax.random.normal(next(ks), (N_A, DEC_BATCH, CONV_W - 1, D_CONV), jnp.float32)
    d['page_table'] = jax.random.permutation(next(ks), n_pool)[:n_used].reshape(DEC_BATCH, n_pages).astype(jnp.int32)
    d['mem_prompt'] = jax.random.normal(next(ks), (BATCH, N_MEM, D_MODEL), jnp.float32)
    d['w_in_a'] = nrm((N_A, D_MODEL, 3 * D_CONV + MEM_W), D_MODEL)
    d['conv_w'] = nrm((N_A, CONV_W, D_CONV), CONV_W)
    d['w_out_a'] = nrm((N_A, D_CONV + MEM_W, D_MODEL), D_CONV + MEM_W, DN_BETA)
    d['w_in_b'] = nrm((N_B, D_MODEL, Q_RANK + MEM_W), D_MODEL)
    d['g_q'] = near_one((N_B, Q_RANK))
    d['w_uq'] = nrm((N_B, Q_RANK, N_HEADS * (NOPE_DIM + ROPE_DIM)), Q_RANK)
    d['w_out_b'] = nrm((N_B, N_HEADS * V_DIM + MEM_W, D_MODEL), N_HEADS * V_DIM + MEM_W, DN_BETA)
    d['w_kv_down'] = nrm((D_MODEL, KV_RANK + ROPE_DIM), D_MODEL)
    d['g_kv'] = near_one((KV_RANK,))
    d['w_uk'] = nrm((KV_RANK, N_HEADS, NOPE_DIM), KV_RANK)
    d['w_uv'] = nrm((KV_RANK, N_HEADS, V_DIM), KV_RANK, DN_BETA)
    d['w_mem_k'] = nrm((DEPTH, D_MODEL, MEM_W), D_MODEL)
    d['w_mem_v'] = nrm((DEPTH, D_MODEL, MEM_W), D_MODEL, DN_BETA)
    d['ln1_g'] = near_one((DEPTH, D_MODEL))
    d['ln1_b'] = small((DEPTH, D_MODEL))
    d['ln2_g'] = near_one((DEPTH, D_MODEL))
    d['ln2_b'] = small((DEPTH, D_MODEL))
    d['w_router'] = nrm((DEPTH, D_MODEL, N_EXPERTS), D_MODEL)
    d['b_router'] = small((DEPTH, N_EXPERTS), 0.01)
    d['w_gate'] = nrm((DEPTH, N_EXPERTS, D_MODEL, D_FF), D_MODEL)
    d['b_gate'] = small((DEPTH, N_EXPERTS, D_FF))
    d['w_up'] = nrm((DEPTH, N_EXPERTS, D_MODEL, D_FF), D_MODEL)
    d['b_up'] = small((DEPTH, N_EXPERTS, D_FF))
    d['w_down'] = nrm((DEPTH, N_EXPERTS, D_FF, D_MODEL), D_FF, DN_BETA)
    d['b_down'] = small((DEPTH, N_EXPERTS, D_MODEL))
    return d


def reference(x_prompt, x_sample, cache_kv_latent, cache_k_rope, cache_mem_k, cache_mem_v, state_conv,
              page_table, mem_prompt,
              w_in_a, conv_w, w_out_a, w_in_b, g_q, w_uq, w_out_b, w_kv_down, g_kv, w_uk, w_uv,
              w_mem_k, w_mem_v, ln1_g, ln1_b, ln2_g, ln2_b, w_router, b_router,
              w_gate, b_gate, w_up, b_up, w_down, b_down):

    def channel_block(l, x):
        y = moe(x, w_router[l], b_router[l], w_gate[l], b_gate[l], w_up[l], b_up[l], w_down[l], b_down[l])
        return layer_norm(DN_ALPHA * x + y, ln2_g[l], ln2_b[l])

    def conv_layer(l, x, prefix, mk, mv):
        b, s, _ = x.shape
        h = x @ w_in_a[l]
        gb, gc, hv, qm = jnp.split(h, [D_CONV, 2 * D_CONV, 3 * D_CONV], -1)
        u_pad = jnp.concatenate([prefix, gc * hv], 1)
        y_conv = gb * causal_conv(u_pad, conv_w[l])
        y_mem = mem_attention(qm.reshape(b, s, MEM_HEADS, MEM_HD), mk, mv).reshape(b, s, MEM_W)
        o = jnp.concatenate([y_conv, y_mem], -1) @ w_out_a[l]
        x = layer_norm(DN_ALPHA * x + o, ln1_g[l], ln1_b[l])
        return channel_block(l, x), u_pad[:, -(CONV_W - 1):]

    def shared_kv(x, pos):
        ckr = x @ w_kv_down
        c_kv = rms_norm(ckr[..., :KV_RANK], g_kv)
        k_r = rope(ckr[..., None, KV_RANK:], pos)[:, :, 0]
        return c_kv, k_r

    def mla_layer(j, x, q_pos, c_kv, k_rope, k_pos, mk, mv, blocked):
        l = N_A + j
        b, s, _ = x.shape
        h = x @ w_in_b[j]
        cq, qm = h[..., :Q_RANK], h[..., Q_RANK:]
        q = (rms_norm(cq, g_q[j]) @ w_uq[j]).reshape(b, s, N_HEADS, NOPE_DIM + ROPE_DIM)
        q_lat = jnp.einsum('bshd,rhd->bshr', q[..., :NOPE_DIM], w_uk)
        q_rope = rope(q[..., NOPE_DIM:], q_pos)
        attn = mla_attention_blocked if blocked else mla_attention
        lat = attn(q_lat, q_rope, c_kv, k_rope, q_pos, k_pos)
        heads = jnp.einsum('bshr,rhd->bshd', lat, w_uv).reshape(b, s, N_HEADS * V_DIM)
        y_mem = mem_attention(qm.reshape(b, s, MEM_HEADS, MEM_HD), mk, mv).reshape(b, s, MEM_W)
        o = jnp.concatenate([heads, y_mem], -1) @ w_out_b[j]
        x = layer_norm(DN_ALPHA * x + o, ln1_g[l], ln1_b[l])
        return channel_block(l, x)

    def trunk(x, n_past, conv_prefix, mem_k, mem_v, past_c, past_kr):
        b, s, _ = x.shape
        q_pos = n_past + jnp.arange(s, dtype=jnp.int32)
        k_pos = jnp.arange(n_past + s, dtype=jnp.int32)
        blocked = past_c is None
        conv_states = []
        c_new = kr_new = c_all = kr_all = None
        for l in range(DEPTH):
            if l < N_A:
                x, st = conv_layer(l, x, conv_prefix[l], mem_k[l], mem_v[l])
                conv_states.append(st)
            else:
                if l == N_A:
                    c_new, kr_new = shared_kv(x, q_pos)
                    if blocked:
                        c_all, kr_all = c_new, kr_new
                    else:
                        c_all = jnp.concatenate([past_c, c_new], 1)
                        kr_all = jnp.concatenate([past_kr, kr_new], 1)
                x = mla_layer(l - N_A, x, q_pos, c_all, kr_all, k_pos, mem_k[l], mem_v[l], blocked)
        return x, c_new, kr_new, jnp.stack(conv_states)

    bp = x_prompt.shape[0]
    mem_k_p = jnp.einsum('bmd,ldf->lbmf', mem_prompt, w_mem_k).reshape(DEPTH, bp, N_MEM, MEM_HEADS, MEM_HD)
    mem_v_p = jnp.einsum('bmd,ldf->lbmf', mem_prompt, w_mem_v).reshape(DEPTH, bp, N_MEM, MEM_HEADS, MEM_HD)
    prefix_p = jnp.zeros((N_A, bp, CONV_W - 1, D_CONV), x_prompt.dtype)
    y_prompt, c_p, kr_p, conv_p = trunk(x_prompt, 0, prefix_p, mem_k_p, mem_v_p, None, None)

    bs = x_sample.shape[0]
    past_c = cache_kv_latent[page_table].reshape(bs, -1, KV_RANK)
    past_kr = cache_k_rope[page_table].reshape(bs, -1, ROPE_DIM)
    n_past = past_c.shape[1]
    y_sample, c_s, kr_s, conv_s = trunk(x_sample, n_past, state_conv, cache_mem_k, cache_mem_v, past_c, past_kr)

    return (y_prompt, y_sample, c_p, kr_p, c_s, kr_s, mem_k_p, mem_v_p, conv_p, conv_s)
```

```python
import functools
import math

import jax
import jax.numpy as jnp
from jax import lax
from jax.experimental import pallas as pl
from jax.experimental.pallas import tpu as pltpu

F32 = jnp.float32
BF16 = jnp.bfloat16
I32 = jnp.int32
U32 = jnp.uint32

TOP_K = 4
SWIGLU_LIMIT = 7.0
SWIGLU_ALPHA = 1.702
ROPE_THETA = 10000.0
LN_EPS = 1e-5
RMS_EPS = 1e-6
MASK_VALUE = -1e30

LANES = 128
SUBLANES = 8
VMEM_LIMIT = 56 * 1024 * 1024

ROW_TILE = 512
MOE_TILE = 256
EXPERT_ROWS = 256
PAGES_PER_STEP = 16
HI_MASK = 0xFFFF0000
NUM_EXPERTS = 32
NUM_EXPERTS_PAD_ROWS = NUM_EXPERTS * SUBLANES

_NT = (((1,), (1,)), ((), ()))


def _cparams(sem):
    return pltpu.CompilerParams(dimension_semantics=sem, vmem_limit_bytes=VMEM_LIMIT)


def _tile(n, pref):
    t = min(n, pref)
    assert n % t == 0, (n, pref)
    return t


def _pack_bf16_pairs(x):
    half = x.shape[1] // 2
    bits = pltpu.bitcast(x, U32)
    return (bits[:, :half] >> 16) | (bits[:, half:] & jnp.uint32(HI_MASK))


def _unpack_bf16_pairs(p):
    lo = pltpu.bitcast(p << 16, F32)
    hi = pltpu.bitcast(p & jnp.uint32(HI_MASK), F32)
    return jnp.concatenate([lo, hi], axis=1).astype(BF16)


def _layer_norm(z, g, b):
    mu = jnp.mean(z, axis=-1, keepdims=True)
    zc = z - mu
    var = jnp.mean(zc * zc, axis=-1, keepdims=True)
    return zc * lax.rsqrt(var + LN_EPS) * g + b


def _rms_norm(x, g):
    return x * lax.rsqrt(jnp.mean(x * x, axis=-1, keepdims=True) + RMS_EPS) * g


def _proj_kernel(x_ref, w_ref, *o_refs):
    h = jnp.dot(x_ref[...].astype(BF16), w_ref[...], preferred_element_type=F32)
    off = 0
    for o_ref in o_refs:
        n = o_ref.shape[-1]
        o_ref[...] = h[:, off:off + n].astype(o_ref.dtype)
        off += n


def _proj(x, w, splits, name):
    m, k = x.shape
    n = w.shape[1]
    assert sum(splits) == n
    tm = _tile(m, ROW_TILE)
    return pl.pallas_call(
        _proj_kernel,
        grid=(m // tm,),
        in_specs=[pl.BlockSpec((tm, k), lambda i: (i, 0)), pl.BlockSpec((k, n), lambda i: (0, 0))],
        out_specs=[pl.BlockSpec((tm, s), lambda i: (i, 0)) for s in splits],
        out_shape=[jax.ShapeDtypeStruct((m, s), F32) for s in splits],
        compiler_params=_cparams(("parallel",)),
        name=name,
    )(x, w)


def _memkv_kernel(x_ref, w_ref, k_ref, v_ref):
    depth, _, f = k_ref.shape
    h = jnp.dot(x_ref[...].astype(BF16), w_ref[...], preferred_element_type=F32)
    for l in range(depth):
        k_ref[l] = h[:, l * f:(l + 1) * f]
        v_ref[l] = h[:, (depth + l) * f:(depth + l + 1) * f]


def _mem_kv(mem, w_cat, depth):
    r, d = mem.shape
    f = w_cat.shape[1] // (2 * depth)
    tm = _tile(r, ROW_TILE)
    shp = jax.ShapeDtypeStruct((depth, r, f), F32)
    spec = pl.BlockSpec((depth, tm, f), lambda i: (0, i, 0))
    return pl.pallas_call(
        _memkv_kernel,
        grid=(r // tm,),
        in_specs=[pl.BlockSpec((tm, d), lambda i: (i, 0)), pl.BlockSpec(w_cat.shape, lambda i: (0, 0))],
        out_specs=[spec, spec],
        out_shape=[shp, shp],
        compiler_params=_cparams(("parallel",)),
        name="mem_kv",
    )(mem, w_cat)


def _conv_seq_kernel(gb_ref, gc_ref, hv_ref, pre_ref, w_ref, y_ref, st_ref, carry):
    j = pl.program_id(1)

    @pl.when(j == 0)
    def _():
        carry[0:2, :] = pre_ref[0]

    u = gc_ref[...] * hv_ref[...]
    ts = u.shape[0]
    c0 = carry[0:1, :]
    c1 = carry[1:2, :]
    row = lax.broadcasted_iota(I32, u.shape, 0)
    u1 = jnp.where(row == 0, c1, pltpu.roll(u, 1, 0))
    u2 = jnp.where(row == 0, c0, jnp.where(row == 1, c1, pltpu.roll(u, 2, 0)))
    w = w_ref[...]
    y_ref[...] = gb_ref[...] * (w[0:1] * u2 + w[1:2] * u1 + w[2:3] * u)
    last = u[ts - 2:ts, :]
    carry[0:2, :] = last
    st_ref[0] = last


def _conv_seq(gb, gc, hv, prefix, w, batch):
    m, c = gb.shape
    s = m // batch
    ts = _tile(s, ROW_TILE)
    nt = s // ts
    row = pl.BlockSpec((ts, c), lambda b, j: (b * nt + j, 0))
    return pl.pallas_call(
        _conv_seq_kernel,
        grid=(batch, nt),
        in_specs=[row, row, row,
                  pl.BlockSpec((1, 2, c), lambda b, j: (b, 0, 0)),
                  pl.BlockSpec((3, c), lambda b, j: (0, 0))],
        out_specs=[row, pl.BlockSpec((1, 2, c), lambda b, j: (b, 0, 0))],
        out_shape=[jax.ShapeDtypeStruct((m, c), F32), jax.ShapeDtypeStruct((batch, 2, c), F32)],
        scratch_shapes=[pltpu.VMEM((SUBLANES, c), F32)],
        compiler_params=_cparams(("parallel", "arbitrary")),
        name="conv_seq",
    )(gb, gc, hv, prefix, w)


def _conv_short_kernel(gb_ref, gc_ref, hv_ref, pre_ref, w_ref, y_ref, st_ref):
    s_len = gb_ref.shape[1]
    w = w_ref[...]

    def u(k):
        if k < 0:
            return pre_ref[:, 2 + k, :]
        return gc_ref[:, k, :] * hv_ref[:, k, :]

    for s in range(s_len):
        y_ref[:, s, :] = gb_ref[:, s, :] * (w[0:1] * u(s - 2) + w[1:2] * u(s - 1) + w[2:3] * u(s))
    for k in range(2):
        st_ref[:, k, :] = u(s_len - 2 + k)


def _conv_short(gb, gc, hv, prefix, w):
    batch, s, c = gb.shape
    full = lambda shp: pl.BlockSpec(shp, lambda i: (0,) * len(shp))
    return pl.pallas_call(
        _conv_short_kernel,
        grid=(1,),
        in_specs=[full(gb.shape)] * 3 + [full(prefix.shape), full(w.shape)],
        out_specs=[full(gb.shape), full(prefix.shape)],
        out_shape=[jax.ShapeDtypeStruct(gb.shape, F32), jax.ShapeDtypeStruct(prefix.shape, F32)],
        compiler_params=_cparams(("arbitrary",)),
        name="conv_short",
    )(gb, gc, hv, prefix, w)


def _mem_attn_kernel(q_ref, k_ref, v_ref, o_ref, *, heads):
    q = q_ref[0].astype(BF16)
    k = k_ref[0].astype(BF16)
    v = v_ref[0].astype(BF16)
    hd = q.shape[1] // heads
    outs = []
    for h in range(heads):
        sl = slice(h * hd, (h + 1) * hd)
        s = lax.dot_general(q[:, sl], k[:, sl], _NT, preferred_element_type=F32) * (hd ** -0.5)
        m = jnp.max(s, axis=-1, keepdims=True)
        p = jnp.exp(s - m)
        l = jnp.sum(p, axis=-1, keepdims=True)
        outs.append(jnp.dot(p.astype(BF16), v[:, sl], preferred_element_type=F32) / l)
    o_ref[0] = jnp.concatenate(outs, axis=-1)


def _mem_attn(q, mk, mv, heads):
    b, s, w = q.shape
    nm = mk.shape[1]
    tq = _tile(s, ROW_TILE)
    return pl.pallas_call(
        functools.partial(_mem_attn_kernel, heads=heads),
        grid=(b, s // tq),
        in_specs=[pl.BlockSpec((1, tq, w), lambda i, j: (i, j, 0)),
                  pl.BlockSpec((1, nm, w), lambda i, j: (i, 0, 0)),
                  pl.BlockSpec((1, nm, w), lambda i, j: (i, 0, 0))],
        out_specs=pl.BlockSpec((1, tq, w), lambda i, j: (i, j, 0)),
        out_shape=jax.ShapeDtypeStruct((b, s, w), F32),
        compiler_params=_cparams(("parallel", "parallel")),
        name="mem_attn",
    )(q, mk, mv)


def _out_ln_kernel(a_ref, m_ref, x_ref, wa_ref, wm_ref, g_ref, b_ref, o_ref, *, alpha):
    o = jnp.dot(a_ref[...].astype(BF16), wa_ref[...], preferred_element_type=F32)
    o = o + jnp.dot(m_ref[...].astype(BF16), wm_ref[...], preferred_element_type=F32)
    o_ref[...] = _layer_norm(alpha * x_ref[...] + o, g_ref[...], b_ref[...])


def _out_ln(a, mem, x, wa, wm, g, b, alpha):
    m, d = x.shape
    tm = _tile(m, ROW_TILE)
    row = lambda n: pl.BlockSpec((tm, n), lambda i: (i, 0))
    full = lambda arr: pl.BlockSpec(arr.shape, lambda i: (0, 0))
    return pl.pallas_call(
        functools.partial(_out_ln_kernel, alpha=alpha),
        grid=(m // tm,),
        in_specs=[row(a.shape[1]), row(mem.shape[1]), row(d), full(wa), full(wm), full(g), full(b)],
        out_specs=row(d),
        out_shape=jax.ShapeDtypeStruct((m, d), F32),
        compiler_params=_cparams(("parallel",)),
        name="out_ln",
    )(a, mem, x, wa, wm, g, b)


def _router_kernel(x_ref, wt_ref, b_ref, pos_ref, gate_ref, cnt_ref):
    x = x_ref[...]
    logits = lax.dot_general(wt_ref[...], x, _NT, precision=lax.Precision.HIGHEST,
                             preferred_element_type=F32) + b_ref[...]
    ne, tt = logits.shape
    eidx = lax.broadcasted_iota(I32, (ne, tt), 0)
    sels, vals = [], []
    l = logits
    for _ in range(TOP_K):
        m = jnp.max(l, axis=0, keepdims=True)
        first = jnp.min(jnp.where(l == m, eidx, ne), axis=0, keepdims=True)
        sel = eidx == first
        sels.append(sel)
        vals.append(m)
        l = jnp.where(sel, -jnp.inf, l)
    ex = [jnp.exp(v - vals[0]) for v in vals]
    den = ex[0]
    for e in ex[1:]:
        den = den + e
    gate_ref[...] = jnp.concatenate([e / den for e in ex], axis=0)

    chosen = jnp.zeros((ne, tt), F32)
    for sel in sels:
        chosen = jnp.where(sel, 1.0, chosen)
    cnt = jnp.sum(chosen, axis=1, keepdims=True)
    cnt_ref[0] = cnt.astype(I32)
    cnt8 = jnp.floor((cnt + (SUBLANES - 1)) * (1.0 / SUBLANES)) * SUBLANES
    er = lax.broadcasted_iota(I32, (ne, ne), 0)
    ec = lax.broadcasted_iota(I32, (ne, ne), 1)
    below = jnp.where(ec < er, 1.0, 0.0)
    seg_start = jnp.dot(below, jnp.broadcast_to(cnt8, (ne, LANES)), precision=lax.Precision.HIGHEST,
                        preferred_element_type=F32)[:, 0:1]
    tr = lax.broadcasted_iota(I32, (tt, tt), 0)
    tc = lax.broadcasted_iota(I32, (tt, tt), 1)
    earlier = jnp.where(tr < tc, 1.0, 0.0).astype(BF16)
    rank = jnp.dot(chosen.astype(BF16), earlier, preferred_element_type=F32)
    slot = seg_start + rank
    pos_ref[...] = jnp.concatenate(
        [jnp.sum(jnp.where(sel, slot, 0.0), axis=0, keepdims=True) for sel in sels], axis=0).astype(I32)


def _router(x, wt, b, tt):
    m, d = x.shape
    ne = wt.shape[0]
    nt = m // tt
    return pl.pallas_call(
        _router_kernel,
        grid=(nt,),
        in_specs=[pl.BlockSpec((tt, d), lambda i: (i, 0)),
                  pl.BlockSpec((ne, d), lambda i: (0, 0)),
                  pl.BlockSpec((ne, 1), lambda i: (0, 0))],
        out_specs=[pl.BlockSpec((TOP_K, tt), lambda i: (0, i)),
                   pl.BlockSpec((TOP_K, tt), lambda i: (0, i)),
                   pl.BlockSpec((1, ne, 1), lambda i: (i, 0, 0))],
        out_shape=[jax.ShapeDtypeStruct((TOP_K, m), I32),
                   jax.ShapeDtypeStruct((TOP_K, m), F32),
                   jax.ShapeDtypeStruct((nt, ne, 1), I32)],
        compiler_params=_cparams(("parallel",)),
        name="router",
    )(x, wt, b)


def _segment_copies(n_ref, local_ref, global_ref, tile, ne, make_copy):
    def args(e):
        k = tile * ne + e
        return (pl.multiple_of(n_ref[k], SUBLANES), pl.multiple_of(local_ref[k], SUBLANES),
                pl.multiple_of(global_ref[k], SUBLANES))

    def start(e, c):
        n, lo, go = args(e)

        @pl.when(n > 0)
        def _():
            make_copy(n, lo, go).start()
        return c

    def wait(e, c):
        n, lo, go = args(e)

        @pl.when(n > 0)
        def _():
            make_copy(n, lo, go).wait()
        return c

    lax.fori_loop(0, ne, start, 0)
    return lambda: lax.fori_loop(0, ne, wait, 0)


def _dispatch_kernel(n_ref, local_ref, global_ref, x_ref, slot_ref, init_ref, xs_ref, buf, sem, *, tile0, ne):
    del init_ref
    tile = pl.program_id(0) + tile0
    x = x_ref[...].astype(BF16)
    slot = slot_ref[...]
    rows = buf.shape[0]
    r = lax.broadcasted_iota(I32, (rows, x.shape[0]), 0)
    onehot = jnp.zeros(r.shape, F32)
    for k in range(TOP_K):
        onehot = jnp.where(slot[k:k + 1, :] == r, 1.0, onehot)
    buf[...] = _pack_bf16_pairs(jnp.dot(onehot.astype(BF16), x, preferred_element_type=F32))
    wait_all = _segment_copies(
        n_ref, local_ref, global_ref, tile, ne,
        lambda n, lo, go: pltpu.make_async_copy(buf.at[pl.ds(lo, n)], xs_ref.at[pl.ds(go, n)], sem))
    wait_all()


def _dispatch(seg, x, slot, xs, tile0, tt):
    m, d = x.shape
    n8, local, glob = seg
    rows = TOP_K * tt + NUM_EXPERTS_PAD_ROWS
    return pl.pallas_call(
        functools.partial(_dispatch_kernel, tile0=tile0, ne=NUM_EXPERTS),
        grid_spec=pltpu.PrefetchScalarGridSpec(
            num_scalar_prefetch=3,
            grid=(m // tt,),
            in_specs=[pl.BlockSpec((tt, d), lambda i, *_: (i, 0)),
                      pl.BlockSpec((TOP_K, tt), lambda i, *_: (0, i)),
                      pl.BlockSpec(memory_space=pl.ANY)],
            out_specs=pl.BlockSpec(memory_space=pl.ANY),
            scratch_shapes=[pltpu.VMEM((rows, d // 2), U32), pltpu.SemaphoreType.DMA]),
        out_shape=jax.ShapeDtypeStruct(xs.shape, U32),
        input_output_aliases={5: 0},
        compiler_params=_cparams(("arbitrary",)),
        name="moe_dispatch",
    )(n8, local, glob, x, slot, xs)


def _expert_kernel(be_ref, nu_ref, xs_ref, wg_ref, bg_ref, wu_ref, bu_ref, wd_ref, bd_ref, ys_ref,
                   wg_bf, wu_bf, wd_bf):
    b = pl.program_id(0)

    @pl.when(b < nu_ref[0])
    def _():
        prev = be_ref[jnp.maximum(b - 1, 0)]

        @pl.when((b == 0) | (be_ref[b] != prev))
        def _():
            wg_bf[...] = wg_ref[0].astype(BF16)
            wu_bf[...] = wu_ref[0].astype(BF16)
            wd_bf[...] = wd_ref[0].astype(BF16)

        x = _unpack_bf16_pairs(xs_ref[...])
        g = jnp.dot(x, wg_bf[...], preferred_element_type=F32) + bg_ref[0]
        u = jnp.dot(x, wu_bf[...], preferred_element_type=F32) + bu_ref[0]
        g = jnp.minimum(g, SWIGLU_LIMIT)
        u = jnp.clip(u, -SWIGLU_LIMIT, SWIGLU_LIMIT)
        h = (u + 1.0) * (g * jax.nn.sigmoid(SWIGLU_ALPHA * g))
        y = jnp.dot(h.astype(BF16), wd_bf[...], preferred_element_type=F32) + bd_ref[0]
        ys_ref[...] = _pack_bf16_pairs(y.astype(BF16).astype(F32))

    @pl.when(b >= nu_ref[0])
    def _():
        ys_ref[...] = jnp.zeros(ys_ref.shape, U32)


def _experts(block_expert, n_used, xs, wg, bg, wu, bu, wd, bd):
    rows, half = xs.shape
    ne, d, f = wg.shape
    nb = rows // EXPERT_ROWS
    blk = lambda b, be, nu: (jnp.minimum(b, nu[0] - 1), 0)
    wsel = lambda b, be, nu: (be[b], 0, 0)
    return pl.pallas_call(
        _expert_kernel,
        grid_spec=pltpu.PrefetchScalarGridSpec(
            num_scalar_prefetch=2,
            grid=(nb,),
            in_specs=[pl.BlockSpec((EXPERT_ROWS, half), blk),
                      pl.BlockSpec((1, d, f), wsel), pl.BlockSpec((1, 1, f), wsel),
                      pl.BlockSpec((1, d, f), wsel), pl.BlockSpec((1, 1, f), wsel),
                      pl.BlockSpec((1, f, d), wsel), pl.BlockSpec((1, 1, d), wsel)],
            out_specs=pl.BlockSpec((EXPERT_ROWS, half), lambda b, be, nu: (b, 0)),
            scratch_shapes=[pltpu.VMEM((d, f), BF16), pltpu.VMEM((d, f), BF16), pltpu.VMEM((f, d), BF16)]),
        out_shape=jax.ShapeDtypeStruct((rows, half), U32),
        compiler_params=_cparams(("arbitrary",)),
        name="moe_experts",
    )(block_expert, n_used, xs, wg, bg.reshape(ne, 1, f), wu, bu.reshape(ne, 1, f), wd, bd.reshape(ne, 1, d))


def _combine_kernel(n_ref, local_ref, global_ref, ys_ref, slot_ref, gate_ref, x_ref, g_ref, b_ref, o_ref,
                    buf, sem, *, tile0, ne, alpha):
    tile = pl.program_id(0) + tile0
    buf[...] = jnp.zeros(buf.shape, U32)
    wait_all = _segment_copies(
        n_ref, local_ref, global_ref, tile, ne,
        lambda n, lo, go: pltpu.make_async_copy(ys_ref.at[pl.ds(go, n)], buf.at[pl.ds(lo, n)], sem))
    slot = slot_ref[...]
    gate = gate_ref[...]
    rows = buf.shape[0]
    r = lax.broadcasted_iota(I32, (slot.shape[0], rows), 1)
    weights = jnp.zeros(r.shape, F32)
    for k in range(TOP_K):
        weights = jnp.where(slot[:, k:k + 1] == r, gate[:, k:k + 1], weights)
    wait_all()
    y = jnp.dot(weights.astype(BF16), _unpack_bf16_pairs(buf[...]), preferred_element_type=F32)
    o_ref[...] = _layer_norm(alpha * x_ref[...] + y, g_ref[...], b_ref[...])


def _combine(seg, ys, slot_t, gate_t, x, g, b, tile0, tt, alpha):
    m, d = x.shape
    n8, local, glob = seg
    rows = TOP_K * tt + NUM_EXPERTS_PAD_ROWS
    return pl.pallas_call(
        functools.partial(_combine_kernel, tile0=tile0, ne=NUM_EXPERTS, alpha=alpha),
        grid_spec=pltpu.PrefetchScalarGridSpec(
            num_scalar_prefetch=3,
            grid=(m // tt,),
            in_specs=[pl.BlockSpec(memory_space=pl.ANY),
                      pl.BlockSpec((tt, TOP_K), lambda i, *_: (i, 0)),
                      pl.BlockSpec((tt, TOP_K), lambda i, *_: (i, 0)),
                      pl.BlockSpec((tt, d), lambda i, *_: (i, 0)),
                      pl.BlockSpec((1, d), lambda i, *_: (0, 0)),
                      pl.BlockSpec((1, d), lambda i, *_: (0, 0))],
            out_specs=pl.BlockSpec((tt, d), lambda i, *_: (i, 0)),
            scratch_shapes=[pltpu.VMEM((rows, d // 2), U32), pltpu.SemaphoreType.DMA]),
        out_shape=jax.ShapeDtypeStruct((m, d), F32),
        compiler_params=_cparams(("arbitrary",)),
        name="moe_combine",
    )(n8, local, glob, ys, slot_t, gate_t, x, g, b)


def _moe_ln(xs_groups, wt, br, wg, bg, wu, bu, wd, bd, g, b, alpha):
    ne = wt.shape[0]
    assert ne == NUM_EXPERTS
    d = xs_groups[0].shape[1]
    tiles = [_tile(x.shape[0], MOE_TILE) for x in xs_groups]
    routed = [_router(x, wt, br, tt) for x, tt in zip(xs_groups, tiles)]
    cnt = jnp.concatenate([r[2][:, :, 0] for r in routed], axis=0)
    n_tiles = cnt.shape[0]
    n8 = (cnt + (SUBLANES - 1)) // SUBLANES * SUBLANES
    local = jnp.cumsum(n8, axis=1) - n8
    per_expert = jnp.sum(n8, axis=0)
    per_expert_pad = (per_expert + (EXPERT_ROWS - 1)) // EXPERT_ROWS * EXPERT_ROWS
    expert_base = jnp.cumsum(per_expert_pad) - per_expert_pad
    glob = expert_base[None, :] + jnp.cumsum(n8, axis=0) - n8
    seg = tuple(a.reshape(-1).astype(I32) for a in (n8, local, glob))
    max_rows = sum(x.shape[0] * TOP_K + (x.shape[0] // tt) * NUM_EXPERTS_PAD_ROWS
                   for x, tt in zip(xs_groups, tiles)) + ne * (EXPERT_ROWS - 1)
    nb = -(-max_rows // EXPERT_ROWS)
    blocks_per_expert = per_expert_pad // EXPERT_ROWS
    n_used = jnp.sum(blocks_per_expert).astype(I32).reshape(1)
    block_expert = jnp.minimum(
        jnp.searchsorted(jnp.cumsum(blocks_per_expert), jnp.arange(nb, dtype=I32), side="right"),
        ne - 1).astype(I32)

    xs = jnp.zeros((nb * EXPERT_ROWS, d // 2), U32)
    tile0 = 0
    for x, tt, r in zip(xs_groups, tiles, routed):
        xs = _dispatch(seg, x, r[0], xs, tile0, tt)
        tile0 += x.shape[0] // tt
    assert tile0 == n_tiles
    ys = _experts(block_expert, n_used, xs, wg, bg, wu, bu, wd, bd)
    outs = []
    tile0 = 0
    for x, tt, r in zip(xs_groups, tiles, routed):
        outs.append(_combine(seg, ys, r[0].T, r[1].T, x, g, b, tile0, tt, alpha))
        tile0 += x.shape[0] // tt
    return outs


def _rope_table_kernel(freq_ref, cos_ref, sin_ref, *, start):
    pos = (lax.broadcasted_iota(I32, cos_ref.shape, 0) + start).astype(F32)
    ang = pos * freq_ref[...]
    cos_ref[...] = jnp.cos(ang)
    sin_ref[...] = jnp.sin(ang)


def _rope_table(start, n, rope_dim):
    half = rope_dim // 2
    freqs = ROPE_THETA ** (-jnp.arange(half, dtype=F32) / half)
    freq_row = jnp.concatenate([freqs, freqs]).reshape(1, rope_dim)
    rows = -(-n // SUBLANES) * SUBLANES
    shp = jax.ShapeDtypeStruct((rows, rope_dim), F32)
    full = pl.BlockSpec((rows, rope_dim), lambda i: (0, 0))
    cos, sin = pl.pallas_call(
        functools.partial(_rope_table_kernel, start=start),
        grid=(1,),
        in_specs=[pl.BlockSpec((1, rope_dim), lambda i: (0, 0))],
        out_specs=[full, full],
        out_shape=[shp, shp],
        name="rope_table",
    )(freq_row)
    return cos[:n], sin[:n]


def _swap_halves_signed(w):
    half = w.shape[-1] // 2
    return jnp.concatenate([-w[..., half:], w[..., :half]], axis=-1)


def _kv_kernel(x_ref, wc_ref, wr_ref, wrs_ref, g_ref, cos_ref, sin_ref, c_ref, kr_ref):
    x = x_ref[...].astype(BF16)
    c = jnp.dot(x, wc_ref[...], preferred_element_type=F32)
    c_ref[...] = _rms_norm(c, g_ref[...])
    r = jnp.dot(x, wr_ref[...], preferred_element_type=F32)
    rs = jnp.dot(x, wrs_ref[...], preferred_element_type=F32)
    kr_ref[...] = r * cos_ref[...] + rs * sin_ref[...]


def _shared_kv(x, wc, wr, wrs, g, cos, sin):
    m, d = x.shape
    rk = wc.shape[1]
    e = wr.shape[1]
    tm = _tile(min(m, cos.shape[0]), ROW_TILE)
    period = cos.shape[0] // tm
    full = lambda arr: pl.BlockSpec(arr.shape, lambda i: (0, 0))
    tab = pl.BlockSpec((tm, e), lambda i: (i % period, 0))
    return pl.pallas_call(
        _kv_kernel,
        grid=(m // tm,),
        in_specs=[pl.BlockSpec((tm, d), lambda i: (i, 0)), full(wc), full(wr), full(wrs), full(g), tab, tab],
        out_specs=[pl.BlockSpec((tm, rk), lambda i: (i, 0)), pl.BlockSpec((tm, e), lambda i: (i, 0))],
        out_shape=[jax.ShapeDtypeStruct((m, rk), F32), jax.ShapeDtypeStruct((m, e), F32)],
        compiler_params=_cparams(("parallel",)),
        name="shared_kv",
    )(x, wc, wr, wrs, g, cos, sin)


def _kv_up_kernel(c_ref, kr_ref, wk_ref, rep_ref, wv_ref, k_ref, v_ref):
    c = c_ref[...].astype(BF16)
    k = jnp.dot(c, wk_ref[...], preferred_element_type=F32)
    k = k + jnp.dot(kr_ref[...].astype(BF16), rep_ref[...], preferred_element_type=F32)
    k_ref[...] = k.astype(BF16)
    v_ref[...] = jnp.dot(c, wv_ref[...], preferred_element_type=F32).astype(BF16)


def _kv_up(c, kr, wk_pad, rep, wv):
    m, rk = c.shape
    tm = _tile(m, ROW_TILE)
    full = lambda arr: pl.BlockSpec(arr.shape, lambda i: (0, 0))
    row = lambda n: pl.BlockSpec((tm, n), lambda i: (i, 0))
    return pl.pallas_call(
        _kv_up_kernel,
        grid=(m // tm,),
        in_specs=[row(rk), row(kr.shape[1]), full(wk_pad), full(rep), full(wv)],
        out_specs=[row(wk_pad.shape[1]), row(wv.shape[1])],
        out_shape=[jax.ShapeDtypeStruct((m, wk_pad.shape[1]), BF16), jax.ShapeDtypeStruct((m, wv.shape[1]), BF16)],
        compiler_params=_cparams(("parallel",)),
        name="kv_up",
    )(c, kr, wk_pad, rep, wv)


def _q_prompt_kernel(x_ref, wc_ref, wm_ref, g_ref, wq_ref, wqs_ref, cos_ref, sin_ref, q_ref, qm_ref):
    x = x_ref[...].astype(BF16)
    qm_ref[...] = jnp.dot(x, wm_ref[...], preferred_element_type=F32)
    cq = _rms_norm(jnp.dot(x, wc_ref[...], preferred_element_type=F32), g_ref[...]).astype(BF16)
    qa = jnp.dot(cq, wq_ref[...], preferred_element_type=F32)
    qb = jnp.dot(cq, wqs_ref[...], preferred_element_type=F32)
    cos = cos_ref[...]
    sin = sin_ref[...]
    for h in range(q_ref.shape[1] // LANES):
        sl = slice(h * LANES, (h + 1) * LANES)
        q_ref[:, sl] = (qa[:, sl] * cos + qb[:, sl] * sin).astype(BF16)


def _q_prompt(x, wc, wm, g, wq_pad, wqs_pad, cos_slot, sin_slot):
    m, d = x.shape
    tm = _tile(min(m, cos_slot.shape[0]), ROW_TILE)
    period = cos_slot.shape[0] // tm
    full = lambda arr: pl.BlockSpec(arr.shape, lambda i: (0, 0))
    tab = pl.BlockSpec((tm, LANES), lambda i: (i % period, 0))
    row = lambda n: pl.BlockSpec((tm, n), lambda i: (i, 0))
    return pl.pallas_call(
        _q_prompt_kernel,
        grid=(m // tm,),
        in_specs=[row(d), full(wc), full(wm), full(g), full(wq_pad), full(wqs_pad), tab, tab],
        out_specs=[row(wq_pad.shape[1]), row(wm.shape[1])],
        out_shape=[jax.ShapeDtypeStruct((m, wq_pad.shape[1]), BF16), jax.ShapeDtypeStruct((m, wm.shape[1]), F32)],
        compiler_params=_cparams(("parallel",)),
        name="q_prompt",
    )(x, wc, wm, g, wq_pad, wqs_pad, cos_slot, sin_slot)


def _mla_prompt_kernel(q_ref, k_ref, v_ref, o_ref, m_ref, l_ref, acc_ref, *, scale, v_dim):
    qi = pl.program_id(2)
    ki = pl.program_id(3)
    tq = q_ref.shape[0]
    tk = k_ref.shape[0]

    @pl.when(ki == 0)
    def _():
        m_ref[...] = jnp.full(m_ref.shape, -jnp.inf, F32)
        l_ref[...] = jnp.zeros(l_ref.shape, F32)
        acc_ref[...] = jnp.zeros(acc_ref.shape, F32)

    @pl.when(ki * tk <= qi * tq + (tq - 1))
    def _():
        qpos = qi * tq + lax.broadcasted_iota(I32, (tq, tk), 0)
        kpos = ki * tk + lax.broadcasted_iota(I32, (tq, tk), 1)
        visible = kpos <= qpos
        v = v_ref[...]
        for hh in range(2):
            sl = slice(hh * LANES, (hh + 1) * LANES)
            s = lax.dot_general(q_ref[:, sl], k_ref[:, sl], _NT, preferred_element_type=F32) * scale
            s = jnp.where(visible, s, MASK_VALUE)
            m_old = m_ref[hh]
            m_new = jnp.maximum(m_old, jnp.max(s, axis=-1, keepdims=True))
            a = jnp.exp(m_old - m_new)
            p = jnp.exp(s - m_new)
            l_ref[hh] = a * l_ref[hh] + jnp.sum(p, axis=-1, keepdims=True)
            acc_ref[hh] = a * acc_ref[hh] + jnp.dot(p.astype(BF16), v, preferred_element_type=F32)
            m_ref[hh] = m_new

    @pl.when(ki == pl.num_programs(3) - 1)
    def _():
        lane = lax.broadcasted_iota(I32, (tq, 2 * v_dim), 1)
        o_ref[...] = jnp.where(lane < v_dim, acc_ref[0] / l_ref[0], acc_ref[1] / l_ref[1]).astype(o_ref.dtype)


def _mla_prompt(q, k, v, batch, scale):
    m, hw = q.shape
    heads = hw // LANES
    v_dim = v.shape[1] // heads
    assert heads % 2 == 0 and 2 * v_dim == LANES
    s = m // batch
    tq = _tile(s, ROW_TILE)
    tk = tq
    nq = s // tq
    nk = s // tk
    last = lambda qi: ((qi + 1) * tq - 1) // tk
    return pl.pallas_call(
        functools.partial(_mla_prompt_kernel, scale=scale, v_dim=v_dim),
        grid=(batch, heads // 2, nq, nk),
        in_specs=[pl.BlockSpec((tq, 2 * LANES), lambda b, h, qi, ki: (b * nq + qi, h)),
                  pl.BlockSpec((tk, 2 * LANES), lambda b, h, qi, ki: (b * nk + jnp.minimum(ki, last(qi)), h)),
                  pl.BlockSpec((tk, 2 * v_dim), lambda b, h, qi, ki: (b * nk + jnp.minimum(ki, last(qi)), h))],
        out_specs=pl.BlockSpec((tq, 2 * v_dim), lambda b, h, qi, ki: (b * nq + qi, h)),
        out_shape=jax.ShapeDtypeStruct((m, heads * v_dim), BF16),
        scratch_shapes=[pltpu.VMEM((2, tq, 1), F32), pltpu.VMEM((2, tq, 1), F32),
                        pltpu.VMEM((2, tq, 2 * v_dim), F32)],
        compiler_params=_cparams(("parallel", "parallel", "parallel", "arbitrary")),
        name="mla_prompt",
    )(q, k, v)


def _q_sample_kernel(x_ref, wc_ref, wm_ref, g_ref, wn_ref, wr_ref, wrs_ref, wuk_ref, cos_ref, sin_ref,
                     ql_ref, qr_ref, qm_ref):
    x = x_ref[...].astype(BF16)
    qm_ref[...] = jnp.dot(x, wm_ref[...], preferred_element_type=F32)
    cq = _rms_norm(jnp.dot(x, wc_ref[...], preferred_element_type=F32), g_ref[...]).astype(BF16)
    cos = cos_ref[...]
    sin = sin_ref[...]
    for h in range(wn_ref.shape[0]):
        qn = jnp.dot(cq, wn_ref[h], preferred_element_type=F32).astype(BF16)
        ql_ref[h] = jnp.dot(qn, wuk_ref[h], preferred_element_type=F32).astype(BF16)
        qr = jnp.dot(cq, wr_ref[h], preferred_element_type=F32) * cos
        qr_ref[h] = (qr + jnp.dot(cq, wrs_ref[h], preferred_element_type=F32) * sin).astype(BF16)


def _q_sample(x, wc, wm, g, wn, wr, wrs, wuk_t, cos, sin):
    m, d = x.shape
    heads = wn.shape[0]
    rk = wuk_t.shape[2]
    e = wr.shape[2]
    full = lambda arr: pl.BlockSpec(arr.shape, lambda i: (0,) * arr.ndim)
    return pl.pallas_call(
        _q_sample_kernel,
        grid=(1,),
        in_specs=[full(a) for a in (x, wc, wm, g, wn, wr, wrs, wuk_t, cos, sin)],
        out_specs=[pl.BlockSpec((heads, m, rk), lambda i: (0, 0, 0)),
                   pl.BlockSpec((heads, m, e), lambda i: (0, 0, 0)),
                   pl.BlockSpec((m, wm.shape[1]), lambda i: (0, 0))],
        out_shape=[jax.ShapeDtypeStruct((heads, m, rk), BF16), jax.ShapeDtypeStruct((heads, m, e), BF16),
                   jax.ShapeDtypeStruct((m, wm.shape[1]), F32)],
        compiler_params=_cparams(("arbitrary",)),
        name="q_sample",
    )(x, wc, wm, g, wn, wr, wrs, wuk_t, cos, sin)


def _mla_sample_kernel(pt_ref, ql_ref, qr_ref, *refs, pages, new_len, scale):
    del pt_ref
    ck_refs = refs[:pages]
    kr_refs = refs[pages:2 * pages]
    cn_ref, kn_ref, o_ref, m_ref, l_ref, acc_ref = refs[2 * pages:]
    c = pl.program_id(1)

    @pl.when(c == 0)
    def _():
        m_ref[...] = jnp.full(m_ref.shape, -jnp.inf, F32)
        l_ref[...] = jnp.zeros(l_ref.shape, F32)
        acc_ref[...] = jnp.zeros(acc_ref.shape, F32)

    ql = ql_ref[0]
    qr = qr_ref[0]

    def attend(kv, kk, visible):
        s = lax.dot_general(ql, kv, _NT, preferred_element_type=F32)
        s = (s + lax.dot_general(qr, kk, _NT, preferred_element_type=F32)) * scale
        if visible is not None:
            s = jnp.where(visible, s, MASK_VALUE)
        m_old = m_ref[...]
        m_new = jnp.maximum(m_old, jnp.max(s, axis=-1, keepdims=True))
        a = jnp.exp(m_old - m_new)
        p = jnp.exp(s - m_new)
        l_ref[...] = a * l_ref[...] + jnp.sum(p, axis=-1, keepdims=True)
        acc_ref[...] = a * acc_ref[...] + jnp.dot(p.astype(BF16), kv, preferred_element_type=F32)
        m_ref[...] = m_new

    attend(jnp.concatenate([r[0].astype(BF16) for r in ck_refs], axis=0),
           jnp.concatenate([r[0].astype(BF16) for r in kr_refs], axis=0), None)

    @pl.when(c == pl.num_programs(1) - 1)
    def _():
        nq, nk = ql.shape[0], cn_ref.shape[1]
        s_of_row = lax.broadcasted_iota(I32, (nq, nk), 0) % new_len
        j = lax.broadcasted_iota(I32, (nq, nk), 1)
        attend(cn_ref[0].astype(BF16), kn_ref[0].astype(BF16), j <= s_of_row)
        o_ref[0] = acc_ref[...] / l_ref[...]


def _mla_sample(page_table, ql, qr, cache_c, cache_kr, c_new, kr_new, new_len, scale):
    bs, n_pages = page_table.shape
    _, nq, rk = ql.shape
    e = qr.shape[2]
    page = cache_c.shape[1]
    pages = math.gcd(n_pages, PAGES_PER_STEP)
    steps = n_pages // pages
    pt = page_table.reshape(-1)

    def page_spec(width, j):
        return pl.BlockSpec((1, page, width), lambda b, c, pt: (pt[b * n_pages + c * pages + j], 0, 0))

    per_b = lambda n, w: pl.BlockSpec((1, n, w), lambda b, c, pt: (b, 0, 0))
    s8 = c_new.shape[1]
    return pl.pallas_call(
        functools.partial(_mla_sample_kernel, pages=pages, new_len=new_len, scale=scale),
        grid_spec=pltpu.PrefetchScalarGridSpec(
            num_scalar_prefetch=1,
            grid=(bs, steps),
            in_specs=([per_b(nq, rk), per_b(nq, e)]
                      + [page_spec(rk, j) for j in range(pages)]
                      + [page_spec(e, j) for j in range(pages)]
                      + [per_b(s8, rk), per_b(s8, e)]),
            out_specs=per_b(nq, rk),
            scratch_shapes=[pltpu.VMEM((nq, 1), F32), pltpu.VMEM((nq, 1), F32), pltpu.VMEM((nq, rk), F32)]),
        out_shape=jax.ShapeDtypeStruct((bs, nq, rk), F32),
        compiler_params=_cparams(("parallel", "arbitrary")),
        name="mla_sample",
    )(pt, ql, qr, *([cache_c] * pages), *([cache_kr] * pages), c_new, kr_new)


def _head_up_kernel(x_ref, w_ref, o_ref):
    o_ref[0] = jnp.dot(x_ref[0].astype(BF16), w_ref[0], preferred_element_type=F32)


def _head_up(lat, wv):
    h, m, rk = lat.shape
    v = wv.shape[2]
    return pl.pallas_call(
        _head_up_kernel,
        grid=(h,),
        in_specs=[pl.BlockSpec((1, m, rk), lambda i: (i, 0, 0)), pl.BlockSpec((1, rk, v), lambda i: (i, 0, 0))],
        out_specs=pl.BlockSpec((1, m, v), lambda i: (i, 0, 0)),
        out_shape=jax.ShapeDtypeStruct((h, m, v), F32),
        compiler_params=_cparams(("parallel",)),
        name="head_up",
    )(lat, wv)


def _head_slots(w, heads, width):
    k = w.shape[0]
    w = w.reshape(k, heads, width)
    return jnp.pad(w, ((0, 0), (0, 0), (0, LANES - width))).reshape(k, heads * LANES)


def kernel(x_prompt, x_sample, cache_kv_latent, cache_k_rope, cache_mem_k, cache_mem_v, state_conv, page_table, mem_prompt, w_in_a, conv_w, w_out_a, w_in_b, g_q, w_uq, w_out_b, w_kv_down, g_kv, w_uk, w_uv, w_mem_k, w_mem_v, ln1_g, ln1_b, ln2_g, ln2_b, w_router, b_router, w_gate, b_gate, w_up, b_up, w_down, b_down):
    bp, sp, d = x_prompt.shape
    bs, ss, _ = x_sample.shape
    depth = ln1_g.shape[0]
    n_a = w_in_a.shape[0]
    d_conv = conv_w.shape[2]
    assert conv_w.shape[1] == 3
    n_mem = mem_prompt.shape[1]
    mem_heads = cache_mem_k.shape[3]
    mem_w = mem_heads * cache_mem_k.shape[4]
    kv_rank, heads, nope = w_uk.shape
    v_dim = w_uv.shape[2]
    rope_dim = cache_k_rope.shape[2]
    q_rank = g_q.shape[1]
    n_past = page_table.shape[1] * cache_kv_latent.shape[1]
    alpha = (2 * depth) ** 0.25
    scale = (nope + rope_dim) ** -0.5
    half = rope_dim // 2

    xp = x_prompt.reshape(bp * sp, d)
    xs = x_sample.reshape(bs * ss, d)

    w_mem = jnp.concatenate([w_mem_k[l] for l in range(depth)] + [w_mem_v[l] for l in range(depth)],
                            axis=1).astype(BF16)
    mem_k_flat, mem_v_flat = _mem_kv(mem_prompt.reshape(bp * n_mem, d), w_mem, depth)
    mem_k_p = mem_k_flat.reshape(depth, bp, n_mem, mem_w)
    mem_v_p = mem_v_flat.reshape(depth, bp, n_mem, mem_w)
    mem_k_s = cache_mem_k.reshape(depth, bs, n_mem, mem_w)
    mem_v_s = cache_mem_v.reshape(depth, bs, n_mem, mem_w)

    conv_p, conv_s = [], []
    c_p = kr_p = c_s = kr_s = None
    k_p = v_p = None
    tables = None

    for l in range(depth):
        g1, b1 = ln1_g[l].reshape(1, d), ln1_b[l].reshape(1, d)
        if l < n_a:
            w_in = w_in_a[l].astype(BF16)
            w_out = w_out_a[l].astype(BF16)
            splits = [d_conv, d_conv, d_conv, mem_w]
            gb, gc, hv, qm = _proj(xp, w_in, splits, "in_proj_a")
            y, st = _conv_seq(gb, gc, hv, jnp.zeros((bp, 2, d_conv), F32), conv_w[l], bp)
            conv_p.append(st)
            ym = _mem_attn(qm.reshape(bp, sp, mem_w), mem_k_p[l], mem_v_p[l], mem_heads).reshape(bp * sp, mem_w)
            xp = _out_ln(y, ym, xp, w_out[:d_conv], w_out[d_conv:], g1, b1, alpha)

            gb, gc, hv, qm = _proj(xs, w_in, splits, "in_proj_a")
            r3 = lambda a: a.reshape(bs, ss, d_conv)
            y, st = _conv_short(r3(gb), r3(gc), r3(hv), state_conv[l], conv_w[l])
            conv_s.append(st)
            ym = _mem_attn(qm.reshape(bs, ss, mem_w), mem_k_s[l], mem_v_s[l], mem_heads).reshape(bs * ss, mem_w)
            xs = _out_ln(y.reshape(bs * ss, d_conv), ym, xs, w_out[:d_conv], w_out[d_conv:], g1, b1, alpha)
        else:
            j = l - n_a
            if tables is None:
                cos_p, sin_p = _rope_table(0, sp, rope_dim)
                cos_s, sin_s = _rope_table(n_past, ss, rope_dim)
                cos_s, sin_s = jnp.tile(cos_s, (bs, 1)), jnp.tile(sin_s, (bs, 1))
                pad = LANES - nope - rope_dim
                cos_slot = jnp.concatenate([jnp.ones((sp, nope), F32), cos_p, jnp.zeros((sp, pad), F32)], axis=1)
                sin_slot = jnp.concatenate([jnp.zeros((sp, nope), F32), sin_p, jnp.zeros((sp, pad), F32)], axis=1)
                tables = True
                wc = w_kv_down[:, :kv_rank].astype(BF16)
                wr = w_kv_down[:, kv_rank:]
                wrs = _swap_halves_signed(wr).astype(BF16)
                wr = wr.astype(BF16)
                gk = g_kv.reshape(1, kv_rank)
                c_p, kr_p = _shared_kv(xp, wc, wr, wrs, gk, cos_p, sin_p)
                c_s, kr_s = _shared_kv(xs, wc, wr, wrs, gk, cos_s, sin_s)
                wk_pad = _head_slots(w_uk.reshape(kv_rank, heads * nope), heads, nope).astype(BF16)
                rep = jnp.pad(jnp.eye(rope_dim, dtype=F32), ((0, 0), (nope, pad)))
                rep = jnp.tile(rep, (1, heads)).astype(BF16)
                k_p, v_p = _kv_up(c_p, kr_p, wk_pad, rep, w_uv.reshape(kv_rank, heads * v_dim).astype(BF16))
                pad_rows = -ss % SUBLANES
                c_new = jnp.pad(c_s.reshape(bs, ss, kv_rank), ((0, 0), (0, pad_rows), (0, 0)))
                kr_new = jnp.pad(kr_s.reshape(bs, ss, rope_dim), ((0, 0), (0, pad_rows), (0, 0)))
                wuk_t = jnp.transpose(w_uk, (1, 2, 0)).astype(BF16)
                wuv_h = jnp.transpose(w_uv, (1, 0, 2)).astype(BF16)

            w_in = w_in_b[j]
            wcq = w_in[:, :q_rank].astype(BF16)
            wqm = w_in[:, q_rank:].astype(BF16)
            gq = g_q[j].reshape(1, q_rank)
            wq = w_uq[j].reshape(q_rank, heads, nope + rope_dim)
            wq_nope = wq[:, :, :nope]
            wq_rope = wq[:, :, nope:]
            wq_rope_sw = _swap_halves_signed(wq_rope)
            w_out = w_out_b[j].astype(BF16)
            hv_w = heads * v_dim

            slot = lambda a, b: _head_slots(jnp.concatenate([a, b], axis=2).reshape(q_rank, -1), heads,
                                            nope + rope_dim).astype(BF16)
            wq_pad = slot(wq_nope, wq_rope)
            wqs_pad = slot(jnp.zeros_like(wq_nope), wq_rope_sw)
            q, qm = _q_prompt(xp, wcq, wqm, gq, wq_pad, wqs_pad, cos_slot, sin_slot)
            att = _mla_prompt(q, k_p, v_p, bp, scale)
            ym = _mem_attn(qm.reshape(bp, sp, mem_w), mem_k_p[l], mem_v_p[l], mem_heads).reshape(bp * sp, mem_w)
            xp = _out_ln(att, ym, xp, w_out[:hv_w], w_out[hv_w:], g1, b1, alpha)

            th = lambda a: jnp.transpose(a, (1, 0, 2)).astype(BF16)
            ql, qr, qm = _q_sample(xs, wcq, wqm, gq, th(wq_nope), th(wq_rope), th(wq_rope_sw), wuk_t,
                                   cos_s, sin_s)
            per_req = lambda a: jnp.transpose(a.reshape(heads, bs, ss, -1), (1, 0, 2, 3)).reshape(bs, heads * ss, -1)
            lat = _mla_sample(page_table, per_req(ql), per_req(qr), cache_kv_latent, cache_k_rope,
                              c_new, kr_new, ss, scale)
            lat = jnp.transpose(lat.reshape(bs, heads, ss, kv_rank), (1, 0, 2, 3)).reshape(heads, bs * ss, kv_rank)
            att = jnp.transpose(_head_up(lat, wuv_h), (1, 0, 2)).reshape(bs * ss, hv_w)
            ym = _mem_attn(qm.reshape(bs, ss, mem_w), mem_k_s[l], mem_v_s[l], mem_heads).reshape(bs * ss, mem_w)
            xs = _out_ln(att, ym, xs, w_out[:hv_w], w_out[hv_w:], g1, b1, alpha)

        xp, xs = _moe_ln([xp, xs], w_router[l].T, b_router[l].reshape(-1, 1), w_gate[l], b_gate[l],
                         w_up[l], b_up[l], w_down[l], b_down[l],
                         ln2_g[l].reshape(1, d), ln2_b[l].reshape(1, d), alpha)

    ne_shape = (depth, bp, n_mem, mem_heads, mem_w // mem_heads)
    return (xp.reshape(bp, sp, d), xs.reshape(bs, ss, d),
            c_p.reshape(bp, sp, kv_rank), kr_p.reshape(bp, sp, rope_dim),
            c_s.reshape(bs, ss, kv_rank), kr_s.reshape(bs, ss, rope_dim),
            mem_k_p.reshape(ne_shape), mem_v_p.reshape(ne_shape),
            jnp.stack(conv_p), jnp.stack(conv_s))
```

```python
import functools
import math

import jax
import jax.numpy as jnp
from jax import lax
from jax.experimental import pallas as pl
from jax.experimental.pallas import tpu as pltpu

F32 = jnp.float32
BF16 = jnp.bfloat16
I32 = jnp.int32
U32 = jnp.uint32

TOP_K = 4
SWIGLU_LIMIT = 7.0
SWIGLU_ALPHA = 1.702
ROPE_THETA = 10000.0
LN_EPS = 1e-5
RMS_EPS = 1e-6
MASK_VALUE = -1e30

LANES = 128
SUBLANES = 8
VMEM_LIMIT = 56 * 1024 * 1024

ROW_TILE = 512
MOE_TILE = 256
EXPERT_ROWS = 512
PAGES_PER_STEP = 16
HI_MASK = 0xFFFF0000
NUM_EXPERTS = 32
NUM_EXPERTS_PAD_ROWS = NUM_EXPERTS * SUBLANES

_NT = (((1,), (1,)), ((), ()))


def _cparams(sem):
    return pltpu.CompilerParams(dimension_semantics=sem, vmem_limit_bytes=VMEM_LIMIT)


def _tile(n, pref):
    t = min(n, pref)
    assert n % t == 0, (n, pref)
    return t


def _pack_bf16_pairs(x):
    half = x.shape[1] // 2
    bits = pltpu.bitcast(x, U32)
    return (bits[:, :half] >> 16) | (bits[:, half:] & jnp.uint32(HI_MASK))


def _unpack_bf16_pairs(p):
    lo = pltpu.bitcast(p << 16, F32)
    hi = pltpu.bitcast(p & jnp.uint32(HI_MASK), F32)
    return jnp.concatenate([lo, hi], axis=1).astype(BF16)


def _layer_norm(z, g, b):
    mu = jnp.mean(z, axis=-1, keepdims=True)
    zc = z - mu
    var = jnp.mean(zc * zc, axis=-1, keepdims=True)
    return zc * lax.rsqrt(var + LN_EPS) * g + b


def _rms_norm(x, g):
    return x * lax.rsqrt(jnp.mean(x * x, axis=-1, keepdims=True) + RMS_EPS) * g


def _proj_kernel(x_ref, w_ref, *o_refs):
    h = jnp.dot(x_ref[...].astype(BF16), w_ref[...], preferred_element_type=F32)
    off = 0
    for o_ref in o_refs:
        n = o_ref.shape[-1]
        o_ref[...] = h[:, off:off + n].astype(o_ref.dtype)
        off += n


def _proj(x, w, splits, name):
    m, k = x.shape
    n = w.shape[1]
    assert sum(splits) == n
    tm = _tile(m, ROW_TILE)
    return pl.pallas_call(
        _proj_kernel,
        grid=(m // tm,),
        in_specs=[pl.BlockSpec((tm, k), lambda i: (i, 0)), pl.BlockSpec((k, n), lambda i: (0, 0))],
        out_specs=[pl.BlockSpec((tm, s), lambda i: (i, 0)) for s in splits],
        out_shape=[jax.ShapeDtypeStruct((m, s), F32) for s in splits],
        compiler_params=_cparams(("parallel",)),
        name=name,
    )(x, w)


def _memkv_kernel(x_ref, w_ref, k_ref, v_ref):
    depth, _, f = k_ref.shape
    h = jnp.dot(x_ref[...].astype(BF16), w_ref[...], preferred_element_type=F32)
    for l in range(depth):
        k_ref[l] = h[:, l * f:(l + 1) * f]
        v_ref[l] = h[:, (depth + l) * f:(depth + l + 1) * f]


def _mem_kv(mem, w_cat, depth):
    r, d = mem.shape
    f = w_cat.shape[1] // (2 * depth)
    tm = _tile(r, ROW_TILE)
    shp = jax.ShapeDtypeStruct((depth, r, f), F32)
    spec = pl.BlockSpec((depth, tm, f), lambda i: (0, i, 0))
    return pl.pallas_call(
        _memkv_kernel,
        grid=(r // tm,),
        in_specs=[pl.BlockSpec((tm, d), lambda i: (i, 0)), pl.BlockSpec(w_cat.shape, lambda i: (0, 0))],
        out_specs=[spec, spec],
        out_shape=[shp, shp],
        compiler_params=_cparams(("parallel",)),
        name="mem_kv",
    )(mem, w_cat)


def _conv_seq_kernel(gb_ref, gc_ref, hv_ref, pre_ref, w_ref, y_ref, st_ref, carry):
    j = pl.program_id(1)

    @pl.when(j == 0)
    def _():
        carry[0:2, :] = pre_ref[0]

    u = gc_ref[...] * hv_ref[...]
    ts = u.shape[0]
    c0 = carry[0:1, :]
    c1 = carry[1:2, :]
    row = lax.broadcasted_iota(I32, u.shape, 0)
    u1 = jnp.where(row == 0, c1, pltpu.roll(u, 1, 0))
    u2 = jnp.where(row == 0, c0, jnp.where(row == 1, c1, pltpu.roll(u, 2, 0)))
    w = w_ref[...]
    y_ref[...] = gb_ref[...] * (w[0:1] * u2 + w[1:2] * u1 + w[2:3] * u)
    last = u[ts - 2:ts, :]
    carry[0:2, :] = last
    st_ref[0] = last


def _conv_seq(gb, gc, hv, prefix, w, batch):
    m, c = gb.shape
    s = m // batch
    ts = _tile(s, ROW_TILE)
    nt = s // ts
    row = pl.BlockSpec((ts, c), lambda b, j: (b * nt + j, 0))
    return pl.pallas_call(
        _conv_seq_kernel,
        grid=(batch, nt),
        in_specs=[row, row, row,
                  pl.BlockSpec((1, 2, c), lambda b, j: (b, 0, 0)),
                  pl.BlockSpec((3, c), lambda b, j: (0, 0))],
        out_specs=[row, pl.BlockSpec((1, 2, c), lambda b, j: (b, 0, 0))],
        out_shape=[jax.ShapeDtypeStruct((m, c), F32), jax.ShapeDtypeStruct((batch, 2, c), F32)],
        scratch_shapes=[pltpu.VMEM((SUBLANES, c), F32)],
        compiler_params=_cparams(("parallel", "arbitrary")),
        name="conv_seq",
    )(gb, gc, hv, prefix, w)


def _conv_short_kernel(gb_ref, gc_ref, hv_ref, pre_ref, w_ref, y_ref, st_ref):
    s_len = gb_ref.shape[1]
    w = w_ref[...]

    def u(k):
        if k < 0:
            return pre_ref[:, 2 + k, :]
        return gc_ref[:, k, :] * hv_ref[:, k, :]

    for s in range(s_len):
        y_ref[:, s, :] = gb_ref[:, s, :] * (w[0:1] * u(s - 2) + w[1:2] * u(s - 1) + w[2:3] * u(s))
    for k in range(2):
        st_ref[:, k, :] = u(s_len - 2 + k)


def _conv_short(gb, gc, hv, prefix, w):
    batch, s, c = gb.shape
    full = lambda shp: pl.BlockSpec(shp, lambda i: (0,) * len(shp))
    return pl.pallas_call(
        _conv_short_kernel,
        grid=(1,),
        in_specs=[full(gb.shape)] * 3 + [full(prefix.shape), full(w.shape)],
        out_specs=[full(gb.shape), full(prefix.shape)],
        out_shape=[jax.ShapeDtypeStruct(gb.shape, F32), jax.ShapeDtypeStruct(prefix.shape, F32)],
        compiler_params=_cparams(("arbitrary",)),
        name="conv_short",
    )(gb, gc, hv, prefix, w)


def _mem_attn_kernel(q_ref, k_ref, v_ref, o_ref, *, heads):
    q = q_ref[0].astype(BF16)
    k = k_ref[0, 0].astype(BF16)
    v = v_ref[0, 0].astype(BF16)
    hd = q.shape[1] // heads
    outs = []
    for h in range(heads):
        sl = slice(h * hd, (h + 1) * hd)
        s = lax.dot_general(q[:, sl], k[:, sl], _NT, preferred_element_type=F32) * (hd ** -0.5)
        m = jnp.max(s, axis=-1, keepdims=True)
        p = jnp.exp(s - m)
        l = jnp.sum(p, axis=-1, keepdims=True)
        outs.append(jnp.dot(p.astype(BF16), v[:, sl], preferred_element_type=F32) / l)
    o_ref[0] = jnp.concatenate(outs, axis=-1)


def _mem_attn(q, mk, mv, layer, heads):
    b, s, w = q.shape
    nm = mk.shape[2]
    tq = _tile(s, ROW_TILE)
    return pl.pallas_call(
        functools.partial(_mem_attn_kernel, heads=heads),
        grid=(b, s // tq),
        in_specs=[pl.BlockSpec((1, tq, w), lambda i, j: (i, j, 0)),
                  pl.BlockSpec((1, 1, nm, w), lambda i, j: (layer, i, 0, 0)),
                  pl.BlockSpec((1, 1, nm, w), lambda i, j: (layer, i, 0, 0))],
        out_specs=pl.BlockSpec((1, tq, w), lambda i, j: (i, j, 0)),
        out_shape=jax.ShapeDtypeStruct((b, s, w), F32),
        compiler_params=_cparams(("parallel", "parallel")),
        name="mem_attn",
    )(q, mk, mv)


def _out_ln_kernel(a_ref, m_ref, x_ref, wa_ref, wm_ref, g_ref, b_ref, o_ref, *, alpha):
    o = jnp.dot(a_ref[...].astype(BF16), wa_ref[...], preferred_element_type=F32)
    o = o + jnp.dot(m_ref[...].astype(BF16), wm_ref[...], preferred_element_type=F32)
    o_ref[...] = _layer_norm(alpha * x_ref[...] + o, g_ref[...], b_ref[...])


def _out_ln(a, mem, x, wa, wm, g, b, alpha):
    m, d = x.shape
    tm = _tile(m, ROW_TILE)
    row = lambda n: pl.BlockSpec((tm, n), lambda i: (i, 0))
    full = lambda arr: pl.BlockSpec(arr.shape, lambda i: (0, 0))
    return pl.pallas_call(
        functools.partial(_out_ln_kernel, alpha=alpha),
        grid=(m // tm,),
        in_specs=[row(a.shape[1]), row(mem.shape[1]), row(d), full(wa), full(wm), full(g), full(b)],
        out_specs=row(d),
        out_shape=jax.ShapeDtypeStruct((m, d), F32),
        compiler_params=_cparams(("parallel",)),
        name="out_ln",
    )(a, mem, x, wa, wm, g, b)


def _router_kernel(x_ref, wt_ref, b_ref, pos_ref, gate_ref, cnt_ref):
    x = x_ref[...]
    logits = lax.dot_general(wt_ref[...], x, _NT, precision=lax.Precision.HIGHEST,
                             preferred_element_type=F32) + b_ref[...]
    ne, tt = logits.shape
    eidx = lax.broadcasted_iota(I32, (ne, tt), 0)
    sels, vals = [], []
    l = logits
    for _ in range(TOP_K):
        m = jnp.max(l, axis=0, keepdims=True)
        first = jnp.min(jnp.where(l == m, eidx, ne), axis=0, keepdims=True)
        sel = eidx == first
        sels.append(sel)
        vals.append(m)
        l = jnp.where(sel, -jnp.inf, l)
    ex = [jnp.exp(v - vals[0]) for v in vals]
    den = ex[0]
    for e in ex[1:]:
        den = den + e
    gate_ref[...] = jnp.concatenate([e / den for e in ex], axis=0)

    chosen = jnp.zeros((ne, tt), F32)
    for sel in sels:
        chosen = jnp.where(sel, 1.0, chosen)
    cnt = jnp.sum(chosen, axis=1, keepdims=True)
    cnt_ref[0] = cnt.astype(I32)
    cnt8 = jnp.floor((cnt + (SUBLANES - 1)) * (1.0 / SUBLANES)) * SUBLANES
    er = lax.broadcasted_iota(I32, (ne, ne), 0)
    ec = lax.broadcasted_iota(I32, (ne, ne), 1)
    below = jnp.where(ec < er, 1.0, 0.0)
    seg_start = jnp.dot(below, jnp.broadcast_to(cnt8, (ne, LANES)), precision=lax.Precision.HIGHEST,
                        preferred_element_type=F32)[:, 0:1]
    tr = lax.broadcasted_iota(I32, (tt, tt), 0)
    tc = lax.broadcasted_iota(I32, (tt, tt), 1)
    earlier = jnp.where(tr < tc, 1.0, 0.0).astype(BF16)
    rank = jnp.dot(chosen.astype(BF16), earlier, preferred_element_type=F32)
    slot = seg_start + rank
    pos_ref[...] = jnp.concatenate(
        [jnp.sum(jnp.where(sel, slot, 0.0), axis=0, keepdims=True) for sel in sels], axis=0).astype(I32)


def _router(x, wt, b, tt):
    m, d = x.shape
    ne = wt.shape[0]
    nt = m // tt
    return pl.pallas_call(
        _router_kernel,
        grid=(nt,),
        in_specs=[pl.BlockSpec((tt, d), lambda i: (i, 0)),
                  pl.BlockSpec((ne, d), lambda i: (0, 0)),
                  pl.BlockSpec((ne, 1), lambda i: (0, 0))],
        out_specs=[pl.BlockSpec((TOP_K, tt), lambda i: (0, i)),
                   pl.BlockSpec((TOP_K, tt), lambda i: (0, i)),
                   pl.BlockSpec((1, ne, 1), lambda i: (i, 0, 0))],
        out_shape=[jax.ShapeDtypeStruct((TOP_K, m), I32),
                   jax.ShapeDtypeStruct((TOP_K, m), F32),
                   jax.ShapeDtypeStruct((nt, ne, 1), I32)],
        compiler_params=_cparams(("parallel",)),
        name="router",
    )(x, wt, b)


def _for_each_segment(n_ref, local_ref, global_ref, tile, ne, fn):
    def body(e, c):
        k = tile * ne + e
        n = pl.multiple_of(n_ref[k], SUBLANES)

        @pl.when(n > 0)
        def _():
            fn(n, pl.multiple_of(local_ref[k], SUBLANES), pl.multiple_of(global_ref[k], SUBLANES))
        return c

    lax.fori_loop(0, ne, body, 0)


def _dispatch_kernel(n_ref, local_ref, global_ref, x_ref, slot_ref, init_ref, xs_ref, buf, sem, *, tile0, ne):
    del init_ref
    step = pl.program_id(0)
    tile = step + tile0
    cur = step % 2
    x = x_ref[...].astype(BF16)
    slot = slot_ref[...]
    rows = buf.shape[1]
    r = lax.broadcasted_iota(I32, (rows, x.shape[0]), 0)
    onehot = jnp.zeros(r.shape, F32)
    for k in range(TOP_K):
        onehot = jnp.where(slot[k:k + 1, :] == r, 1.0, onehot)
    buf[cur] = _pack_bf16_pairs(jnp.dot(onehot.astype(BF16), x, preferred_element_type=F32))

    def copy(b):
        return lambda n, lo, go: pltpu.make_async_copy(buf.at[b, pl.ds(lo, n)], xs_ref.at[pl.ds(go, n)], sem.at[b])

    _for_each_segment(n_ref, local_ref, global_ref, tile, ne, lambda *a: copy(cur)(*a).start())

    @pl.when(step > 0)
    def _():
        _for_each_segment(n_ref, local_ref, global_ref, tile - 1, ne, lambda *a: copy(1 - cur)(*a).wait())

    @pl.when(step == pl.num_programs(0) - 1)
    def _():
        _for_each_segment(n_ref, local_ref, global_ref, tile, ne, lambda *a: copy(cur)(*a).wait())


def _dispatch(seg, x, slot, xs, tile0, tt):
    m, d = x.shape
    n8, local, glob = seg
    rows = TOP_K * tt + NUM_EXPERTS_PAD_ROWS
    return pl.pallas_call(
        functools.partial(_dispatch_kernel, tile0=tile0, ne=NUM_EXPERTS),
        grid_spec=pltpu.PrefetchScalarGridSpec(
            num_scalar_prefetch=3,
            grid=(m // tt,),
            in_specs=[pl.BlockSpec((tt, d), lambda i, *_: (i, 0)),
                      pl.BlockSpec((TOP_K, tt), lambda i, *_: (0, i)),
                      pl.BlockSpec(memory_space=pl.ANY)],
            out_specs=pl.BlockSpec(memory_space=pl.ANY),
            scratch_shapes=[pltpu.VMEM((2, rows, d // 2), U32), pltpu.SemaphoreType.DMA((2,))]),
        out_shape=jax.ShapeDtypeStruct(xs.shape, U32),
        input_output_aliases={5: 0},
        compiler_params=_cparams(("arbitrary",)),
        name="moe_dispatch",
    )(n8, local, glob, x, slot, xs)


def _expert_kernel(be_ref, nu_ref, xs_ref, wg_ref, bg_ref, wu_ref, bu_ref, wd_ref, bd_ref, ys_ref,
                   wg_bf, wu_bf, wd_bf):
    b = pl.program_id(0)

    @pl.when(b < nu_ref[0])
    def _():
        prev = be_ref[jnp.maximum(b - 1, 0)]

        @pl.when((b == 0) | (be_ref[b] != prev))
        def _():
            wg_bf[...] = wg_ref[0].astype(BF16)
            wu_bf[...] = wu_ref[0].astype(BF16)
            wd_bf[...] = wd_ref[0].astype(BF16)

        x = _unpack_bf16_pairs(xs_ref[...])
        g = jnp.dot(x, wg_bf[...], preferred_element_type=F32) + bg_ref[0]
        u = jnp.dot(x, wu_bf[...], preferred_element_type=F32) + bu_ref[0]
        g = jnp.minimum(g, SWIGLU_LIMIT)
        u = jnp.clip(u, -SWIGLU_LIMIT, SWIGLU_LIMIT)
        h = (u + 1.0) * (g * jax.nn.sigmoid(SWIGLU_ALPHA * g))
        y = jnp.dot(h.astype(BF16), wd_bf[...], preferred_element_type=F32) + bd_ref[0]
        ys_ref[...] = _pack_bf16_pairs(y.astype(BF16).astype(F32))

    @pl.when(b >= nu_ref[0])
    def _():
        ys_ref[...] = jnp.zeros(ys_ref.shape, U32)


def _experts(block_expert, n_used, xs, layer, wg, bg, wu, bu, wd, bd):
    rows, half = xs.shape
    depth, ne, d, f = wg.shape
    nb = rows // EXPERT_ROWS
    blk = lambda b, be, nu: (jnp.minimum(b, nu[0] - 1), 0)
    wsel = lambda b, be, nu: (layer * ne + be[b], 0, 0)
    stack = lambda w: w.reshape((depth * ne,) + w.shape[2:])
    bias = lambda v: v.reshape(depth * ne, 1, v.shape[2])
    return pl.pallas_call(
        _expert_kernel,
        grid_spec=pltpu.PrefetchScalarGridSpec(
            num_scalar_prefetch=2,
            grid=(nb,),
            in_specs=[pl.BlockSpec((EXPERT_ROWS, half), blk),
                      pl.BlockSpec((1, d, f), wsel), pl.BlockSpec((1, 1, f), wsel),
                      pl.BlockSpec((1, d, f), wsel), pl.BlockSpec((1, 1, f), wsel),
                      pl.BlockSpec((1, f, d), wsel), pl.BlockSpec((1, 1, d), wsel)],
            out_specs=pl.BlockSpec((EXPERT_ROWS, half), lambda b, be, nu: (b, 0)),
            scratch_shapes=[pltpu.VMEM((d, f), BF16), pltpu.VMEM((d, f), BF16), pltpu.VMEM((f, d), BF16)]),
        out_shape=jax.ShapeDtypeStruct((rows, half), U32),
        compiler_params=_cparams(("arbitrary",)),
        name="moe_experts",
    )(block_expert, n_used, xs, stack(wg), bias(bg), stack(wu), bias(bu), stack(wd), bias(bd))


def _combine_kernel(n_ref, local_ref, global_ref, ys_ref, slot_ref, gate_ref, x_ref, g_ref, b_ref, o_ref,
                    buf, sem, *, tile0, ne, alpha):
    step = pl.program_id(0)
    tile = step + tile0
    cur = step % 2

    def copy(b):
        return lambda n, lo, go: pltpu.make_async_copy(ys_ref.at[pl.ds(go, n)], buf.at[b, pl.ds(lo, n)], sem.at[b])

    def fetch(t, b):
        buf[b] = jnp.zeros(buf.shape[1:], U32)
        _for_each_segment(n_ref, local_ref, global_ref, t, ne, lambda *a: copy(b)(*a).start())

    @pl.when(step == 0)
    def _():
        fetch(tile, cur)

    @pl.when(step < pl.num_programs(0) - 1)
    def _():
        fetch(tile + 1, 1 - cur)

    slot = slot_ref[...]
    gate = gate_ref[...]
    rows = buf.shape[1]
    r = lax.broadcasted_iota(I32, (slot.shape[0], rows), 1)
    weights = jnp.zeros(r.shape, F32)
    for k in range(TOP_K):
        weights = jnp.where(slot[:, k:k + 1] == r, gate[:, k:k + 1], weights)
    _for_each_segment(n_ref, local_ref, global_ref, tile, ne, lambda *a: copy(cur)(*a).wait())
    y = jnp.dot(weights.astype(BF16), _unpack_bf16_pairs(buf[cur]), preferred_element_type=F32)
    o_ref[...] = _layer_norm(alpha * x_ref[...] + y, g_ref[...], b_ref[...])


def _combine(seg, ys, slot_t, gate_t, x, g, b, tile0, tt, alpha):
    m, d = x.shape
    n8, local, glob = seg
    rows = TOP_K * tt + NUM_EXPERTS_PAD_ROWS
    return pl.pallas_call(
        functools.partial(_combine_kernel, tile0=tile0, ne=NUM_EXPERTS, alpha=alpha),
        grid_spec=pltpu.PrefetchScalarGridSpec(
            num_scalar_prefetch=3,
            grid=(m // tt,),
            in_specs=[pl.BlockSpec(memory_space=pl.ANY),
                      pl.BlockSpec((tt, TOP_K), lambda i, *_: (i, 0)),
                      pl.BlockSpec((tt, TOP_K), lambda i, *_: (i, 0)),
                      pl.BlockSpec((tt, d), lambda i, *_: (i, 0)),
                      pl.BlockSpec((1, d), lambda i, *_: (0, 0)),
                      pl.BlockSpec((1, d), lambda i, *_: (0, 0))],
            out_specs=pl.BlockSpec((tt, d), lambda i, *_: (i, 0)),
            scratch_shapes=[pltpu.VMEM((2, rows, d // 2), U32), pltpu.SemaphoreType.DMA((2,))]),
        out_shape=jax.ShapeDtypeStruct((m, d), F32),
        compiler_params=_cparams(("arbitrary",)),
        name="moe_combine",
    )(n8, local, glob, ys, slot_t, gate_t, x, g, b)


def _moe_ln(xs_groups, layer, wt, br, wg, bg, wu, bu, wd, bd, g, b, alpha):
    ne = wt.shape[0]
    assert ne == NUM_EXPERTS
    d = xs_groups[0].shape[1]
    tiles = [_tile(x.shape[0], MOE_TILE) for x in xs_groups]
    routed = [_router(x, wt, br, tt) for x, tt in zip(xs_groups, tiles)]
    cnt = jnp.concatenate([r[2][:, :, 0] for r in routed], axis=0)
    n_tiles = cnt.shape[0]
    n8 = (cnt + (SUBLANES - 1)) // SUBLANES * SUBLANES
    local = jnp.cumsum(n8, axis=1) - n8
    per_expert = jnp.sum(n8, axis=0)
    per_expert_pad = (per_expert + (EXPERT_ROWS - 1)) // EXPERT_ROWS * EXPERT_ROWS
    expert_base = jnp.cumsum(per_expert_pad) - per_expert_pad
    glob = expert_base[None, :] + jnp.cumsum(n8, axis=0) - n8
    seg = tuple(a.reshape(-1).astype(I32) for a in (n8, local, glob))
    max_rows = sum(x.shape[0] * TOP_K + (x.shape[0] // tt) * NUM_EXPERTS_PAD_ROWS
                   for x, tt in zip(xs_groups, tiles)) + ne * (EXPERT_ROWS - 1)
    nb = -(-max_rows // EXPERT_ROWS)
    blocks_per_expert = per_expert_pad // EXPERT_ROWS
    n_used = jnp.sum(blocks_per_expert).astype(I32).reshape(1)
    ends = jnp.cumsum(blocks_per_expert)
    block_expert = jnp.minimum(
        jnp.sum((ends[None, :] <= jnp.arange(nb, dtype=I32)[:, None]).astype(I32), axis=1), ne - 1)

    xs = jnp.zeros((nb * EXPERT_ROWS, d // 2), U32)
    tile0 = 0
    for x, tt, r in zip(xs_groups, tiles, routed):
        xs = _dispatch(seg, x, r[0], xs, tile0, tt)
        tile0 += x.shape[0] // tt
    assert tile0 == n_tiles
    ys = _experts(block_expert, n_used, xs, layer, wg, bg, wu, bu, wd, bd)
    outs = []
    tile0 = 0
    for x, tt, r in zip(xs_groups, tiles, routed):
        outs.append(_combine(seg, ys, r[0].T, r[1].T, x, g, b, tile0, tt, alpha))
        tile0 += x.shape[0] // tt
    return outs


def _rope_table_kernel(freq_ref, cos_ref, sin_ref, *, start):
    pos = (lax.broadcasted_iota(I32, cos_ref.shape, 0) + start).astype(F32)
    ang = pos * freq_ref[...]
    cos_ref[...] = jnp.cos(ang)
    sin_ref[...] = jnp.sin(ang)


def _rope_table(start, n, rope_dim):
    half = rope_dim // 2
    freqs = ROPE_THETA ** (-jnp.arange(half, dtype=F32) / half)
    freq_row = jnp.concatenate([freqs, freqs]).reshape(1, rope_dim)
    rows = -(-n // SUBLANES) * SUBLANES
    shp = jax.ShapeDtypeStruct((rows, rope_dim), F32)
    full = pl.BlockSpec((rows, rope_dim), lambda i: (0, 0))
    cos, sin = pl.pallas_call(
        functools.partial(_rope_table_kernel, start=start),
        grid=(1,),
        in_specs=[pl.BlockSpec((1, rope_dim), lambda i: (0, 0))],
        out_specs=[full, full],
        out_shape=[shp, shp],
        name="rope_table",
    )(freq_row)
    return cos[:n], sin[:n]


def _swap_halves_signed(w):
    half = w.shape[-1] // 2
    return jnp.concatenate([-w[..., half:], w[..., :half]], axis=-1)


def _kv_kernel(x_ref, wc_ref, wr_ref, wrs_ref, g_ref, cos_ref, sin_ref, c_ref, kr_ref):
    x = x_ref[...].astype(BF16)
    c = jnp.dot(x, wc_ref[...], preferred_element_type=F32)
    c_ref[...] = _rms_norm(c, g_ref[...])
    r = jnp.dot(x, wr_ref[...], preferred_element_type=F32)
    rs = jnp.dot(x, wrs_ref[...], preferred_element_type=F32)
    kr_ref[...] = r * cos_ref[...] + rs * sin_ref[...]


def _shared_kv(x, wc, wr, wrs, g, cos, sin):
    m, d = x.shape
    rk = wc.shape[1]
    e = wr.shape[1]
    tm = _tile(min(m, cos.shape[0]), ROW_TILE)
    period = cos.shape[0] // tm
    full = lambda arr: pl.BlockSpec(arr.shape, lambda i: (0, 0))
    tab = pl.BlockSpec((tm, e), lambda i: (i % period, 0))
    return pl.pallas_call(
        _kv_kernel,
        grid=(m // tm,),
        in_specs=[pl.BlockSpec((tm, d), lambda i: (i, 0)), full(wc), full(wr), full(wrs), full(g), tab, tab],
        out_specs=[pl.BlockSpec((tm, rk), lambda i: (i, 0)), pl.BlockSpec((tm, e), lambda i: (i, 0))],
        out_shape=[jax.ShapeDtypeStruct((m, rk), F32), jax.ShapeDtypeStruct((m, e), F32)],
        compiler_params=_cparams(("parallel",)),
        name="shared_kv",
    )(x, wc, wr, wrs, g, cos, sin)


def _kv_up_kernel(c_ref, kr_ref, wk_ref, rep_ref, wv_ref, k_ref, v_ref):
    c = c_ref[...].astype(BF16)
    k = jnp.dot(c, wk_ref[...], preferred_element_type=F32)
    k = k + jnp.dot(kr_ref[...].astype(BF16), rep_ref[...], preferred_element_type=F32)
    k_ref[...] = k.astype(BF16)
    v_ref[...] = jnp.dot(c, wv_ref[...], preferred_element_type=F32).astype(BF16)


def _kv_up(c, kr, wk_pad, rep, wv):
    m, rk = c.shape
    tm = _tile(m, ROW_TILE)
    full = lambda arr: pl.BlockSpec(arr.shape, lambda i: (0, 0))
    row = lambda n: pl.BlockSpec((tm, n), lambda i: (i, 0))
    return pl.pallas_call(
        _kv_up_kernel,
        grid=(m // tm,),
        in_specs=[row(rk), row(kr.shape[1]), full(wk_pad), full(rep), full(wv)],
        out_specs=[row(wk_pad.shape[1]), row(wv.shape[1])],
        out_shape=[jax.ShapeDtypeStruct((m, wk_pad.shape[1]), BF16), jax.ShapeDtypeStruct((m, wv.shape[1]), BF16)],
        compiler_params=_cparams(("parallel",)),
        name="kv_up",
    )(c, kr, wk_pad, rep, wv)


def _q_prompt_kernel(x_ref, wc_ref, wm_ref, g_ref, wq_ref, wqs_ref, cos_ref, sin_ref, q_ref, qm_ref):
    x = x_ref[...].astype(BF16)
    qm_ref[...] = jnp.dot(x, wm_ref[...], preferred_element_type=F32)
    cq = _rms_norm(jnp.dot(x, wc_ref[...], preferred_element_type=F32), g_ref[...]).astype(BF16)
    qa = jnp.dot(cq, wq_ref[...], preferred_element_type=F32)
    qb = jnp.dot(cq, wqs_ref[...], preferred_element_type=F32)
    cos = cos_ref[...]
    sin = sin_ref[...]
    for h in range(q_ref.shape[1] // LANES):
        sl = slice(h * LANES, (h + 1) * LANES)
        q_ref[:, sl] = (qa[:, sl] * cos + qb[:, sl] * sin).astype(BF16)


def _q_prompt(x, wc, wm, g, wq_pad, wqs_pad, cos_slot, sin_slot):
    m, d = x.shape
    tm = _tile(min(m, cos_slot.shape[0]), ROW_TILE)
    period = cos_slot.shape[0] // tm
    full = lambda arr: pl.BlockSpec(arr.shape, lambda i: (0, 0))
    tab = pl.BlockSpec((tm, LANES), lambda i: (i % period, 0))
    row = lambda n: pl.BlockSpec((tm, n), lambda i: (i, 0))
    return pl.pallas_call(
        _q_prompt_kernel,
        grid=(m // tm,),
        in_specs=[row(d), full(wc), full(wm), full(g), full(wq_pad), full(wqs_pad), tab, tab],
        out_specs=[row(wq_pad.shape[1]), row(wm.shape[1])],
        out_shape=[jax.ShapeDtypeStruct((m, wq_pad.shape[1]), BF16), jax.ShapeDtypeStruct((m, wm.shape[1]), F32)],
        compiler_params=_cparams(("parallel",)),
        name="q_prompt",
    )(x, wc, wm, g, wq_pad, wqs_pad, cos_slot, sin_slot)


def _mla_prompt_kernel(q_ref, k_ref, v_ref, o_ref, *, v_dim):
    qi = pl.program_id(2)
    tq = q_ref.shape[0]
    heads = [slice(hh * LANES, (hh + 1) * LANES) for hh in range(2)]
    q = [q_ref[:, sl] for sl in heads]

    def attend(carry, start, visible):
        kb = k_ref[pl.ds(start, tq), :]
        vb = v_ref[pl.ds(start, tq), :]
        out = []
        for hh in range(2):
            m_old, l_old, acc = carry[hh]
            s = lax.dot_general(q[hh], kb[:, heads[hh]], _NT, preferred_element_type=F32)
            if visible is not None:
                s = jnp.where(visible, s, MASK_VALUE)
            m_new = jnp.maximum(m_old, jnp.max(s, axis=-1, keepdims=True))
            a = jnp.exp(m_old - m_new)
            p = jnp.exp(s - m_new)
            out.append((m_new, a * l_old + jnp.sum(p, axis=-1, keepdims=True),
                        a * acc + jnp.dot(p.astype(BF16), vb, preferred_element_type=F32)))
        return tuple(out)

    init = tuple((jnp.full((tq, 1), -jnp.inf, F32), jnp.zeros((tq, 1), F32), jnp.zeros((tq, 2 * v_dim), F32))
                 for _ in range(2))
    carry = lax.fori_loop(0, qi, lambda ki, c: attend(c, pl.multiple_of(ki * tq, tq), None), init)
    row = lax.broadcasted_iota(I32, (tq, tq), 0)
    col = lax.broadcasted_iota(I32, (tq, tq), 1)
    (_, l0, acc0), (_, l1, acc1) = attend(carry, pl.multiple_of(qi * tq, tq), col <= row)
    lane = lax.broadcasted_iota(I32, (tq, 2 * v_dim), 1)
    o_ref[...] = jnp.where(lane < v_dim, acc0 / l0, acc1 / l1).astype(o_ref.dtype)


def _mla_prompt(q, k, v, batch):
    m, hw = q.shape
    heads = hw // LANES
    v_dim = v.shape[1] // heads
    assert heads % 2 == 0 and 2 * v_dim == LANES
    s = m // batch
    tq = _tile(s, ROW_TILE)
    nq = s // tq
    return pl.pallas_call(
        functools.partial(_mla_prompt_kernel, v_dim=v_dim),
        grid=(batch, heads // 2, nq),
        in_specs=[pl.BlockSpec((tq, 2 * LANES), lambda b, h, qi: (b * nq + qi, h)),
                  pl.BlockSpec((s, 2 * LANES), lambda b, h, qi: (b, h)),
                  pl.BlockSpec((s, 2 * v_dim), lambda b, h, qi: (b, h))],
        out_specs=pl.BlockSpec((tq, 2 * v_dim), lambda b, h, qi: (b * nq + qi, h)),
        out_shape=jax.ShapeDtypeStruct((m, heads * v_dim), BF16),
        compiler_params=_cparams(("parallel", "parallel", "arbitrary")),
        name="mla_prompt",
    )(q, k, v)


def _q_sample_kernel(x_ref, wc_ref, wm_ref, g_ref, wn_ref, wr_ref, wrs_ref, wuk_ref, cos_ref, sin_ref,
                     ql_ref, qr_ref, qm_ref):
    x = x_ref[...].astype(BF16)
    qm_ref[...] = jnp.dot(x, wm_ref[...], preferred_element_type=F32)
    cq = _rms_norm(jnp.dot(x, wc_ref[...], preferred_element_type=F32), g_ref[...]).astype(BF16)
    cos = cos_ref[...]
    sin = sin_ref[...]
    for h in range(wn_ref.shape[0]):
        qn = jnp.dot(cq, wn_ref[h], preferred_element_type=F32).astype(BF16)
        ql_ref[h] = jnp.dot(qn, wuk_ref[h], preferred_element_type=F32).astype(BF16)
        qr = jnp.dot(cq, wr_ref[h], preferred_element_type=F32) * cos
        qr_ref[h] = (qr + jnp.dot(cq, wrs_ref[h], preferred_element_type=F32) * sin).astype(BF16)


def _q_sample(x, wc, wm, g, wn, wr, wrs, wuk_t, cos, sin):
    m, d = x.shape
    heads = wn.shape[0]
    rk = wuk_t.shape[2]
    e = wr.shape[2]
    full = lambda arr: pl.BlockSpec(arr.shape, lambda i: (0,) * arr.ndim)
    return pl.pallas_call(
        _q_sample_kernel,
        grid=(1,),
        in_specs=[full(a) for a in (x, wc, wm, g, wn, wr, wrs, wuk_t, cos, sin)],
        out_specs=[pl.BlockSpec((heads, m, rk), lambda i: (0, 0, 0)),
                   pl.BlockSpec((heads, m, e), lambda i: (0, 0, 0)),
                   pl.BlockSpec((m, wm.shape[1]), lambda i: (0, 0))],
        out_shape=[jax.ShapeDtypeStruct((heads, m, rk), BF16), jax.ShapeDtypeStruct((heads, m, e), BF16),
                   jax.ShapeDtypeStruct((m, wm.shape[1]), F32)],
        compiler_params=_cparams(("arbitrary",)),
        name="q_sample",
    )(x, wc, wm, g, wn, wr, wrs, wuk_t, cos, sin)


def _mla_sample_kernel(pt_ref, ql_ref, qr_ref, *refs, pages, new_len, scale):
    del pt_ref
    ck_refs = refs[:pages]
    kr_refs = refs[pages:2 * pages]
    cn_ref, kn_ref, o_ref, m_ref, l_ref, acc_ref = refs[2 * pages:]
    c = pl.program_id(1)
    streams = m_ref.shape[0]
    per = pages // streams

    @pl.when(c == 0)
    def _():
        m_ref[...] = jnp.full(m_ref.shape, -jnp.inf, F32)
        l_ref[...] = jnp.zeros(l_ref.shape, F32)
        acc_ref[...] = jnp.zeros(acc_ref.shape, F32)

    ql = ql_ref[0]
    qr = qr_ref[0]

    def attend(st, kv, rope_scores, visible):
        s = (lax.dot_general(ql, kv, _NT, preferred_element_type=F32) + rope_scores) * scale
        if visible is not None:
            s = jnp.where(visible, s, MASK_VALUE)
        m_old = m_ref[st]
        m_new = jnp.maximum(m_old, jnp.max(s, axis=-1, keepdims=True))
        a = jnp.exp(m_old - m_new)
        p = jnp.exp(s - m_new)
        l_ref[st] = a * l_ref[st] + jnp.sum(p, axis=-1, keepdims=True)
        acc_ref[st] = a * acc_ref[st] + jnp.dot(p.astype(BF16), kv, preferred_element_type=F32)
        m_ref[st] = m_new

    for st in range(streams):
        part = slice(st * per, (st + 1) * per)
        kr_t = jnp.concatenate([r[0].astype(BF16) for r in kr_refs[part]], axis=1)
        attend(st, jnp.concatenate([r[0].astype(BF16) for r in ck_refs[part]], axis=0),
               jnp.dot(qr, kr_t, preferred_element_type=F32), None)

    @pl.when(c == pl.num_programs(1) - 1)
    def _():
        nq, nk = ql.shape[0], cn_ref.shape[1]
        s_of_row = lax.broadcasted_iota(I32, (nq, nk), 0) % new_len
        j = lax.broadcasted_iota(I32, (nq, nk), 1)
        attend(0, cn_ref[0].astype(BF16),
               lax.dot_general(qr, kn_ref[0].astype(BF16), _NT, preferred_element_type=F32), j <= s_of_row)
        m = m_ref[0]
        for st in range(1, streams):
            m = jnp.maximum(m, m_ref[st])
        l = jnp.zeros(m.shape, F32)
        acc = jnp.zeros(acc_ref.shape[1:], F32)
        for st in range(streams):
            w = jnp.exp(m_ref[st] - m)
            l = l + w * l_ref[st]
            acc = acc + w * acc_ref[st]
        o_ref[0] = acc / l


def _mla_sample(page_table, ql, qr, cache_c, cache_kr_t, c_new, kr_new, new_len, scale):
    bs, n_pages = page_table.shape
    _, nq, rk = ql.shape
    e = qr.shape[2]
    page = cache_c.shape[1]
    pages = math.gcd(n_pages, PAGES_PER_STEP)
    steps = n_pages // pages
    streams = 2 if pages % 2 == 0 else 1
    pt = page_table.reshape(-1)

    def page_spec(shape, j):
        return pl.BlockSpec((1,) + shape, lambda b, c, pt: (pt[b * n_pages + c * pages + j], 0, 0))

    per_b = lambda n, w: pl.BlockSpec((1, n, w), lambda b, c, pt: (b, 0, 0))
    s8 = c_new.shape[1]
    return pl.pallas_call(
        functools.partial(_mla_sample_kernel, pages=pages, new_len=new_len, scale=scale),
        grid_spec=pltpu.PrefetchScalarGridSpec(
            num_scalar_prefetch=1,
            grid=(bs, steps),
            in_specs=([per_b(nq, rk), per_b(nq, e)]
                      + [page_spec((page, rk), j) for j in range(pages)]
                      + [page_spec((e, page), j) for j in range(pages)]
                      + [per_b(s8, rk), per_b(s8, e)]),
            out_specs=per_b(nq, rk),
            scratch_shapes=[pltpu.VMEM((streams, nq, 1), F32), pltpu.VMEM((streams, nq, 1), F32),
                            pltpu.VMEM((streams, nq, rk), F32)]),
        out_shape=jax.ShapeDtypeStruct((bs, nq, rk), F32),
        compiler_params=_cparams(("parallel", "arbitrary")),
        name="mla_sample",
    )(pt, ql, qr, *([cache_c] * pages), *([cache_kr_t] * pages), c_new, kr_new)


def _head_up_kernel(x_ref, w_ref, o_ref):
    o_ref[0] = jnp.dot(x_ref[0].astype(BF16), w_ref[0], preferred_element_type=F32)


def _head_up(lat, wv):
    h, m, rk = lat.shape
    v = wv.shape[2]
    return pl.pallas_call(
        _head_up_kernel,
        grid=(h,),
        in_specs=[pl.BlockSpec((1, m, rk), lambda i: (i, 0, 0)), pl.BlockSpec((1, rk, v), lambda i: (i, 0, 0))],
        out_specs=pl.BlockSpec((1, m, v), lambda i: (i, 0, 0)),
        out_shape=jax.ShapeDtypeStruct((h, m, v), F32),
        compiler_params=_cparams(("parallel",)),
        name="head_up",
    )(lat, wv)


def _head_slots(w, heads, width):
    k = w.shape[0]
    w = w.reshape(k, heads, width)
    return jnp.pad(w, ((0, 0), (0, 0), (0, LANES - width))).reshape(k, heads * LANES)


def kernel(x_prompt, x_sample, cache_kv_latent, cache_k_rope, cache_mem_k, cache_mem_v, state_conv, page_table, mem_prompt, w_in_a, conv_w, w_out_a, w_in_b, g_q, w_uq, w_out_b, w_kv_down, g_kv, w_uk, w_uv, w_mem_k, w_mem_v, ln1_g, ln1_b, ln2_g, ln2_b, w_router, b_router, w_gate, b_gate, w_up, b_up, w_down, b_down):
    bp, sp, d = x_prompt.shape
    bs, ss, _ = x_sample.shape
    depth = ln1_g.shape[0]
    n_a = w_in_a.shape[0]
    d_conv = conv_w.shape[2]
    assert conv_w.shape[1] == 3
    n_mem = mem_prompt.shape[1]
    mem_heads = cache_mem_k.shape[3]
    mem_w = mem_heads * cache_mem_k.shape[4]
    kv_rank, heads, nope = w_uk.shape
    v_dim = w_uv.shape[2]
    rope_dim = cache_k_rope.shape[2]
    q_rank = g_q.shape[1]
    n_past = page_table.shape[1] * cache_kv_latent.shape[1]
    alpha = (2 * depth) ** 0.25
    scale = (nope + rope_dim) ** -0.5
    half = rope_dim // 2

    xp = x_prompt.reshape(bp * sp, d)
    xs = x_sample.reshape(bs * ss, d)

    w_mem = jnp.concatenate([w_mem_k[l] for l in range(depth)] + [w_mem_v[l] for l in range(depth)],
                            axis=1).astype(BF16)
    mem_k_flat, mem_v_flat = _mem_kv(mem_prompt.reshape(bp * n_mem, d), w_mem, depth)
    mem_k_p = mem_k_flat.reshape(depth, bp, n_mem, mem_w)
    mem_v_p = mem_v_flat.reshape(depth, bp, n_mem, mem_w)
    mem_k_s = cache_mem_k.reshape(depth, bs, n_mem, mem_w)
    mem_v_s = cache_mem_v.reshape(depth, bs, n_mem, mem_w)

    conv_p, conv_s = [], []
    c_p = kr_p = c_s = kr_s = None
    k_p = v_p = None
    tables = None

    for l in range(depth):
        g1, b1 = ln1_g[l].reshape(1, d), ln1_b[l].reshape(1, d)
        if l < n_a:
            w_in = w_in_a[l].astype(BF16)
            w_out = w_out_a[l].astype(BF16)
            splits = [d_conv, d_conv, d_conv, mem_w]
            gb, gc, hv, qm = _proj(xp, w_in, splits, "in_proj_a")
            y, st = _conv_seq(gb, gc, hv, jnp.zeros((bp, 2, d_conv), F32), conv_w[l], bp)
            conv_p.append(st)
            ym = _mem_attn(qm.reshape(bp, sp, mem_w), mem_k_p, mem_v_p, l, mem_heads).reshape(bp * sp, mem_w)
            xp = _out_ln(y, ym, xp, w_out[:d_conv], w_out[d_conv:], g1, b1, alpha)

            gb, gc, hv, qm = _proj(xs, w_in, splits, "in_proj_a")
            r3 = lambda a: a.reshape(bs, ss, d_conv)
            y, st = _conv_short(r3(gb), r3(gc), r3(hv), state_conv[l], conv_w[l])
            conv_s.append(st)
            ym = _mem_attn(qm.reshape(bs, ss, mem_w), mem_k_s, mem_v_s, l, mem_heads).reshape(bs * ss, mem_w)
            xs = _out_ln(y.reshape(bs * ss, d_conv), ym, xs, w_out[:d_conv], w_out[d_conv:], g1, b1, alpha)
        else:
            j = l - n_a
            if tables is None:
                cos_p, sin_p = _rope_table(0, sp, rope_dim)
                cos_s, sin_s = _rope_table(n_past, ss, rope_dim)
                cos_s, sin_s = jnp.tile(cos_s, (bs, 1)), jnp.tile(sin_s, (bs, 1))
                pad = LANES - nope - rope_dim
                cos_slot = scale * jnp.concatenate([jnp.ones((sp, nope), F32), cos_p, jnp.zeros((sp, pad), F32)], axis=1)
                sin_slot = scale * jnp.concatenate([jnp.zeros((sp, nope), F32), sin_p, jnp.zeros((sp, pad), F32)], axis=1)
                tables = True
                wc = w_kv_down[:, :kv_rank].astype(BF16)
                wr = w_kv_down[:, kv_rank:]
                wrs = _swap_halves_signed(wr).astype(BF16)
                wr = wr.astype(BF16)
                gk = g_kv.reshape(1, kv_rank)
                c_p, kr_p = _shared_kv(xp, wc, wr, wrs, gk, cos_p, sin_p)
                c_s, kr_s = _shared_kv(xs, wc, wr, wrs, gk, cos_s, sin_s)
                wk_pad = _head_slots(w_uk.reshape(kv_rank, heads * nope), heads, nope).astype(BF16)
                rep = jnp.pad(jnp.eye(rope_dim, dtype=F32), ((0, 0), (nope, pad)))
                rep = jnp.tile(rep, (1, heads)).astype(BF16)
                k_p, v_p = _kv_up(c_p, kr_p, wk_pad, rep, w_uv.reshape(kv_rank, heads * v_dim).astype(BF16))
                pad_rows = -ss % SUBLANES
                c_new = jnp.pad(c_s.reshape(bs, ss, kv_rank), ((0, 0), (0, pad_rows), (0, 0)))
                kr_new = jnp.pad(kr_s.reshape(bs, ss, rope_dim), ((0, 0), (0, pad_rows), (0, 0)))
                cache_kr_t = jnp.transpose(cache_k_rope, (0, 2, 1))
                wuk_t = jnp.transpose(w_uk, (1, 2, 0)).astype(BF16)
                wuv_h = jnp.transpose(w_uv, (1, 0, 2)).astype(BF16)

            w_in = w_in_b[j]
            wcq = w_in[:, :q_rank].astype(BF16)
            wqm = w_in[:, q_rank:].astype(BF16)
            gq = g_q[j].reshape(1, q_rank)
            wq = w_uq[j].reshape(q_rank, heads, nope + rope_dim)
            wq_nope = wq[:, :, :nope]
            wq_rope = wq[:, :, nope:]
            wq_rope_sw = _swap_halves_signed(wq_rope)
            w_out = w_out_b[j].astype(BF16)
            hv_w = heads * v_dim

            slot = lambda a, b: _head_slots(jnp.concatenate([a, b], axis=2).reshape(q_rank, -1), heads,
                                            nope + rope_dim).astype(BF16)
            wq_pad = slot(wq_nope, wq_rope)
            wqs_pad = slot(jnp.zeros_like(wq_nope), wq_rope_sw)
            q, qm = _q_prompt(xp, wcq, wqm, gq, wq_pad, wqs_pad, cos_slot, sin_slot)
            att = _mla_prompt(q, k_p, v_p, bp)
            ym = _mem_attn(qm.reshape(bp, sp, mem_w), mem_k_p, mem_v_p, l, mem_heads).reshape(bp * sp, mem_w)
            xp = _out_ln(att, ym, xp, w_out[:hv_w], w_out[hv_w:], g1, b1, alpha)

            th = lambda a: jnp.transpose(a, (1, 0, 2)).astype(BF16)
            ql, qr, qm = _q_sample(xs, wcq, wqm, gq, th(wq_nope), th(wq_rope), th(wq_rope_sw), wuk_t,
                                   cos_s, sin_s)
            per_req = lambda a: jnp.transpose(a.reshape(heads, bs, ss, -1), (1, 0, 2, 3)).reshape(bs, heads * ss, -1)
            lat = _mla_sample(page_table, per_req(ql), per_req(qr), cache_kv_latent, cache_kr_t,
                              c_new, kr_new, ss, scale)
            lat = jnp.transpose(lat.reshape(bs, heads, ss, kv_rank), (1, 0, 2, 3)).reshape(heads, bs * ss, kv_rank)
            att = jnp.transpose(_head_up(lat, wuv_h), (1, 0, 2)).reshape(bs * ss, hv_w)
            ym = _mem_attn(qm.reshape(bs, ss, mem_w), mem_k_s, mem_v_s, l, mem_heads).reshape(bs * ss, mem_w)
            xs = _out_ln(att, ym, xs, w_out[:hv_w], w_out[hv_w:], g1, b1, alpha)

        xp, xs = _moe_ln([xp, xs], l, w_router[l].T, b_router[l].reshape(-1, 1), w_gate, b_gate,
                         w_up, b_up, w_down, b_down,
                         ln2_g[l].reshape(1, d), ln2_b[l].reshape(1, d), alpha)

    ne_shape = (depth, bp, n_mem, mem_heads, mem_w // mem_heads)
    return (xp.reshape(bp, sp, d), xs.reshape(bs, ss, d),
            c_p.reshape(bp, sp, kv_rank), kr_p.reshape(bp, sp, rope_dim),
            c_s.reshape(bs, ss, kv_rank), kr_s.reshape(bs, ss, rope_dim),
            mem_k_p.reshape(ne_shape), mem_v_p.reshape(ne_shape),
            jnp.stack(conv_p), jnp.stack(conv_s))
```

```python
import functools
import math

import jax
import jax.numpy as jnp
from jax import lax
from jax.experimental import pallas as pl
from jax.experimental.pallas import tpu as pltpu

F32 = jnp.float32
BF16 = jnp.bfloat16
I32 = jnp.int32
U32 = jnp.uint32

TOP_K = 4
SWIGLU_LIMIT = 7.0
SWIGLU_ALPHA = 1.702
ROPE_THETA = 10000.0
LN_EPS = 1e-5
RMS_EPS = 1e-6
MASK_VALUE = -1e30

LANES = 128
SUBLANES = 8
VMEM_LIMIT = 56 * 1024 * 1024

ROW_TILE = 512
MOE_TILE = 256
EXPERT_ROWS = 512
KEY_CHUNK = 2048
NUM_EXPERTS = 32

_NT = (((1,), (1,)), ((), ()))


def _cparams(sem):
    return pltpu.CompilerParams(dimension_semantics=sem, vmem_limit_bytes=VMEM_LIMIT)


def _tile(n, pref):
    t = min(n, pref)
    assert n % t == 0, (n, pref)
    return t


def _layer_norm(z, g, b):
    mu = jnp.mean(z, axis=-1, keepdims=True)
    zc = z - mu
    var = jnp.mean(zc * zc, axis=-1, keepdims=True)
    return zc * lax.rsqrt(var + LN_EPS) * g + b


def _rms_norm(x, g):
    return x * lax.rsqrt(jnp.mean(x * x, axis=-1, keepdims=True) + RMS_EPS) * g


def _proj_kernel(x_ref, w_ref, *o_refs):
    h = jnp.dot(x_ref[...].astype(BF16), w_ref[...], preferred_element_type=F32)
    off = 0
    for o_ref in o_refs:
        n = o_ref.shape[-1]
        o_ref[...] = h[:, off:off + n].astype(o_ref.dtype)
        off += n


def _proj(x, w, splits, name):
    m, k = x.shape
    n = w.shape[1]
    assert sum(splits) == n
    tm = _tile(m, ROW_TILE)
    return pl.pallas_call(
        _proj_kernel,
        grid=(m // tm,),
        in_specs=[pl.BlockSpec((tm, k), lambda i: (i, 0)), pl.BlockSpec((k, n), lambda i: (0, 0))],
        out_specs=[pl.BlockSpec((tm, s), lambda i: (i, 0)) for s in splits],
        out_shape=[jax.ShapeDtypeStruct((m, s), F32) for s in splits],
        compiler_params=_cparams(("parallel",)),
        name=name,
    )(x, w)


def _memkv_kernel(x_ref, w_ref, k_ref, v_ref):
    depth, _, f = k_ref.shape
    h = jnp.dot(x_ref[...].astype(BF16), w_ref[...], preferred_element_type=F32)
    for l in range(depth):
        k_ref[l] = h[:, l * f:(l + 1) * f]
        v_ref[l] = h[:, (depth + l) * f:(depth + l + 1) * f]


def _mem_kv(mem, w_cat, depth):
    r, d = mem.shape
    f = w_cat.shape[1] // (2 * depth)
    tm = _tile(r, ROW_TILE)
    shp = jax.ShapeDtypeStruct((depth, r, f), F32)
    spec = pl.BlockSpec((depth, tm, f), lambda i: (0, i, 0))
    return pl.pallas_call(
        _memkv_kernel,
        grid=(r // tm,),
        in_specs=[pl.BlockSpec((tm, d), lambda i: (i, 0)), pl.BlockSpec(w_cat.shape, lambda i: (0, 0))],
        out_specs=[spec, spec],
        out_shape=[shp, shp],
        compiler_params=_cparams(("parallel",)),
        name="mem_kv",
    )(mem, w_cat)


def _conv_seq_kernel(gb_ref, gc_ref, hv_ref, pre_ref, w_ref, y_ref, st_ref, carry):
    j = pl.program_id(1)

    @pl.when(j == 0)
    def _():
        carry[0:2, :] = pre_ref[0]

    u = gc_ref[...] * hv_ref[...]
    ts = u.shape[0]
    c0 = carry[0:1, :]
    c1 = carry[1:2, :]
    row = lax.broadcasted_iota(I32, u.shape, 0)
    u1 = jnp.where(row == 0, c1, pltpu.roll(u, 1, 0))
    u2 = jnp.where(row == 0, c0, jnp.where(row == 1, c1, pltpu.roll(u, 2, 0)))
    w = w_ref[...]
    y_ref[...] = gb_ref[...] * (w[0:1] * u2 + w[1:2] * u1 + w[2:3] * u)
    last = u[ts - 2:ts, :]
    carry[0:2, :] = last
    st_ref[0] = last


def _conv_seq(gb, gc, hv, prefix, w, batch):
    m, c = gb.shape
    s = m // batch
    ts = _tile(s, ROW_TILE)
    nt = s // ts
    row = pl.BlockSpec((ts, c), lambda b, j: (b * nt + j, 0))
    return pl.pallas_call(
        _conv_seq_kernel,
        grid=(batch, nt),
        in_specs=[row, row, row,
                  pl.BlockSpec((1, 2, c), lambda b, j: (b, 0, 0)),
                  pl.BlockSpec((3, c), lambda b, j: (0, 0))],
        out_specs=[row, pl.BlockSpec((1, 2, c), lambda b, j: (b, 0, 0))],
        out_shape=[jax.ShapeDtypeStruct((m, c), F32), jax.ShapeDtypeStruct((batch, 2, c), F32)],
        scratch_shapes=[pltpu.VMEM((SUBLANES, c), F32)],
        compiler_params=_cparams(("parallel", "arbitrary")),
        name="conv_seq",
    )(gb, gc, hv, prefix, w)


def _conv_short_kernel(gb_ref, gc_ref, hv_ref, pre_ref, w_ref, y_ref, st_ref):
    s_len = gb_ref.shape[1]
    w = w_ref[...]

    def u(k):
        if k < 0:
            return pre_ref[:, 2 + k, :]
        return gc_ref[:, k, :] * hv_ref[:, k, :]

    for s in range(s_len):
        y_ref[:, s, :] = gb_ref[:, s, :] * (w[0:1] * u(s - 2) + w[1:2] * u(s - 1) + w[2:3] * u(s))
    for k in range(2):
        st_ref[:, k, :] = u(s_len - 2 + k)


def _conv_short(gb, gc, hv, prefix, w):
    batch, s, c = gb.shape
    full = lambda shp: pl.BlockSpec(shp, lambda i: (0,) * len(shp))
    return pl.pallas_call(
        _conv_short_kernel,
        grid=(1,),
        in_specs=[full(gb.shape)] * 3 + [full(prefix.shape), full(w.shape)],
        out_specs=[full(gb.shape), full(prefix.shape)],
        out_shape=[jax.ShapeDtypeStruct(gb.shape, F32), jax.ShapeDtypeStruct(prefix.shape, F32)],
        compiler_params=_cparams(("arbitrary",)),
        name="conv_short",
    )(gb, gc, hv, prefix, w)


def _mem_attn_kernel(q_ref, k_ref, v_ref, o_ref, *, heads):
    q = q_ref[0].astype(BF16)
    k = k_ref[0, 0].astype(BF16)
    v = v_ref[0, 0].astype(BF16)
    hd = q.shape[1] // heads
    outs = []
    for h in range(heads):
        sl = slice(h * hd, (h + 1) * hd)
        s = lax.dot_general(q[:, sl], k[:, sl], _NT, preferred_element_type=F32) * (hd ** -0.5)
        m = jnp.max(s, axis=-1, keepdims=True)
        p = jnp.exp(s - m)
        l = jnp.sum(p, axis=-1, keepdims=True)
        outs.append(jnp.dot(p.astype(BF16), v[:, sl], preferred_element_type=F32) / l)
    o_ref[0] = jnp.concatenate(outs, axis=-1)


def _mem_attn(q, mk, mv, layer, heads):
    b, s, w = q.shape
    nm = mk.shape[2]
    tq = _tile(s, ROW_TILE)
    return pl.pallas_call(
        functools.partial(_mem_attn_kernel, heads=heads),
        grid=(b, s // tq),
        in_specs=[pl.BlockSpec((1, tq, w), lambda i, j: (i, j, 0)),
                  pl.BlockSpec((1, 1, nm, w), lambda i, j: (layer, i, 0, 0)),
                  pl.BlockSpec((1, 1, nm, w), lambda i, j: (layer, i, 0, 0))],
        out_specs=pl.BlockSpec((1, tq, w), lambda i, j: (i, j, 0)),
        out_shape=jax.ShapeDtypeStruct((b, s, w), F32),
        compiler_params=_cparams(("parallel", "parallel")),
        name="mem_attn",
    )(q, mk, mv)


def _out_ln_kernel(a_ref, m_ref, x_ref, wa_ref, wm_ref, g_ref, b_ref, o_ref, *, alpha):
    o = jnp.dot(a_ref[...].astype(BF16), wa_ref[...], preferred_element_type=F32)
    o = o + jnp.dot(m_ref[...].astype(BF16), wm_ref[...], preferred_element_type=F32)
    o_ref[...] = _layer_norm(alpha * x_ref[...] + o, g_ref[...], b_ref[...])


def _out_ln(a, mem, x, wa, wm, g, b, alpha):
    m, d = x.shape
    tm = _tile(m, ROW_TILE)
    row = lambda n: pl.BlockSpec((tm, n), lambda i: (i, 0))
    full = lambda arr: pl.BlockSpec(arr.shape, lambda i: (0, 0))
    return pl.pallas_call(
        functools.partial(_out_ln_kernel, alpha=alpha),
        grid=(m // tm,),
        in_specs=[row(a.shape[1]), row(mem.shape[1]), row(d), full(wa), full(wm), full(g), full(b)],
        out_specs=row(d),
        out_shape=jax.ShapeDtypeStruct((m, d), F32),
        compiler_params=_cparams(("parallel",)),
        name="out_ln",
    )(a, mem, x, wa, wm, g, b)


def _router_kernel(x_ref, wt_ref, b_ref, pos_ref, gate_ref, cnt_ref):
    x = x_ref[...]
    logits = lax.dot_general(wt_ref[...], x, _NT, precision=lax.Precision.HIGHEST,
                             preferred_element_type=F32) + b_ref[...]
    ne, tt = logits.shape
    eidx = lax.broadcasted_iota(I32, (ne, tt), 0)
    sels, vals = [], []
    l = logits
    for _ in range(TOP_K):
        m = jnp.max(l, axis=0, keepdims=True)
        first = jnp.min(jnp.where(l == m, eidx, ne), axis=0, keepdims=True)
        sel = eidx == first
        sels.append(sel)
        vals.append(m)
        l = jnp.where(sel, -jnp.inf, l)
    ex = [jnp.exp(v - vals[0]) for v in vals]
    den = ex[0]
    for e in ex[1:]:
        den = den + e
    gate_ref[...] = jnp.concatenate([e / den for e in ex], axis=0)

    chosen = jnp.zeros((ne, tt), F32)
    for sel in sels:
        chosen = jnp.where(sel, 1.0, chosen)
    cnt = jnp.sum(chosen, axis=1, keepdims=True)
    cnt_ref[0] = cnt.astype(I32)
    cnt8 = jnp.floor((cnt + (SUBLANES - 1)) * (1.0 / SUBLANES)) * SUBLANES
    er = lax.broadcasted_iota(I32, (ne, ne), 0)
    ec = lax.broadcasted_iota(I32, (ne, ne), 1)
    below = jnp.where(ec < er, 1.0, 0.0)
    seg_start = jnp.dot(below, jnp.broadcast_to(cnt8, (ne, LANES)), precision=lax.Precision.HIGHEST,
                        preferred_element_type=F32)[:, 0:1]
    tr = lax.broadcasted_iota(I32, (tt, tt), 0)
    tc = lax.broadcasted_iota(I32, (tt, tt), 1)
    earlier = jnp.where(tr < tc, 1.0, 0.0).astype(BF16)
    rank = jnp.dot(chosen.astype(BF16), earlier, preferred_element_type=F32)
    slot = seg_start + rank
    pos_ref[...] = jnp.concatenate(
        [jnp.sum(jnp.where(sel, slot, 0.0), axis=0, keepdims=True) for sel in sels], axis=0).astype(I32)


def _router(x, wt, b, tt):
    m, d = x.shape
    ne = wt.shape[0]
    nt = m // tt
    return pl.pallas_call(
        _router_kernel,
        grid=(nt,),
        in_specs=[pl.BlockSpec((tt, d), lambda i: (i, 0)),
                  pl.BlockSpec((ne, d), lambda i: (0, 0)),
                  pl.BlockSpec((ne, 1), lambda i: (0, 0))],
        out_specs=[pl.BlockSpec((TOP_K, tt), lambda i: (0, i)),
                   pl.BlockSpec((TOP_K, tt), lambda i: (0, i)),
                   pl.BlockSpec((1, ne, 1), lambda i: (i, 0, 0))],
        out_shape=[jax.ShapeDtypeStruct((TOP_K, m), I32),
                   jax.ShapeDtypeStruct((TOP_K, m), F32),
                   jax.ShapeDtypeStruct((nt, ne, 1), I32)],
        compiler_params=_cparams(("parallel",)),
        name="router",
    )(x, wt, b)


def _sorted_rows(tt):
    return TOP_K * tt + NUM_EXPERTS * SUBLANES


def _for_each_segment(n_ref, local_ref, global_ref, tile, ne, fn):
    def body(e, c):
        k = tile * ne + e
        n = pl.multiple_of(n_ref[k], SUBLANES)

        @pl.when(n > 0)
        def _():
            fn(n, pl.multiple_of(local_ref[k], SUBLANES), pl.multiple_of(global_ref[k], SUBLANES))
        return c

    lax.fori_loop(0, ne, body, 0)


def _dispatch_kernel(n_ref, local_ref, global_ref, fill_n_ref, fill_at_ref, *refs, ne, group_tiles):
    xs_ref, buf, zeros, sem = refs[-4:]
    step = pl.program_id(0)
    last = pl.num_programs(0) - 1
    tile = step
    cur = step % 2

    def sort_tile(x_ref, slot_ref):
        x = x_ref[...].astype(BF16)
        slot = slot_ref[...]
        rows = _sorted_rows(x.shape[0])
        r = lax.broadcasted_iota(I32, (rows, x.shape[0]), 0)
        onehot = jnp.zeros(r.shape, F32)
        for k in range(TOP_K):
            onehot = jnp.where(slot[k:k + 1, :] == r, 1.0, onehot)
        buf[cur, :rows] = jnp.dot(onehot.astype(BF16), x, preferred_element_type=F32)

    first = 0
    for g, nt in enumerate(group_tiles):
        pl.when((step >= first) & (step < first + nt))(
            functools.partial(sort_tile, refs[2 * g], refs[2 * g + 1]))
        first += nt

    def copy(b):
        return lambda n, lo, go: pltpu.make_async_copy(buf.at[b, pl.ds(lo, n)], xs_ref.at[pl.ds(go, n)], sem.at[b])

    _for_each_segment(n_ref, local_ref, global_ref, tile, ne, lambda *a: copy(cur)(*a).start())

    def fill_copies(act):
        def body(k, c):
            n = pl.multiple_of(fill_n_ref[k], SUBLANES)
            at = pl.multiple_of(fill_at_ref[k], SUBLANES)

            @pl.when(n > 0)
            def _():
                act(pltpu.make_async_copy(zeros.at[pl.ds(0, n)], xs_ref.at[pl.ds(at, n)], sem.at[2]))
            return c

        lax.fori_loop(0, fill_n_ref.shape[0], body, 0)

    @pl.when(step == 0)
    def _():
        zeros[...] = jnp.zeros(zeros.shape, F32)
        fill_copies(lambda c: c.start())

    @pl.when(step > 0)
    def _():
        _for_each_segment(n_ref, local_ref, global_ref, tile - 1, ne, lambda *a: copy(1 - cur)(*a).wait())

    @pl.when(step == last)
    def _():
        _for_each_segment(n_ref, local_ref, global_ref, tile, ne, lambda *a: copy(cur)(*a).wait())
        fill_copies(lambda c: c.wait())


def _dispatch(seg, fills, groups, tiles, slots, rows_total):
    d = groups[0].shape[1]
    group_tiles = [x.shape[0] // tt for x, tt in zip(groups, tiles)]
    in_specs, args, first = [], [], 0
    for x, tt, slot, nt in zip(groups, tiles, slots, group_tiles):
        own = lambda i, first=first, nt=nt: jnp.clip(i - first, 0, nt - 1)
        in_specs += [pl.BlockSpec((tt, d), lambda i, *_, own=own: (own(i), 0)),
                     pl.BlockSpec((TOP_K, tt), lambda i, *_, own=own: (0, own(i)))]
        args += [x, slot]
        first += nt
    return pl.pallas_call(
        functools.partial(_dispatch_kernel, ne=NUM_EXPERTS, group_tiles=tuple(group_tiles)),
        grid_spec=pltpu.PrefetchScalarGridSpec(
            num_scalar_prefetch=5,
            grid=(sum(group_tiles),),
            in_specs=in_specs,
            out_specs=pl.BlockSpec(memory_space=pl.ANY),
            scratch_shapes=[pltpu.VMEM((2, _sorted_rows(max(tiles)), d), F32), pltpu.VMEM((EXPERT_ROWS, d), F32),
                            pltpu.SemaphoreType.DMA((3,))]),
        out_shape=jax.ShapeDtypeStruct((rows_total, d), F32),
        compiler_params=_cparams(("arbitrary",)),
        name="moe_dispatch",
    )(*seg, *fills, *args)


def _expert_kernel(be_ref, nu_ref, xs_ref, wg_ref, bg_ref, wu_ref, bu_ref, wd_ref, bd_ref, ys_ref,
                   wg_bf, wu_bf, wd_bf):
    b = pl.program_id(0)

    @pl.when(b < nu_ref[0])
    def _():
        prev = be_ref[jnp.maximum(b - 1, 0)]

        @pl.when((b == 0) | (be_ref[b] != prev))
        def _():
            wg_bf[...] = wg_ref[0].astype(BF16)
            wu_bf[...] = wu_ref[0].astype(BF16)
            wd_bf[...] = wd_ref[0].astype(BF16)

        x = xs_ref[...].astype(BF16)
        g = jnp.dot(x, wg_bf[...], preferred_element_type=F32) + bg_ref[0]
        u = jnp.dot(x, wu_bf[...], preferred_element_type=F32) + bu_ref[0]
        g = jnp.minimum(g, SWIGLU_LIMIT)
        u = jnp.clip(u, -SWIGLU_LIMIT, SWIGLU_LIMIT)
        h = (u + 1.0) * (g * jax.nn.sigmoid(SWIGLU_ALPHA * g))
        ys_ref[...] = jnp.dot(h.astype(BF16), wd_bf[...], preferred_element_type=F32) + bd_ref[0]

    @pl.when(b >= nu_ref[0])
    def _():
        ys_ref[...] = jnp.zeros(ys_ref.shape, F32)


def _experts(block_expert, n_used, xs, layer, wg, bg, wu, bu, wd, bd):
    rows = xs.shape[0]
    depth, ne, d, f = wg.shape
    nb = rows // EXPERT_ROWS
    blk = lambda b, be, nu: (jnp.maximum(jnp.minimum(b, nu[0] - 1), 0), 0)
    wsel = lambda b, be, nu: (layer * ne + be[b], 0, 0)
    stack = lambda w: w.reshape((depth * ne,) + w.shape[2:])
    bias = lambda v: v.reshape(depth * ne, 1, v.shape[2])
    return pl.pallas_call(
        _expert_kernel,
        grid_spec=pltpu.PrefetchScalarGridSpec(
            num_scalar_prefetch=2,
            grid=(nb,),
            in_specs=[pl.BlockSpec((EXPERT_ROWS, d), blk),
                      pl.BlockSpec((1, d, f), wsel), pl.BlockSpec((1, 1, f), wsel),
                      pl.BlockSpec((1, d, f), wsel), pl.BlockSpec((1, 1, f), wsel),
                      pl.BlockSpec((1, f, d), wsel), pl.BlockSpec((1, 1, d), wsel)],
            out_specs=pl.BlockSpec((EXPERT_ROWS, d), lambda b, be, nu: (b, 0)),
            scratch_shapes=[pltpu.VMEM((d, f), BF16), pltpu.VMEM((d, f), BF16), pltpu.VMEM((f, d), BF16)]),
        out_shape=jax.ShapeDtypeStruct((rows, d), F32),
        compiler_params=_cparams(("arbitrary",)),
        name="moe_experts",
    )(block_expert, n_used, xs, stack(wg), bias(bg), stack(wu), bias(bu), stack(wd), bias(bd))


def _combine_kernel(n_ref, local_ref, global_ref, ys_ref, slot_ref, gate_ref, x_ref, g_ref, b_ref, o_ref,
                    buf, sem, *, tile0, ne, alpha):
    step = pl.program_id(0)
    tile = step + tile0
    cur = step % 2

    def copy(b):
        return lambda n, lo, go: pltpu.make_async_copy(ys_ref.at[pl.ds(go, n)], buf.at[b, pl.ds(lo, n)], sem.at[b])

    def fetch(t, b):
        buf[b] = jnp.zeros(buf.shape[1:], F32)
        _for_each_segment(n_ref, local_ref, global_ref, t, ne, lambda *a: copy(b)(*a).start())

    @pl.when(step == 0)
    def _():
        fetch(tile, cur)

    @pl.when(step < pl.num_programs(0) - 1)
    def _():
        fetch(tile + 1, 1 - cur)

    slot = slot_ref[...]
    gate = gate_ref[...]
    rows = buf.shape[1]
    r = lax.broadcasted_iota(I32, (slot.shape[0], rows), 1)
    weights = jnp.zeros(r.shape, F32)
    for k in range(TOP_K):
        weights = jnp.where(slot[:, k:k + 1] == r, gate[:, k:k + 1], weights)
    _for_each_segment(n_ref, local_ref, global_ref, tile, ne, lambda *a: copy(cur)(*a).wait())
    y = jnp.dot(weights.astype(BF16), buf[cur].astype(BF16), preferred_element_type=F32)
    o_ref[...] = _layer_norm(alpha * x_ref[...] + y, g_ref[...], b_ref[...])


def _combine(seg, ys, slot_t, gate_t, x, g, b, tile0, tt, alpha):
    m, d = x.shape
    return pl.pallas_call(
        functools.partial(_combine_kernel, tile0=tile0, ne=NUM_EXPERTS, alpha=alpha),
        grid_spec=pltpu.PrefetchScalarGridSpec(
            num_scalar_prefetch=3,
            grid=(m // tt,),
            in_specs=[pl.BlockSpec(memory_space=pl.ANY),
                      pl.BlockSpec((tt, TOP_K), lambda i, *_: (i, 0)),
                      pl.BlockSpec((tt, TOP_K), lambda i, *_: (i, 0)),
                      pl.BlockSpec((tt, d), lambda i, *_: (i, 0)),
                      pl.BlockSpec((1, d), lambda i, *_: (0, 0)),
                      pl.BlockSpec((1, d), lambda i, *_: (0, 0))],
            out_specs=pl.BlockSpec((tt, d), lambda i, *_: (i, 0)),
            scratch_shapes=[pltpu.VMEM((2, _sorted_rows(tt), d), F32), pltpu.SemaphoreType.DMA((2,))]),
        out_shape=jax.ShapeDtypeStruct((m, d), F32),
        compiler_params=_cparams(("arbitrary",)),
        name="moe_combine",
    )(*seg, ys, slot_t, gate_t, x, g, b)


def _moe_ln(xs_groups, layer, wt, br, wg, bg, wu, bu, wd, bd, g, b, alpha):
    ne = wt.shape[0]
    assert ne == NUM_EXPERTS
    d = xs_groups[0].shape[1]
    tiles = [_tile(x.shape[0], MOE_TILE) for x in xs_groups]
    routed = [_router(x, wt, br, tt) for x, tt in zip(xs_groups, tiles)]
    cnt = jnp.concatenate([r[2][:, :, 0] for r in routed], axis=0)
    n_tiles = cnt.shape[0]
    n8 = (cnt + (SUBLANES - 1)) // SUBLANES * SUBLANES
    local = jnp.cumsum(n8, axis=1) - n8
    per_expert = jnp.sum(n8, axis=0)
    per_expert_pad = (per_expert + (EXPERT_ROWS - 1)) // EXPERT_ROWS * EXPERT_ROWS
    expert_base = jnp.cumsum(per_expert_pad) - per_expert_pad
    glob = expert_base[None, :] + jnp.cumsum(n8, axis=0) - n8
    seg = tuple(a.reshape(-1).astype(I32) for a in (n8, local, glob))
    max_rows = sum(_sorted_rows(tt) * (x.shape[0] // tt) for x, tt in zip(xs_groups, tiles)) + ne * (EXPERT_ROWS - 1)
    nb = -(-max_rows // EXPERT_ROWS)
    blocks_per_expert = per_expert_pad // EXPERT_ROWS
    n_used = jnp.sum(blocks_per_expert).astype(I32).reshape(1)
    ends = jnp.cumsum(blocks_per_expert)
    blocks = jnp.arange(nb, dtype=I32)
    block_expert = jnp.minimum(jnp.sum((ends[None, :] <= blocks[:, None]).astype(I32), axis=1), ne - 1)
    fills = (jnp.concatenate([per_expert_pad - per_expert, jnp.where(blocks >= n_used[0], EXPERT_ROWS, 0)]),
             jnp.concatenate([expert_base + per_expert, blocks * EXPERT_ROWS]))
    fills = tuple(a.astype(I32) for a in fills)

    xs = _dispatch(seg, fills, xs_groups, tiles, [r[0] for r in routed], nb * EXPERT_ROWS)
    ys = _experts(block_expert, n_used, xs, layer, wg, bg, wu, bu, wd, bd)
    outs = []
    tile0 = 0
    for x, tt, r in zip(xs_groups, tiles, routed):
        outs.append(_combine(seg, ys, r[0].T, r[1].T, x, g, b, tile0, tt, alpha))
        tile0 += x.shape[0] // tt
    return outs


def _rope_table_kernel(freq_ref, cos_ref, sin_ref, *, start):
    pos = (lax.broadcasted_iota(I32, cos_ref.shape, 0) + start).astype(F32)
    ang = pos * freq_ref[...]
    cos_ref[...] = jnp.cos(ang)
    sin_ref[...] = jnp.sin(ang)


def _rope_table(start, n, rope_dim):
    half = rope_dim // 2
    freqs = ROPE_THETA ** (-jnp.arange(half, dtype=F32) / half)
    freq_row = jnp.concatenate([freqs, freqs]).reshape(1, rope_dim)
    rows = -(-n // SUBLANES) * SUBLANES
    shp = jax.ShapeDtypeStruct((rows, rope_dim), F32)
    full = pl.BlockSpec((rows, rope_dim), lambda i: (0, 0))
    cos, sin = pl.pallas_call(
        functools.partial(_rope_table_kernel, start=start),
        grid=(1,),
        in_specs=[pl.BlockSpec((1, rope_dim), lambda i: (0, 0))],
        out_specs=[full, full],
        out_shape=[shp, shp],
        name="rope_table",
    )(freq_row)
    return cos[:n], sin[:n]


def _swap_halves_signed(w):
    half = w.shape[-1] // 2
    return jnp.concatenate([-w[..., half:], w[..., :half]], axis=-1)


def _kv_kernel(x_ref, wc_ref, wr_ref, wrs_ref, g_ref, cos_ref, sin_ref, c_ref, kr_ref):
    x = x_ref[...].astype(BF16)
    c = jnp.dot(x, wc_ref[...], preferred_element_type=F32)
    c_ref[...] = _rms_norm(c, g_ref[...])
    r = jnp.dot(x, wr_ref[...], preferred_element_type=F32)
    rs = jnp.dot(x, wrs_ref[...], preferred_element_type=F32)
    kr_ref[...] = r * cos_ref[...] + rs * sin_ref[...]


def _shared_kv(x, wc, wr, wrs, g, cos, sin):
    m, d = x.shape
    rk = wc.shape[1]
    e = wr.shape[1]
    tm = _tile(min(m, cos.shape[0]), ROW_TILE)
    period = cos.shape[0] // tm
    full = lambda arr: pl.BlockSpec(arr.shape, lambda i: (0, 0))
    tab = pl.BlockSpec((tm, e), lambda i: (i % period, 0))
    return pl.pallas_call(
        _kv_kernel,
        grid=(m // tm,),
        in_specs=[pl.BlockSpec((tm, d), lambda i: (i, 0)), full(wc), full(wr), full(wrs), full(g), tab, tab],
        out_specs=[pl.BlockSpec((tm, rk), lambda i: (i, 0)), pl.BlockSpec((tm, e), lambda i: (i, 0))],
        out_shape=[jax.ShapeDtypeStruct((m, rk), F32), jax.ShapeDtypeStruct((m, e), F32)],
        compiler_params=_cparams(("parallel",)),
        name="shared_kv",
    )(x, wc, wr, wrs, g, cos, sin)


def _kv_up_kernel(c_ref, kr_ref, wk_ref, rep_ref, wv_ref, k_ref, v_ref):
    c = c_ref[...].astype(BF16)
    k = jnp.dot(c, wk_ref[...], preferred_element_type=F32)
    k = k + jnp.dot(kr_ref[...].astype(BF16), rep_ref[...], preferred_element_type=F32)
    k_ref[...] = k.astype(BF16)
    v_ref[...] = jnp.dot(c, wv_ref[...], preferred_element_type=F32).astype(BF16)


def _kv_up(c, kr, wk_pad, rep, wv):
    m, rk = c.shape
    tm = _tile(m, ROW_TILE)
    full = lambda arr: pl.BlockSpec(arr.shape, lambda i: (0, 0))
    row = lambda n: pl.BlockSpec((tm, n), lambda i: (i, 0))
    return pl.pallas_call(
        _kv_up_kernel,
        grid=(m // tm,),
        in_specs=[row(rk), row(kr.shape[1]), full(wk_pad), full(rep), full(wv)],
        out_specs=[row(wk_pad.shape[1]), row(wv.shape[1])],
        out_shape=[jax.ShapeDtypeStruct((m, wk_pad.shape[1]), BF16), jax.ShapeDtypeStruct((m, wv.shape[1]), BF16)],
        compiler_params=_cparams(("parallel",)),
        name="kv_up",
    )(c, kr, wk_pad, rep, wv)


def _q_prompt_kernel(x_ref, wc_ref, wm_ref, g_ref, wq_ref, wqs_ref, cos_ref, sin_ref, q_ref, qm_ref):
    x = x_ref[...].astype(BF16)
    qm_ref[...] = jnp.dot(x, wm_ref[...], preferred_element_type=F32)
    cq = _rms_norm(jnp.dot(x, wc_ref[...], preferred_element_type=F32), g_ref[...]).astype(BF16)
    qa = jnp.dot(cq, wq_ref[...], preferred_element_type=F32)
    qb = jnp.dot(cq, wqs_ref[...], preferred_element_type=F32)
    cos = cos_ref[...]
    sin = sin_ref[...]
    for h in range(q_ref.shape[1] // LANES):
        sl = slice(h * LANES, (h + 1) * LANES)
        q_ref[:, sl] = (qa[:, sl] * cos + qb[:, sl] * sin).astype(BF16)


def _q_prompt(x, wc, wm, g, wq_pad, wqs_pad, cos_slot, sin_slot):
    m, d = x.shape
    tm = _tile(min(m, cos_slot.shape[0]), ROW_TILE)
    period = cos_slot.shape[0] // tm
    full = lambda arr: pl.BlockSpec(arr.shape, lambda i: (0, 0))
    tab = pl.BlockSpec((tm, LANES), lambda i: (i % period, 0))
    row = lambda n: pl.BlockSpec((tm, n), lambda i: (i, 0))
    return pl.pallas_call(
        _q_prompt_kernel,
        grid=(m // tm,),
        in_specs=[row(d), full(wc), full(wm), full(g), full(wq_pad), full(wqs_pad), tab, tab],
        out_specs=[row(wq_pad.shape[1]), row(wm.shape[1])],
        out_shape=[jax.ShapeDtypeStruct((m, wq_pad.shape[1]), BF16), jax.ShapeDtypeStruct((m, wm.shape[1]), F32)],
        compiler_params=_cparams(("parallel",)),
        name="q_prompt",
    )(x, wc, wm, g, wq_pad, wqs_pad, cos_slot, sin_slot)


def _mla_prompt_kernel(q_ref, k_ref, v_ref, o_ref, *, v_dim):
    qi = pl.program_id(2)
    tq = q_ref.shape[0]
    heads = [slice(hh * LANES, (hh + 1) * LANES) for hh in range(2)]
    q = [q_ref[:, sl] for sl in heads]

    def attend(carry, start, visible):
        kb = k_ref[pl.ds(start, tq), :]
        vb = v_ref[pl.ds(start, tq), :]
        out = []
        for hh in range(2):
            m_old, l_old, acc = carry[hh]
            s = lax.dot_general(q[hh], kb[:, heads[hh]], _NT, preferred_element_type=F32)
            if visible is not None:
                s = jnp.where(visible, s, MASK_VALUE)
            m_new = jnp.maximum(m_old, jnp.max(s, axis=-1, keepdims=True))
            a = jnp.exp(m_old - m_new)
            p = jnp.exp(s - m_new)
            out.append((m_new, a * l_old + jnp.sum(p, axis=-1, keepdims=True),
                        a * acc + jnp.dot(p.astype(BF16), vb, preferred_element_type=F32)))
        return tuple(out)

    init = tuple((jnp.full((tq, 1), -jnp.inf, F32), jnp.zeros((tq, 1), F32), jnp.zeros((tq, 2 * v_dim), F32))
                 for _ in range(2))
    carry = lax.fori_loop(0, qi, lambda ki, c: attend(c, pl.multiple_of(ki * tq, tq), None), init)
    row = lax.broadcasted_iota(I32, (tq, tq), 0)
    col = lax.broadcasted_iota(I32, (tq, tq), 1)
    (_, l0, acc0), (_, l1, acc1) = attend(carry, pl.multiple_of(qi * tq, tq), col <= row)
    lane = lax.broadcasted_iota(I32, (tq, 2 * v_dim), 1)
    o_ref[...] = jnp.where(lane < v_dim, acc0 / l0, acc1 / l1).astype(o_ref.dtype)


def _mla_prompt(q, k, v, batch):
    m, hw = q.shape
    heads = hw // LANES
    v_dim = v.shape[1] // heads
    assert heads % 2 == 0 and 2 * v_dim == LANES
    s = m // batch
    tq = _tile(s, ROW_TILE)
    nq = s // tq
    return pl.pallas_call(
        functools.partial(_mla_prompt_kernel, v_dim=v_dim),
        grid=(batch, heads // 2, nq),
        in_specs=[pl.BlockSpec((tq, 2 * LANES), lambda b, h, qi: (b * nq + qi, h)),
                  pl.BlockSpec((s, 2 * LANES), lambda b, h, qi: (b, h)),
                  pl.BlockSpec((s, 2 * v_dim), lambda b, h, qi: (b, h))],
        out_specs=pl.BlockSpec((tq, 2 * v_dim), lambda b, h, qi: (b * nq + qi, h)),
        out_shape=jax.ShapeDtypeStruct((m, heads * v_dim), BF16),
        compiler_params=_cparams(("parallel", "parallel", "arbitrary")),
        name="mla_prompt",
    )(q, k, v)


def _q_sample_kernel(x_ref, wc_ref, wm_ref, g_ref, wn_ref, wr_ref, wrs_ref, wuk_ref, cos_ref, sin_ref,
                     ql_ref, qr_ref, qm_ref):
    x = x_ref[...].astype(BF16)
    qm_ref[...] = jnp.dot(x, wm_ref[...], preferred_element_type=F32)
    cq = _rms_norm(jnp.dot(x, wc_ref[...], preferred_element_type=F32), g_ref[...]).astype(BF16)
    cos = cos_ref[...]
    sin = sin_ref[...]
    for h in range(wn_ref.shape[0]):
        qn = jnp.dot(cq, wn_ref[h], preferred_element_type=F32).astype(BF16)
        ql_ref[h] = jnp.dot(qn, wuk_ref[h], preferred_element_type=F32).astype(BF16)
        qr = jnp.dot(cq, wr_ref[h], preferred_element_type=F32) * cos
        qr_ref[h] = (qr + jnp.dot(cq, wrs_ref[h], preferred_element_type=F32) * sin).astype(BF16)


def _q_sample(x, wc, wm, g, wn, wr, wrs, wuk_t, cos, sin):
    m, d = x.shape
    heads = wn.shape[0]
    rk = wuk_t.shape[2]
    e = wr.shape[2]
    full = lambda arr: pl.BlockSpec(arr.shape, lambda i: (0,) * arr.ndim)
    return pl.pallas_call(
        _q_sample_kernel,
        grid=(1,),
        in_specs=[full(a) for a in (x, wc, wm, g, wn, wr, wrs, wuk_t, cos, sin)],
        out_specs=[pl.BlockSpec((heads, m, rk), lambda i: (0, 0, 0)),
                   pl.BlockSpec((heads, m, e), lambda i: (0, 0, 0)),
                   pl.BlockSpec((m, wm.shape[1]), lambda i: (0, 0))],
        out_shape=[jax.ShapeDtypeStruct((heads, m, rk), BF16), jax.ShapeDtypeStruct((heads, m, e), BF16),
                   jax.ShapeDtypeStruct((m, wm.shape[1]), F32)],
        compiler_params=_cparams(("arbitrary",)),
        name="q_sample",
    )(x, wc, wm, g, wn, wr, wrs, wuk_t, cos, sin)


def _mla_sample_kernel(pt_ref, ql_ref, qr_ref, cn_ref, kn_ref, cache_c, cache_kr, o_ref,
                       kv_buf, kr_buf, s_ref, sem, *, n_pages, new_len, scale, chunk):
    b = pl.program_id(0)
    cur = b % 2
    page = cache_c.shape[1]
    past = kv_buf.shape[1]

    def for_each_page(req, buf, act):
        def body(p, c):
            src = pt_ref[req * n_pages + p]
            at = pl.multiple_of(p * page, page)
            act(pltpu.make_async_copy(cache_c.at[src], kv_buf.at[buf, pl.ds(at, page)], sem.at[buf]))
            act(pltpu.make_async_copy(cache_kr.at[src], kr_buf.at[buf, :, pl.ds(at, page)], sem.at[buf]))
            return c

        lax.fori_loop(0, n_pages, body, 0)

    @pl.when(b == 0)
    def _():
        for_each_page(b, cur, lambda c: c.start())

    @pl.when(b + 1 < pl.num_programs(0))
    def _():
        for_each_page(b + 1, 1 - cur, lambda c: c.start())

    ql = ql_ref[0]
    qr = qr_ref[0]
    nq, nk = ql.shape[0], cn_ref.shape[1]
    cn = cn_ref[0].astype(BF16)
    s_of_row = lax.broadcasted_iota(I32, (nq, nk), 0) % new_len
    j = lax.broadcasted_iota(I32, (nq, nk), 1)
    s_new = lax.dot_general(ql, cn, _NT, preferred_element_type=F32)
    s_new = (s_new + lax.dot_general(qr, kn_ref[0].astype(BF16), _NT, preferred_element_type=F32)) * scale
    s_new = jnp.where(j <= s_of_row, s_new, MASK_VALUE)

    for_each_page(b, cur, lambda c: c.wait())
    for c in range(past // chunk):
        keys = slice(c * chunk, (c + 1) * chunk)
        s = lax.dot_general(ql, kv_buf[cur, keys, :].astype(BF16), _NT, preferred_element_type=F32)
        s = s + jnp.dot(qr, kr_buf[cur, :, keys].astype(BF16), preferred_element_type=F32)
        s_ref[:, keys] = s * scale
    s = s_ref[...]
    m = jnp.maximum(jnp.max(s, axis=-1, keepdims=True), jnp.max(s_new, axis=-1, keepdims=True))
    p = jnp.exp(s - m)
    p_new = jnp.exp(s_new - m)
    l = jnp.sum(p, axis=-1, keepdims=True) + jnp.sum(p_new, axis=-1, keepdims=True)
    acc = jnp.dot(p_new.astype(BF16), cn, preferred_element_type=F32)
    p = p.astype(BF16)
    for c in range(past // chunk):
        keys = slice(c * chunk, (c + 1) * chunk)
        acc = acc + jnp.dot(p[:, keys], kv_buf[cur, keys, :].astype(BF16), preferred_element_type=F32)
    o_ref[0] = acc / l


def _mla_sample(page_table, ql, qr, cache_c, cache_kr_t, c_new, kr_new, new_len, scale):
    bs, n_pages = page_table.shape
    _, nq, rk = ql.shape
    e = qr.shape[2]
    page = cache_c.shape[1]
    past = n_pages * page
    chunk = math.gcd(past, KEY_CHUNK)
    per_b = lambda n, w: pl.BlockSpec((1, n, w), lambda b, pt: (b, 0, 0))
    s8 = c_new.shape[1]
    return pl.pallas_call(
        functools.partial(_mla_sample_kernel, n_pages=n_pages, new_len=new_len, scale=scale, chunk=chunk),
        grid_spec=pltpu.PrefetchScalarGridSpec(
            num_scalar_prefetch=1,
            grid=(bs,),
            in_specs=[per_b(nq, rk), per_b(nq, e), per_b(s8, rk), per_b(s8, e),
                      pl.BlockSpec(memory_space=pl.ANY), pl.BlockSpec(memory_space=pl.ANY)],
            out_specs=per_b(nq, rk),
            scratch_shapes=[pltpu.VMEM((2, past, rk), F32), pltpu.VMEM((2, e, past), F32),
                            pltpu.VMEM((nq, past), F32), pltpu.SemaphoreType.DMA((2,))]),
        out_shape=jax.ShapeDtypeStruct((bs, nq, rk), F32),
        compiler_params=_cparams(("arbitrary",)),
        name="mla_sample",
    )(page_table.reshape(-1), ql, qr, c_new, kr_new, cache_c, cache_kr_t)


def _head_up_kernel(x_ref, w_ref, o_ref):
    o_ref[0] = jnp.dot(x_ref[0].astype(BF16), w_ref[0], preferred_element_type=F32)


def _head_up(lat, wv):
    h, m, rk = lat.shape
    v = wv.shape[2]
    return pl.pallas_call(
        _head_up_kernel,
        grid=(h,),
        in_specs=[pl.BlockSpec((1, m, rk), lambda i: (i, 0, 0)), pl.BlockSpec((1, rk, v), lambda i: (i, 0, 0))],
        out_specs=pl.BlockSpec((1, m, v), lambda i: (i, 0, 0)),
        out_shape=jax.ShapeDtypeStruct((h, m, v), F32),
        compiler_params=_cparams(("parallel",)),
        name="head_up",
    )(lat, wv)


def _head_slots(w, heads, width):
    k = w.shape[0]
    w = w.reshape(k, heads, width)
    return jnp.pad(w, ((0, 0), (0, 0), (0, LANES - width))).reshape(k, heads * LANES)


def kernel(x_prompt, x_sample, cache_kv_latent, cache_k_rope, cache_mem_k, cache_mem_v, state_conv, page_table, mem_prompt, w_in_a, conv_w, w_out_a, w_in_b, g_q, w_uq, w_out_b, w_kv_down, g_kv, w_uk, w_uv, w_mem_k, w_mem_v, ln1_g, ln1_b, ln2_g, ln2_b, w_router, b_router, w_gate, b_gate, w_up, b_up, w_down, b_down):
    bp, sp, d = x_prompt.shape
    bs, ss, _ = x_sample.shape
    depth = ln1_g.shape[0]
    n_a = w_in_a.shape[0]
    d_conv = conv_w.shape[2]
    assert conv_w.shape[1] == 3
    n_mem = mem_prompt.shape[1]
    mem_heads = cache_mem_k.shape[3]
    mem_w = mem_heads * cache_mem_k.shape[4]
    kv_rank, heads, nope = w_uk.shape
    v_dim = w_uv.shape[2]
    rope_dim = cache_k_rope.shape[2]
    q_rank = g_q.shape[1]
    n_past = page_table.shape[1] * cache_kv_latent.shape[1]
    alpha = (2 * depth) ** 0.25
    scale = (nope + rope_dim) ** -0.5
    half = rope_dim // 2

    xp = x_prompt.reshape(bp * sp, d)
    xs = x_sample.reshape(bs * ss, d)

    w_mem = jnp.concatenate([w_mem_k[l] for l in range(depth)] + [w_mem_v[l] for l in range(depth)],
                            axis=1).astype(BF16)
    mem_k_flat, mem_v_flat = _mem_kv(mem_prompt.reshape(bp * n_mem, d), w_mem, depth)
    mem_k_p = mem_k_flat.reshape(depth, bp, n_mem, mem_w)
    mem_v_p = mem_v_flat.reshape(depth, bp, n_mem, mem_w)
    mem_k_s = cache_mem_k.reshape(depth, bs, n_mem, mem_w)
    mem_v_s = cache_mem_v.reshape(depth, bs, n_mem, mem_w)

    conv_p, conv_s = [], []
    c_p = kr_p = c_s = kr_s = None
    k_p = v_p = None
    tables = None

    for l in range(depth):
        g1, b1 = ln1_g[l].reshape(1, d), ln1_b[l].reshape(1, d)
        if l < n_a:
            w_in = w_in_a[l].astype(BF16)
            w_out = w_out_a[l].astype(BF16)
            splits = [d_conv, d_conv, d_conv, mem_w]
            gb, gc, hv, qm = _proj(xp, w_in, splits, "in_proj_a")
            y, st = _conv_seq(gb, gc, hv, jnp.zeros((bp, 2, d_conv), F32), conv_w[l], bp)
            conv_p.append(st)
            ym = _mem_attn(qm.reshape(bp, sp, mem_w), mem_k_p, mem_v_p, l, mem_heads).reshape(bp * sp, mem_w)
            xp = _out_ln(y, ym, xp, w_out[:d_conv], w_out[d_conv:], g1, b1, alpha)

            gb, gc, hv, qm = _proj(xs, w_in, splits, "in_proj_a")
            r3 = lambda a: a.reshape(bs, ss, d_conv)
            y, st = _conv_short(r3(gb), r3(gc), r3(hv), state_conv[l], conv_w[l])
            conv_s.append(st)
            ym = _mem_attn(qm.reshape(bs, ss, mem_w), mem_k_s, mem_v_s, l, mem_heads).reshape(bs * ss, mem_w)
            xs = _out_ln(y.reshape(bs * ss, d_conv), ym, xs, w_out[:d_conv], w_out[d_conv:], g1, b1, alpha)
        else:
            j = l - n_a
            if tables is None:
                cos_p, sin_p = _rope_table(0, sp, rope_dim)
                cos_s, sin_s = _rope_table(n_past, ss, rope_dim)
                cos_s, sin_s = jnp.tile(cos_s, (bs, 1)), jnp.tile(sin_s, (bs, 1))
                pad = LANES - nope - rope_dim
                cos_slot = scale * jnp.concatenate([jnp.ones((sp, nope), F32), cos_p, jnp.zeros((sp, pad), F32)], axis=1)
                sin_slot = scale * jnp.concatenate([jnp.zeros((sp, nope), F32), sin_p, jnp.zeros((sp, pad), F32)], axis=1)
                tables = True
                wc = w_kv_down[:, :kv_rank].astype(BF16)
                wr = w_kv_down[:, kv_rank:]
                wrs = _swap_halves_signed(wr).astype(BF16)
                wr = wr.astype(BF16)
                gk = g_kv.reshape(1, kv_rank)
                c_p, kr_p = _shared_kv(xp, wc, wr, wrs, gk, cos_p, sin_p)
                c_s, kr_s = _shared_kv(xs, wc, wr, wrs, gk, cos_s, sin_s)
                wk_pad = _head_slots(w_uk.reshape(kv_rank, heads * nope), heads, nope).astype(BF16)
                rep = jnp.pad(jnp.eye(rope_dim, dtype=F32), ((0, 0), (nope, pad)))
                rep = jnp.tile(rep, (1, heads)).astype(BF16)
                k_p, v_p = _kv_up(c_p, kr_p, wk_pad, rep, w_uv.reshape(kv_rank, heads * v_dim).astype(BF16))
                pad_rows = -ss % SUBLANES
                c_new = jnp.pad(c_s.reshape(bs, ss, kv_rank), ((0, 0), (0, pad_rows), (0, 0)))
                kr_new = jnp.pad(kr_s.reshape(bs, ss, rope_dim), ((0, 0), (0, pad_rows), (0, 0)))
                cache_kr_t = jnp.transpose(cache_k_rope, (0, 2, 1))
                wuk_t = jnp.transpose(w_uk, (1, 2, 0)).astype(BF16)
                wuv_h = jnp.transpose(w_uv, (1, 0, 2)).astype(BF16)

            w_in = w_in_b[j]
            wcq = w_in[:, :q_rank].astype(BF16)
            wqm = w_in[:, q_rank:].astype(BF16)
            gq = g_q[j].reshape(1, q_rank)
            wq = w_uq[j].reshape(q_rank, heads, nope + rope_dim)
            wq_nope = wq[:, :, :nope]
            wq_rope = wq[:, :, nope:]
            wq_rope_sw = _swap_halves_signed(wq_rope)
            w_out = w_out_b[j].astype(BF16)
            hv_w = heads * v_dim

            slot = lambda a, b: _head_slots(jnp.concatenate([a, b], axis=2).reshape(q_rank, -1), heads,
                                            nope + rope_dim).astype(BF16)
            wq_pad = slot(wq_nope, wq_rope)
            wqs_pad = slot(jnp.zeros_like(wq_nope), wq_rope_sw)
            q, qm = _q_prompt(xp, wcq, wqm, gq, wq_pad, wqs_pad, cos_slot, sin_slot)
            att = _mla_prompt(q, k_p, v_p, bp)
            ym = _mem_attn(qm.reshape(bp, sp, mem_w), mem_k_p, mem_v_p, l, mem_heads).reshape(bp * sp, mem_w)
            xp = _out_ln(att, ym, xp, w_out[:hv_w], w_out[hv_w:], g1, b1, alpha)

            th = lambda a: jnp.transpose(a, (1, 0, 2)).astype(BF16)
            ql, qr, qm = _q_sample(xs, wcq, wqm, gq, th(wq_nope), th(wq_rope), th(wq_rope_sw), wuk_t,
                                   cos_s, sin_s)
            per_req = lambda a: jnp.transpose(a.reshape(heads, bs, ss, -1), (1, 0, 2, 3)).reshape(bs, heads * ss, -1)
            lat = _mla_sample(page_table, per_req(ql), per_req(qr), cache_kv_latent, cache_kr_t,
                              c_new, kr_new, ss, scale)
            lat = jnp.transpose(lat.reshape(bs, heads, ss, kv_rank), (1, 0, 2, 3)).reshape(heads, bs * ss, kv_rank)
            att = jnp.transpose(_head_up(lat, wuv_h), (1, 0, 2)).reshape(bs * ss, hv_w)
            ym = _mem_attn(qm.reshape(bs, ss, mem_w), mem_k_s, mem_v_s, l, mem_heads).reshape(bs * ss, mem_w)
            xs = _out_ln(att, ym, xs, w_out[:hv_w], w_out[hv_w:], g1, b1, alpha)

        xp, xs = _moe_ln([xp, xs], l, w_router[l].T, b_router[l].reshape(-1, 1), w_gate, b_gate,
                         w_up, b_up, w_down, b_down,
                         ln2_g[l].reshape(1, d), ln2_b[l].reshape(1, d), alpha)

    ne_shape = (depth, bp, n_mem, mem_heads, mem_w // mem_heads)
    return (xp.reshape(bp, sp, d), xs.reshape(bs, ss, d),
            c_p.reshape(bp, sp, kv_rank), kr_p.reshape(bp, sp, rope_dim),
            c_s.reshape(bs, ss, kv_rank), kr_s.reshape(bs, ss, rope_dim),
            mem_k_p.reshape(ne_shape), mem_v_p.reshape(ne_shape),
            jnp.stack(conv_p), jnp.stack(conv_s))
```

```python
import functools
import math

import jax
import jax.numpy as jnp
from jax import lax
from jax.experimental import pallas as pl
from jax.experimental.pallas import tpu as pltpu

F32 = jnp.float32
BF16 = jnp.bfloat16
I32 = jnp.int32
U32 = jnp.uint32

TOP_K = 4
SWIGLU_LIMIT = 7.0
SWIGLU_ALPHA = 1.702
ROPE_THETA = 10000.0
LN_EPS = 1e-5
RMS_EPS = 1e-6
MASK_VALUE = -1e30

LANES = 128
SUBLANES = 8
VMEM_LIMIT = 56 * 1024 * 1024

ROW_TILE = 512
MOE_TILE = 512
EXPERT_ROWS = 512
KEY_CHUNK = 2048
MEM_ATTN_ROWS = 32
ATTN_Q_TILE = 512
ATTN_K_TILE = 512
NUM_EXPERTS = 32

_NT = (((1,), (1,)), ((), ()))


def _cparams(sem):
    return pltpu.CompilerParams(dimension_semantics=sem, vmem_limit_bytes=VMEM_LIMIT)


def _tile(n, pref):
    t = min(n, pref)
    assert n % t == 0, (n, pref)
    return t


def _layer_norm(z, g, b):
    mu = jnp.mean(z, axis=-1, keepdims=True)
    zc = z - mu
    var = jnp.mean(zc * zc, axis=-1, keepdims=True)
    return zc * lax.rsqrt(var + LN_EPS) * g + b


def _rms_norm(x, g):
    return x * lax.rsqrt(jnp.mean(x * x, axis=-1, keepdims=True) + RMS_EPS) * g


def _proj_kernel(x_ref, w_ref, *o_refs):
    h = jnp.dot(x_ref[...].astype(BF16), w_ref[...], preferred_element_type=F32)
    off = 0
    for o_ref in o_refs:
        n = o_ref.shape[-1]
        o_ref[...] = h[:, off:off + n].astype(o_ref.dtype)
        off += n


def _proj(x, w, splits, name):
    m, k = x.shape
    n = w.shape[1]
    assert sum(splits) == n
    tm = _tile(m, ROW_TILE)
    return pl.pallas_call(
        _proj_kernel,
        grid=(m // tm,),
        in_specs=[pl.BlockSpec((tm, k), lambda i: (i, 0)), pl.BlockSpec((k, n), lambda i: (0, 0))],
        out_specs=[pl.BlockSpec((tm, s), lambda i: (i, 0)) for s in splits],
        out_shape=[jax.ShapeDtypeStruct((m, s), F32) for s in splits],
        compiler_params=_cparams(("parallel",)),
        name=name,
    )(x, w)


def _memkv_kernel(x_ref, w_ref, k_ref, v_ref):
    depth, _, f = k_ref.shape
    h = jnp.dot(x_ref[...].astype(BF16), w_ref[...], preferred_element_type=F32)
    for l in range(depth):
        k_ref[l] = h[:, l * f:(l + 1) * f]
        v_ref[l] = h[:, (depth + l) * f:(depth + l + 1) * f]


def _mem_kv(mem, w_cat, depth):
    r, d = mem.shape
    f = w_cat.shape[1] // (2 * depth)
    tm = _tile(r, ROW_TILE)
    shp = jax.ShapeDtypeStruct((depth, r, f), F32)
    spec = pl.BlockSpec((depth, tm, f), lambda i: (0, i, 0))
    return pl.pallas_call(
        _memkv_kernel,
        grid=(r // tm,),
        in_specs=[pl.BlockSpec((tm, d), lambda i: (i, 0)), pl.BlockSpec(w_cat.shape, lambda i: (0, 0))],
        out_specs=[spec, spec],
        out_shape=[shp, shp],
        compiler_params=_cparams(("parallel",)),
        name="mem_kv",
    )(mem, w_cat)


def _proj_conv_kernel(x_ref, w_ref, pre_ref, cw_ref, y_ref, qm_ref, st_ref, carry):
    j = pl.program_id(1)

    @pl.when(j == 0)
    def _():
        carry[0:2, :] = pre_ref[0]

    c = y_ref.shape[1]
    h = jnp.dot(x_ref[...].astype(BF16), w_ref[...], preferred_element_type=F32)
    qm_ref[...] = h[:, 3 * c:]
    u = h[:, c:2 * c] * h[:, 2 * c:3 * c]
    ts = u.shape[0]
    c0 = carry[0:1, :]
    c1 = carry[1:2, :]
    row = lax.broadcasted_iota(I32, u.shape, 0)
    u1 = jnp.where(row == 0, c1, pltpu.roll(u, 1, 0))
    u2 = jnp.where(row == 0, c0, jnp.where(row == 1, c1, pltpu.roll(u, 2, 0)))
    w = cw_ref[...]
    y_ref[...] = h[:, :c] * (w[0:1] * u2 + w[1:2] * u1 + w[2:3] * u)
    last = u[ts - 2:ts, :]
    carry[0:2, :] = last
    st_ref[0] = last


def _proj_conv(x, w_in, prefix, cw, batch):
    m, d = x.shape
    c = cw.shape[1]
    n = w_in.shape[1]
    s = m // batch
    ts = _tile(s, ROW_TILE)
    nt = s // ts
    row = lambda width: pl.BlockSpec((ts, width), lambda b, j: (b * nt + j, 0))
    state = pl.BlockSpec((1, 2, c), lambda b, j: (b, 0, 0))
    return pl.pallas_call(
        _proj_conv_kernel,
        grid=(batch, nt),
        in_specs=[row(d), pl.BlockSpec((d, n), lambda b, j: (0, 0)), state, pl.BlockSpec((3, c), lambda b, j: (0, 0))],
        out_specs=[row(c), row(n - 3 * c), state],
        out_shape=[jax.ShapeDtypeStruct((m, c), F32), jax.ShapeDtypeStruct((m, n - 3 * c), F32),
                   jax.ShapeDtypeStruct((batch, 2, c), F32)],
        scratch_shapes=[pltpu.VMEM((SUBLANES, c), F32)],
        compiler_params=_cparams(("parallel", "arbitrary")),
        name="proj_conv",
    )(x, w_in, prefix, cw)


def _conv_short_kernel(gb_ref, gc_ref, hv_ref, pre_ref, w_ref, y_ref, st_ref):
    s_len = gb_ref.shape[1]
    w = w_ref[...]

    def u(k):
        if k < 0:
            return pre_ref[:, 2 + k, :]
        return gc_ref[:, k, :] * hv_ref[:, k, :]

    for s in range(s_len):
        y_ref[:, s, :] = gb_ref[:, s, :] * (w[0:1] * u(s - 2) + w[1:2] * u(s - 1) + w[2:3] * u(s))
    for k in range(2):
        st_ref[:, k, :] = u(s_len - 2 + k)


def _conv_short(gb, gc, hv, prefix, w):
    batch, s, c = gb.shape
    full = lambda shp: pl.BlockSpec(shp, lambda i: (0,) * len(shp))
    return pl.pallas_call(
        _conv_short_kernel,
        grid=(1,),
        in_specs=[full(gb.shape)] * 3 + [full(prefix.shape), full(w.shape)],
        out_specs=[full(gb.shape), full(prefix.shape)],
        out_shape=[jax.ShapeDtypeStruct(gb.shape, F32), jax.ShapeDtypeStruct(prefix.shape, F32)],
        compiler_params=_cparams(("arbitrary",)),
        name="conv_short",
    )(gb, gc, hv, prefix, w)


def _mem_attn_kernel(q_ref, k_ref, v_ref, o_ref, *, heads):
    hd = q_ref.shape[2] // heads
    for i in range(q_ref.shape[0]):
        q = q_ref[i].astype(BF16)
        k = k_ref[0, i].astype(BF16)
        v = v_ref[0, i].astype(BF16)
        outs = []
        for h in range(heads):
            sl = slice(h * hd, (h + 1) * hd)
            s = lax.dot_general(q[:, sl], k[:, sl], _NT, preferred_element_type=F32) * (hd ** -0.5)
            m = jnp.max(s, axis=-1, keepdims=True)
            p = jnp.exp(s - m)
            l = jnp.sum(p, axis=-1, keepdims=True)
            outs.append(jnp.dot(p.astype(BF16), v[:, sl], preferred_element_type=F32) / l)
        o_ref[i] = jnp.concatenate(outs, axis=-1)


def _mem_attn(q, mk, mv, layer, heads):
    b, s, w = q.shape
    nm = mk.shape[2]
    tq = _tile(s, ROW_TILE)
    bb = math.gcd(b, max(1, MEM_ATTN_ROWS // tq))
    return pl.pallas_call(
        functools.partial(_mem_attn_kernel, heads=heads),
        grid=(b // bb, s // tq),
        in_specs=[pl.BlockSpec((bb, tq, w), lambda i, j: (i, j, 0)),
                  pl.BlockSpec((1, bb, nm, w), lambda i, j: (layer, i, 0, 0)),
                  pl.BlockSpec((1, bb, nm, w), lambda i, j: (layer, i, 0, 0))],
        out_specs=pl.BlockSpec((bb, tq, w), lambda i, j: (i, j, 0)),
        out_shape=jax.ShapeDtypeStruct((b, s, w), F32),
        compiler_params=_cparams(("parallel", "parallel")),
        name="mem_attn",
    )(q, mk, mv)


def _out_ln_kernel(a_ref, m_ref, x_ref, wa_ref, wm_ref, g_ref, b_ref, o_ref, *, alpha):
    o = jnp.dot(a_ref[...].astype(BF16), wa_ref[...], preferred_element_type=F32)
    o = o + jnp.dot(m_ref[...].astype(BF16), wm_ref[...], preferred_element_type=F32)
    o_ref[...] = _layer_norm(alpha * x_ref[...] + o, g_ref[...], b_ref[...])


def _out_ln(a, mem, x, wa, wm, g, b, alpha):
    m, d = x.shape
    tm = _tile(m, ROW_TILE)
    row = lambda n: pl.BlockSpec((tm, n), lambda i: (i, 0))
    full = lambda arr: pl.BlockSpec(arr.shape, lambda i: (0, 0))
    return pl.pallas_call(
        functools.partial(_out_ln_kernel, alpha=alpha),
        grid=(m // tm,),
        in_specs=[row(a.shape[1]), row(mem.shape[1]), row(d), full(wa), full(wm), full(g), full(b)],
        out_specs=row(d),
        out_shape=jax.ShapeDtypeStruct((m, d), F32),
        compiler_params=_cparams(("parallel",)),
        name="out_ln",
    )(a, mem, x, wa, wm, g, b)


def _router_kernel(x_ref, wt_ref, b_ref, pos_ref, gate_ref, cnt_ref):
    x = x_ref[...]
    logits = lax.dot_general(wt_ref[...], x, _NT, precision=lax.Precision.HIGHEST,
                             preferred_element_type=F32) + b_ref[...]
    ne, tt = logits.shape
    eidx = lax.broadcasted_iota(I32, (ne, tt), 0)
    sels, vals = [], []
    l = logits
    for _ in range(TOP_K):
        m = jnp.max(l, axis=0, keepdims=True)
        first = jnp.min(jnp.where(l == m, eidx, ne), axis=0, keepdims=True)
        sel = eidx == first
        sels.append(sel)
        vals.append(m)
        l = jnp.where(sel, -jnp.inf, l)
    ex = [jnp.exp(v - vals[0]) for v in vals]
    den = ex[0]
    for e in ex[1:]:
        den = den + e
    gate_ref[...] = jnp.concatenate([e / den for e in ex], axis=0)

    chosen = jnp.zeros((ne, tt), F32)
    for sel in sels:
        chosen = jnp.where(sel, 1.0, chosen)
    cnt = jnp.sum(chosen, axis=1, keepdims=True)
    cnt_ref[0] = cnt.astype(I32)
    cnt8 = jnp.floor((cnt + (SUBLANES - 1)) * (1.0 / SUBLANES)) * SUBLANES
    er = lax.broadcasted_iota(I32, (ne, ne), 0)
    ec = lax.broadcasted_iota(I32, (ne, ne), 1)
    below = jnp.where(ec < er, 1.0, 0.0)
    seg_start = jnp.dot(below, jnp.broadcast_to(cnt8, (ne, LANES)), precision=lax.Precision.HIGHEST,
                        preferred_element_type=F32)[:, 0:1]
    tr = lax.broadcasted_iota(I32, (tt, tt), 0)
    tc = lax.broadcasted_iota(I32, (tt, tt), 1)
    earlier = jnp.where(tr < tc, 1.0, 0.0).astype(BF16)
    rank = jnp.dot(chosen.astype(BF16), earlier, preferred_element_type=F32)
    slot = seg_start + rank
    pos_ref[...] = jnp.concatenate(
        [jnp.sum(jnp.where(sel, slot, 0.0), axis=0, keepdims=True) for sel in sels], axis=0).astype(I32)


def _router(x, wt, b, tt):
    m, d = x.shape
    ne = wt.shape[0]
    nt = m // tt
    return pl.pallas_call(
        _router_kernel,
        grid=(nt,),
        in_specs=[pl.BlockSpec((tt, d), lambda i: (i, 0)),
                  pl.BlockSpec((ne, d), lambda i: (0, 0)),
                  pl.BlockSpec((ne, 1), lambda i: (0, 0))],
        out_specs=[pl.BlockSpec((TOP_K, tt), lambda i: (0, i)),
                   pl.BlockSpec((TOP_K, tt), lambda i: (0, i)),
                   pl.BlockSpec((1, ne, 1), lambda i: (i, 0, 0))],
        out_shape=[jax.ShapeDtypeStruct((TOP_K, m), I32),
                   jax.ShapeDtypeStruct((TOP_K, m), F32),
                   jax.ShapeDtypeStruct((nt, ne, 1), I32)],
        compiler_params=_cparams(("parallel",)),
        name="router",
    )(x, wt, b)


def _sorted_rows(tt):
    return TOP_K * tt + NUM_EXPERTS * SUBLANES


def _for_each_segment(n_ref, local_ref, global_ref, tile, ne, fn):
    def body(e, c):
        k = tile * ne + e
        n = pl.multiple_of(n_ref[k], SUBLANES)

        @pl.when(n > 0)
        def _():
            fn(n, pl.multiple_of(local_ref[k], SUBLANES), pl.multiple_of(global_ref[k], SUBLANES))
        return c

    lax.fori_loop(0, ne, body, 0)


def _dispatch_kernel(n_ref, local_ref, global_ref, fill_n_ref, fill_at_ref, *refs, ne, group_tiles):
    xs_ref, buf, zeros, sem = refs[-4:]
    step = pl.program_id(0)
    last = pl.num_programs(0) - 1
    tile = step
    cur = step % 2

    def sort_tile(x_ref, slot_ref):
        x = x_ref[...].astype(BF16)
        slot = slot_ref[...]
        rows = _sorted_rows(x.shape[0])
        r = lax.broadcasted_iota(I32, (rows, x.shape[0]), 0)
        onehot = jnp.zeros(r.shape, F32)
        for k in range(TOP_K):
            onehot = jnp.where(slot[k:k + 1, :] == r, 1.0, onehot)
        buf[cur, :rows] = jnp.dot(onehot.astype(BF16), x, preferred_element_type=F32)

    first = 0
    for g, nt in enumerate(group_tiles):
        pl.when((step >= first) & (step < first + nt))(
            functools.partial(sort_tile, refs[2 * g], refs[2 * g + 1]))
        first += nt

    def copy(b):
        return lambda n, lo, go: pltpu.make_async_copy(buf.at[b, pl.ds(lo, n)], xs_ref.at[pl.ds(go, n)], sem.at[b])

    _for_each_segment(n_ref, local_ref, global_ref, tile, ne, lambda *a: copy(cur)(*a).start())

    def fill_copies(act):
        def body(k, c):
            n = pl.multiple_of(fill_n_ref[k], SUBLANES)
            at = pl.multiple_of(fill_at_ref[k], SUBLANES)

            @pl.when(n > 0)
            def _():
                act(pltpu.make_async_copy(zeros.at[pl.ds(0, n)], xs_ref.at[pl.ds(at, n)], sem.at[2]))
            return c

        lax.fori_loop(0, fill_n_ref.shape[0], body, 0)

    @pl.when(step == 0)
    def _():
        zeros[...] = jnp.zeros(zeros.shape, F32)
        fill_copies(lambda c: c.start())

    @pl.when(step > 0)
    def _():
        _for_each_segment(n_ref, local_ref, global_ref, tile - 1, ne, lambda *a: copy(1 - cur)(*a).wait())

    @pl.when(step == last)
    def _():
        _for_each_segment(n_ref, local_ref, global_ref, tile, ne, lambda *a: copy(cur)(*a).wait())
        fill_copies(lambda c: c.wait())


def _dispatch(seg, fills, groups, tiles, slots, rows_total):
    d = groups[0].shape[1]
    group_tiles = [x.shape[0] // tt for x, tt in zip(groups, tiles)]
    in_specs, args, first = [], [], 0
    for x, tt, slot, nt in zip(groups, tiles, slots, group_tiles):
        own = lambda i, first=first, nt=nt: jnp.clip(i - first, 0, nt - 1)
        in_specs += [pl.BlockSpec((tt, d), lambda i, *_, own=own: (own(i), 0)),
                     pl.BlockSpec((TOP_K, tt), lambda i, *_, own=own: (0, own(i)))]
        args += [x, slot]
        first += nt
    return pl.pallas_call(
        functools.partial(_dispatch_kernel, ne=NUM_EXPERTS, group_tiles=tuple(group_tiles)),
        grid_spec=pltpu.PrefetchScalarGridSpec(
            num_scalar_prefetch=5,
            grid=(sum(group_tiles),),
            in_specs=in_specs,
            out_specs=pl.BlockSpec(memory_space=pl.ANY),
            scratch_shapes=[pltpu.VMEM((2, _sorted_rows(max(tiles)), d), F32), pltpu.VMEM((EXPERT_ROWS, d), F32),
                            pltpu.SemaphoreType.DMA((3,))]),
        out_shape=jax.ShapeDtypeStruct((rows_total, d), F32),
        compiler_params=_cparams(("arbitrary",)),
        name="moe_dispatch",
    )(*seg, *fills, *args)


def _expert_kernel(be_ref, nu_ref, nxt_ref, xs_ref, bg_ref, bu_ref, bd_ref, wg_hbm, wu_hbm, wd_hbm, ys_ref,
                   wg_f32, wu_f32, wd_f32, wg_bf, wu_bf, wd_bf, sem, *, first_weight):
    b = pl.program_id(0)

    def weight_copies(e, act):
        w = first_weight + e
        act(pltpu.make_async_copy(wg_hbm.at[w], wg_f32, sem.at[0]))
        act(pltpu.make_async_copy(wu_hbm.at[w], wu_f32, sem.at[1]))
        act(pltpu.make_async_copy(wd_hbm.at[w], wd_f32, sem.at[2]))

    @pl.when(b < nu_ref[0])
    def _():
        prev = be_ref[jnp.maximum(b - 1, 0)]

        @pl.when((b == 0) | (be_ref[b] != prev))
        def _():
            @pl.when(b == 0)
            def _():
                weight_copies(be_ref[b], lambda c: c.start())

            weight_copies(be_ref[b], lambda c: c.wait())
            wg_bf[...] = wg_f32[...].astype(BF16)
            wu_bf[...] = wu_f32[...].astype(BF16)
            wd_bf[...] = wd_f32[...].astype(BF16)

            @pl.when(nxt_ref[b] >= 0)
            def _():
                weight_copies(nxt_ref[b], lambda c: c.start())

        x = xs_ref[...].astype(BF16)
        g = jnp.dot(x, wg_bf[...], preferred_element_type=F32) + bg_ref[0]
        u = jnp.dot(x, wu_bf[...], preferred_element_type=F32) + bu_ref[0]
        g = jnp.minimum(g, SWIGLU_LIMIT)
        u = jnp.clip(u, -SWIGLU_LIMIT, SWIGLU_LIMIT)
        h = (u + 1.0) * (g * jax.nn.sigmoid(SWIGLU_ALPHA * g))
        ys_ref[...] = jnp.dot(h.astype(BF16), wd_bf[...], preferred_element_type=F32) + bd_ref[0]

    @pl.when(b >= nu_ref[0])
    def _():
        ys_ref[...] = jnp.zeros(ys_ref.shape, F32)


def _experts(block_expert, n_used, next_expert, xs, layer, wg, bg, wu, bu, wd, bd):
    rows = xs.shape[0]
    depth, ne, d, f = wg.shape
    nb = rows // EXPERT_ROWS
    blk = lambda b, be, nu, nxt: (jnp.maximum(jnp.minimum(b, nu[0] - 1), 0), 0)
    wsel = lambda b, be, nu, nxt: (layer * ne + be[b], 0, 0)
    stack = lambda w: w.reshape((depth * ne,) + w.shape[2:])
    bias = lambda v: v.reshape(depth * ne, 1, v.shape[2])
    anywhere = pl.BlockSpec(memory_space=pl.ANY)
    return pl.pallas_call(
        functools.partial(_expert_kernel, first_weight=layer * ne),
        grid_spec=pltpu.PrefetchScalarGridSpec(
            num_scalar_prefetch=3,
            grid=(nb,),
            in_specs=[pl.BlockSpec((EXPERT_ROWS, d), blk),
                      pl.BlockSpec((1, 1, f), wsel), pl.BlockSpec((1, 1, f), wsel), pl.BlockSpec((1, 1, d), wsel),
                      anywhere, anywhere, anywhere],
            out_specs=pl.BlockSpec((EXPERT_ROWS, d), lambda b, be, nu, nxt: (b, 0)),
            scratch_shapes=[pltpu.VMEM((d, f), F32), pltpu.VMEM((d, f), F32), pltpu.VMEM((f, d), F32),
                            pltpu.VMEM((d, f), BF16), pltpu.VMEM((d, f), BF16), pltpu.VMEM((f, d), BF16),
                            pltpu.SemaphoreType.DMA((3,))]),
        out_shape=jax.ShapeDtypeStruct((rows, d), F32),
        compiler_params=_cparams(("arbitrary",)),
        name="moe_experts",
    )(block_expert, n_used, next_expert, xs, bias(bg), bias(bu), bias(bd), stack(wg), stack(wu), stack(wd))


def _combine_kernel(n_ref, local_ref, global_ref, ys_ref, slot_ref, gate_ref, x_ref, g_ref, b_ref, o_ref,
                    buf, sem, *, tile0, ne, alpha):
    step = pl.program_id(0)
    tile = step + tile0
    cur = step % 2

    def copy(b):
        return lambda n, lo, go: pltpu.make_async_copy(ys_ref.at[pl.ds(go, n)], buf.at[b, pl.ds(lo, n)], sem.at[b])

    def fetch(t, b):
        covered = TOP_K * slot_ref.shape[0]
        buf[b, covered:] = jnp.zeros((buf.shape[1] - covered, buf.shape[2]), F32)
        _for_each_segment(n_ref, local_ref, global_ref, t, ne, lambda *a: copy(b)(*a).start())

    @pl.when(step == 0)
    def _():
        fetch(tile, cur)

    @pl.when(step < pl.num_programs(0) - 1)
    def _():
        fetch(tile + 1, 1 - cur)

    slot = slot_ref[...]
    gate = gate_ref[...]
    rows = buf.shape[1]
    r = lax.broadcasted_iota(I32, (slot.shape[0], rows), 1)
    weights = jnp.zeros(r.shape, F32)
    for k in range(TOP_K):
        weights = jnp.where(slot[:, k:k + 1] == r, gate[:, k:k + 1], weights)
    _for_each_segment(n_ref, local_ref, global_ref, tile, ne, lambda *a: copy(cur)(*a).wait())
    y = jnp.dot(weights.astype(BF16), buf[cur].astype(BF16), preferred_element_type=F32)
    o_ref[...] = _layer_norm(alpha * x_ref[...] + y, g_ref[...], b_ref[...])


def _combine(seg, ys, slot_t, gate_t, x, g, b, tile0, tt, alpha):
    m, d = x.shape
    return pl.pallas_call(
        functools.partial(_combine_kernel, tile0=tile0, ne=NUM_EXPERTS, alpha=alpha),
        grid_spec=pltpu.PrefetchScalarGridSpec(
            num_scalar_prefetch=3,
            grid=(m // tt,),
            in_specs=[pl.BlockSpec(memory_space=pl.ANY),
                      pl.BlockSpec((tt, TOP_K), lambda i, *_: (i, 0)),
                      pl.BlockSpec((tt, TOP_K), lambda i, *_: (i, 0)),
                      pl.BlockSpec((tt, d), lambda i, *_: (i, 0)),
                      pl.BlockSpec((1, d), lambda i, *_: (0, 0)),
                      pl.BlockSpec((1, d), lambda i, *_: (0, 0))],
            out_specs=pl.BlockSpec((tt, d), lambda i, *_: (i, 0)),
            scratch_shapes=[pltpu.VMEM((2, _sorted_rows(tt), d), F32), pltpu.SemaphoreType.DMA((2,))]),
        out_shape=jax.ShapeDtypeStruct((m, d), F32),
        compiler_params=_cparams(("arbitrary",)),
        name="moe_combine",
    )(*seg, ys, slot_t, gate_t, x, g, b)


def _moe_ln(xs_groups, layer, wt, br, wg, bg, wu, bu, wd, bd, g, b, alpha):
    ne = wt.shape[0]
    assert ne == NUM_EXPERTS
    d = xs_groups[0].shape[1]
    tiles = [_tile(x.shape[0], MOE_TILE) for x in xs_groups]
    routed = [_router(x, wt, br, tt) for x, tt in zip(xs_groups, tiles)]
    cnt = jnp.concatenate([r[2][:, :, 0] for r in routed], axis=0)
    n_tiles = cnt.shape[0]
    n8 = (cnt + (SUBLANES - 1)) // SUBLANES * SUBLANES
    local = jnp.cumsum(n8, axis=1) - n8
    per_expert = jnp.sum(n8, axis=0)
    per_expert_pad = (per_expert + (EXPERT_ROWS - 1)) // EXPERT_ROWS * EXPERT_ROWS
    expert_base = jnp.cumsum(per_expert_pad) - per_expert_pad
    glob = expert_base[None, :] + jnp.cumsum(n8, axis=0) - n8
    seg = tuple(a.reshape(-1).astype(I32) for a in (n8, local, glob))
    max_rows = sum(_sorted_rows(tt) * (x.shape[0] // tt) for x, tt in zip(xs_groups, tiles)) + ne * (EXPERT_ROWS - 1)
    nb = -(-max_rows // EXPERT_ROWS)
    blocks_per_expert = per_expert_pad // EXPERT_ROWS
    n_used = jnp.sum(blocks_per_expert).astype(I32).reshape(1)
    ends = jnp.cumsum(blocks_per_expert)
    blocks = jnp.arange(nb, dtype=I32)
    block_expert = jnp.minimum(jnp.sum((ends[None, :] <= blocks[:, None]).astype(I32), axis=1), ne - 1)
    following = ends[block_expert]
    next_expert = jnp.where(following < n_used[0], block_expert[jnp.minimum(following, nb - 1)], -1).astype(I32)
    fills = (jnp.concatenate([per_expert_pad - per_expert, jnp.where(blocks >= n_used[0], EXPERT_ROWS, 0)]),
             jnp.concatenate([expert_base + per_expert, blocks * EXPERT_ROWS]))
    fills = tuple(a.astype(I32) for a in fills)

    xs = _dispatch(seg, fills, xs_groups, tiles, [r[0] for r in routed], nb * EXPERT_ROWS)
    ys = _experts(block_expert, n_used, next_expert, xs, layer, wg, bg, wu, bu, wd, bd)
    outs = []
    tile0 = 0
    for x, tt, r in zip(xs_groups, tiles, routed):
        outs.append(_combine(seg, ys, r[0].T, r[1].T, x, g, b, tile0, tt, alpha))
        tile0 += x.shape[0] // tt
    return outs


def _rope_table_kernel(freq_ref, cos_ref, sin_ref, *, start):
    pos = (lax.broadcasted_iota(I32, cos_ref.shape, 0) + start).astype(F32)
    ang = pos * freq_ref[...]
    cos_ref[...] = jnp.cos(ang)
    sin_ref[...] = jnp.sin(ang)


def _rope_table(start, n, rope_dim):
    half = rope_dim // 2
    freqs = ROPE_THETA ** (-jnp.arange(half, dtype=F32) / half)
    freq_row = jnp.concatenate([freqs, freqs]).reshape(1, rope_dim)
    rows = -(-n // SUBLANES) * SUBLANES
    shp = jax.ShapeDtypeStruct((rows, rope_dim), F32)
    full = pl.BlockSpec((rows, rope_dim), lambda i: (0, 0))
    cos, sin = pl.pallas_call(
        functools.partial(_rope_table_kernel, start=start),
        grid=(1,),
        in_specs=[pl.BlockSpec((1, rope_dim), lambda i: (0, 0))],
        out_specs=[full, full],
        out_shape=[shp, shp],
        name="rope_table",
    )(freq_row)
    return cos[:n], sin[:n]


def _swap_halves_signed(w):
    half = w.shape[-1] // 2
    return jnp.concatenate([-w[..., half:], w[..., :half]], axis=-1)


def _kv_kernel(x_ref, wc_ref, wr_ref, wrs_ref, g_ref, cos_ref, sin_ref, c_ref, kr_ref):
    x = x_ref[...].astype(BF16)
    c = jnp.dot(x, wc_ref[...], preferred_element_type=F32)
    c_ref[...] = _rms_norm(c, g_ref[...])
    r = jnp.dot(x, wr_ref[...], preferred_element_type=F32)
    rs = jnp.dot(x, wrs_ref[...], preferred_element_type=F32)
    kr_ref[...] = r * cos_ref[...] + rs * sin_ref[...]


def _shared_kv(x, wc, wr, wrs, g, cos, sin):
    m, d = x.shape
    rk = wc.shape[1]
    e = wr.shape[1]
    tm = _tile(min(m, cos.shape[0]), ROW_TILE)
    period = cos.shape[0] // tm
    full = lambda arr: pl.BlockSpec(arr.shape, lambda i: (0, 0))
    tab = pl.BlockSpec((tm, e), lambda i: (i % period, 0))
    return pl.pallas_call(
        _kv_kernel,
        grid=(m // tm,),
        in_specs=[pl.BlockSpec((tm, d), lambda i: (i, 0)), full(wc), full(wr), full(wrs), full(g), tab, tab],
        out_specs=[pl.BlockSpec((tm, rk), lambda i: (i, 0)), pl.BlockSpec((tm, e), lambda i: (i, 0))],
        out_shape=[jax.ShapeDtypeStruct((m, rk), F32), jax.ShapeDtypeStruct((m, e), F32)],
        compiler_params=_cparams(("parallel",)),
        name="shared_kv",
    )(x, wc, wr, wrs, g, cos, sin)


def _kv_up_kernel(c_ref, kr_ref, wk_ref, rep_ref, wv_ref, k_ref, v_ref):
    c = c_ref[...].astype(BF16)
    k = jnp.dot(c, wk_ref[...], preferred_element_type=F32)
    k = k + jnp.dot(kr_ref[...].astype(BF16), rep_ref[...], preferred_element_type=F32)
    k_ref[...] = k.astype(BF16)
    v_ref[...] = jnp.dot(c, wv_ref[...], preferred_element_type=F32).astype(BF16)


def _kv_up(c, kr, wk_pad, rep, wv):
    m, rk = c.shape
    tm = _tile(m, ROW_TILE)
    full = lambda arr: pl.BlockSpec(arr.shape, lambda i: (0, 0))
    row = lambda n: pl.BlockSpec((tm, n), lambda i: (i, 0))
    return pl.pallas_call(
        _kv_up_kernel,
        grid=(m // tm,),
        in_specs=[row(rk), row(kr.shape[1]), full(wk_pad), full(rep), full(wv)],
        out_specs=[row(wk_pad.shape[1]), row(wv.shape[1])],
        out_shape=[jax.ShapeDtypeStruct((m, wk_pad.shape[1]), BF16), jax.ShapeDtypeStruct((m, wv.shape[1]), BF16)],
        compiler_params=_cparams(("parallel",)),
        name="kv_up",
    )(c, kr, wk_pad, rep, wv)


def _q_prompt_kernel(x_ref, wc_ref, wm_ref, g_ref, wq_ref, wqs_ref, cos_ref, sin_ref, q_ref, qm_ref):
    x = x_ref[...].astype(BF16)
    qm_ref[...] = jnp.dot(x, wm_ref[...], preferred_element_type=F32)
    cq = _rms_norm(jnp.dot(x, wc_ref[...], preferred_element_type=F32), g_ref[...]).astype(BF16)
    qa = jnp.dot(cq, wq_ref[...], preferred_element_type=F32)
    qb = jnp.dot(cq, wqs_ref[...], preferred_element_type=F32)
    cos = cos_ref[...]
    sin = sin_ref[...]
    for h in range(q_ref.shape[1] // LANES):
        sl = slice(h * LANES, (h + 1) * LANES)
        q_ref[:, sl] = (qa[:, sl] * cos + qb[:, sl] * sin).astype(BF16)


def _q_prompt(x, wc, wm, g, wq_pad, wqs_pad, cos_slot, sin_slot):
    m, d = x.shape
    tm = _tile(min(m, cos_slot.shape[0]), ROW_TILE)
    period = cos_slot.shape[0] // tm
    full = lambda arr: pl.BlockSpec(arr.shape, lambda i: (0, 0))
    tab = pl.BlockSpec((tm, LANES), lambda i: (i % period, 0))
    row = lambda n: pl.BlockSpec((tm, n), lambda i: (i, 0))
    return pl.pallas_call(
        _q_prompt_kernel,
        grid=(m // tm,),
        in_specs=[row(d), full(wc), full(wm), full(g), full(wq_pad), full(wqs_pad), tab, tab],
        out_specs=[row(wq_pad.shape[1]), row(wm.shape[1])],
        out_shape=[jax.ShapeDtypeStruct((m, wq_pad.shape[1]), BF16), jax.ShapeDtypeStruct((m, wm.shape[1]), F32)],
        compiler_params=_cparams(("parallel",)),
        name="q_prompt",
    )(x, wc, wm, g, wq_pad, wqs_pad, cos_slot, sin_slot)


def _mla_prompt_kernel(q_ref, k_ref, v_ref, o_ref, *, v_dim, tk):
    qi = pl.program_id(2)
    tq = q_ref.shape[0]
    heads = [slice(hh * LANES, (hh + 1) * LANES) for hh in range(2)]
    q = [q_ref[:, sl] for sl in heads]

    def attend(carry, start, visible):
        kb = k_ref[pl.ds(start, tk), :]
        vb = v_ref[pl.ds(start, tk), :]
        out = []
        for hh in range(2):
            m_old, l_old, acc = carry[hh]
            s = lax.dot_general(q[hh], kb[:, heads[hh]], _NT, preferred_element_type=F32)
            if visible is not None:
                s = jnp.where(visible, s, MASK_VALUE)
            m_new = jnp.maximum(m_old, jnp.max(s, axis=-1, keepdims=True))
            a = jnp.exp(m_old - m_new)
            p = jnp.exp(s - m_new)
            out.append((m_new, a * l_old + jnp.sum(p, axis=-1, keepdims=True),
                        a * acc + jnp.dot(p.astype(BF16), vb, preferred_element_type=F32)))
        return tuple(out)

    init = tuple((jnp.full((tq, 1), -jnp.inf, F32), jnp.zeros((tq, 1), F32), jnp.zeros((tq, 2 * v_dim), F32))
                 for _ in range(2))
    per_tile = tq // tk
    carry = lax.fori_loop(0, qi * per_tile, lambda ki, c: attend(c, pl.multiple_of(ki * tk, tk), None), init)
    row = lax.broadcasted_iota(I32, (tq, tk), 0)
    col = lax.broadcasted_iota(I32, (tq, tk), 1)
    for j in range(per_tile):
        carry = attend(carry, pl.multiple_of(qi * tq + j * tk, tk), col + j * tk <= row)
    (_, l0, acc0), (_, l1, acc1) = carry
    lane = lax.broadcasted_iota(I32, (tq, 2 * v_dim), 1)
    o_ref[...] = jnp.where(lane < v_dim, acc0 / l0, acc1 / l1).astype(o_ref.dtype)


def _mla_prompt(q, k, v, batch):
    m, hw = q.shape
    heads = hw // LANES
    v_dim = v.shape[1] // heads
    assert heads % 2 == 0 and 2 * v_dim == LANES
    s = m // batch
    tq = _tile(s, ATTN_Q_TILE)
    tk = _tile(tq, ATTN_K_TILE)
    nq = s // tq
    return pl.pallas_call(
        functools.partial(_mla_prompt_kernel, v_dim=v_dim, tk=tk),
        grid=(batch, heads // 2, nq),
        in_specs=[pl.BlockSpec((tq, 2 * LANES), lambda b, h, qi: (b * nq + qi, h)),
                  pl.BlockSpec((s, 2 * LANES), lambda b, h, qi: (b, h)),
                  pl.BlockSpec((s, 2 * v_dim), lambda b, h, qi: (b, h))],
        out_specs=pl.BlockSpec((tq, 2 * v_dim), lambda b, h, qi: (b * nq + qi, h)),
        out_shape=jax.ShapeDtypeStruct((m, heads * v_dim), BF16),
        compiler_params=_cparams(("parallel", "parallel", "arbitrary")),
        name="mla_prompt",
    )(q, k, v)


def _q_sample_kernel(x_ref, wc_ref, wm_ref, g_ref, wn_ref, wr_ref, wrs_ref, wuk_ref, cos_ref, sin_ref,
                     ql_ref, qr_ref, qm_ref):
    x = x_ref[...].astype(BF16)
    qm_ref[...] = jnp.dot(x, wm_ref[...], preferred_element_type=F32)
    cq = _rms_norm(jnp.dot(x, wc_ref[...], preferred_element_type=F32), g_ref[...]).astype(BF16)
    cos = cos_ref[...]
    sin = sin_ref[...]
    for h in range(wn_ref.shape[0]):
        qn = jnp.dot(cq, wn_ref[h], preferred_element_type=F32).astype(BF16)
        ql_ref[h] = jnp.dot(qn, wuk_ref[h], preferred_element_type=F32).astype(BF16)
        qr = jnp.dot(cq, wr_ref[h], preferred_element_type=F32) * cos
        qr_ref[h] = (qr + jnp.dot(cq, wrs_ref[h], preferred_element_type=F32) * sin).astype(BF16)


def _q_sample(x, wc, wm, g, wn, wr, wrs, wuk_t, cos, sin):
    m, d = x.shape
    heads = wn.shape[0]
    rk = wuk_t.shape[2]
    e = wr.shape[2]
    full = lambda arr: pl.BlockSpec(arr.shape, lambda i: (0,) * arr.ndim)
    return pl.pallas_call(
        _q_sample_kernel,
        grid=(1,),
        in_specs=[full(a) for a in (x, wc, wm, g, wn, wr, wrs, wuk_t, cos, sin)],
        out_specs=[pl.BlockSpec((heads, m, rk), lambda i: (0, 0, 0)),
                   pl.BlockSpec((heads, m, e), lambda i: (0, 0, 0)),
                   pl.BlockSpec((m, wm.shape[1]), lambda i: (0, 0))],
        out_shape=[jax.ShapeDtypeStruct((heads, m, rk), BF16), jax.ShapeDtypeStruct((heads, m, e), BF16),
                   jax.ShapeDtypeStruct((m, wm.shape[1]), F32)],
        compiler_params=_cparams(("arbitrary",)),
        name="q_sample",
    )(x, wc, wm, g, wn, wr, wrs, wuk_t, cos, sin)


def _mla_sample_kernel(pt_ref, ql_ref, qr_ref, cn_ref, kn_ref, cache_c, cache_kr, o_ref,
                       kv_buf, kr_buf, s_ref, sem, *, n_pages, new_len, scale, chunk):
    b = pl.program_id(0)
    cur = b % 2
    page = cache_c.shape[1]
    past = kv_buf.shape[1]

    def for_each_page(req, buf, act):
        def body(p, c):
            src = pt_ref[req * n_pages + p]
            at = pl.multiple_of(p * page, page)
            act(pltpu.make_async_copy(cache_c.at[src], kv_buf.at[buf, pl.ds(at, page)], sem.at[buf]))
            act(pltpu.make_async_copy(cache_kr.at[src], kr_buf.at[buf, :, pl.ds(at, page)], sem.at[buf]))
            return c

        lax.fori_loop(0, n_pages, body, 0)

    @pl.when(b == 0)
    def _():
        for_each_page(b, cur, lambda c: c.start())

    @pl.when(b + 1 < pl.num_programs(0))
    def _():
        for_each_page(b + 1, 1 - cur, lambda c: c.start())

    ql = ql_ref[0]
    qr = qr_ref[0]
    nq, nk = ql.shape[0], cn_ref.shape[1]
    cn = cn_ref[0].astype(BF16)
    s_of_row = lax.broadcasted_iota(I32, (nq, nk), 0) % new_len
    j = lax.broadcasted_iota(I32, (nq, nk), 1)
    s_new = lax.dot_general(ql, cn, _NT, preferred_element_type=F32)
    s_new = (s_new + lax.dot_general(qr, kn_ref[0].astype(BF16), _NT, preferred_element_type=F32)) * scale
    s_new = jnp.where(j <= s_of_row, s_new, MASK_VALUE)

    for_each_page(b, cur, lambda c: c.wait())
    for c in range(past // chunk):
        keys = slice(c * chunk, (c + 1) * chunk)
        s = lax.dot_general(ql, kv_buf[cur, keys, :].astype(BF16), _NT, preferred_element_type=F32)
        s = s + jnp.dot(qr, kr_buf[cur, :, keys].astype(BF16), preferred_element_type=F32)
        s_ref[:, keys] = s * scale
    s = s_ref[...]
    m = jnp.maximum(jnp.max(s, axis=-1, keepdims=True), jnp.max(s_new, axis=-1, keepdims=True))
    p = jnp.exp(s - m)
    p_new = jnp.exp(s_new - m)
    l = jnp.sum(p, axis=-1, keepdims=True) + jnp.sum(p_new, axis=-1, keepdims=True)
    acc = jnp.dot(p_new.astype(BF16), cn, preferred_element_type=F32)
    p = p.astype(BF16)
    for c in range(past // chunk):
        keys = slice(c * chunk, (c + 1) * chunk)
        acc = acc + jnp.dot(p[:, keys], kv_buf[cur, keys, :].astype(BF16), preferred_element_type=F32)
    o_ref[0] = acc / l


def _mla_sample(page_table, ql, qr, cache_c, cache_kr_t, c_new, kr_new, new_len, scale):
    bs, n_pages = page_table.shape
    _, nq, rk = ql.shape
    e = qr.shape[2]
    page = cache_c.shape[1]
    past = n_pages * page
    chunk = math.gcd(past, KEY_CHUNK)
    per_b = lambda n, w: pl.BlockSpec((1, n, w), lambda b, pt: (b, 0, 0))
    s8 = c_new.shape[1]
    return pl.pallas_call(
        functools.partial(_mla_sample_kernel, n_pages=n_pages, new_len=new_len, scale=scale, chunk=chunk),
        grid_spec=pltpu.PrefetchScalarGridSpec(
            num_scalar_prefetch=1,
            grid=(bs,),
            in_specs=[per_b(nq, rk), per_b(nq, e), per_b(s8, rk), per_b(s8, e),
                      pl.BlockSpec(memory_space=pl.ANY), pl.BlockSpec(memory_space=pl.ANY)],
            out_specs=per_b(nq, rk),
            scratch_shapes=[pltpu.VMEM((2, past, rk), F32), pltpu.VMEM((2, e, past), F32),
                            pltpu.VMEM((nq, past), F32), pltpu.SemaphoreType.DMA((2,))]),
        out_shape=jax.ShapeDtypeStruct((bs, nq, rk), F32),
        compiler_params=_cparams(("arbitrary",)),
        name="mla_sample",
    )(page_table.reshape(-1), ql, qr, c_new, kr_new, cache_c, cache_kr_t)


def _head_up_kernel(x_ref, w_ref, o_ref):
    o_ref[0] = jnp.dot(x_ref[0].astype(BF16), w_ref[0], preferred_element_type=F32)


def _head_up(lat, wv):
    h, m, rk = lat.shape
    v = wv.shape[2]
    return pl.pallas_call(
        _head_up_kernel,
        grid=(h,),
        in_specs=[pl.BlockSpec((1, m, rk), lambda i: (i, 0, 0)), pl.BlockSpec((1, rk, v), lambda i: (i, 0, 0))],
        out_specs=pl.BlockSpec((1, m, v), lambda i: (i, 0, 0)),
        out_shape=jax.ShapeDtypeStruct((h, m, v), F32),
        compiler_params=_cparams(("parallel",)),
        name="head_up",
    )(lat, wv)


def _head_slots(w, heads, width):
    k = w.shape[0]
    w = w.reshape(k, heads, width)
    return jnp.pad(w, ((0, 0), (0, 0), (0, LANES - width))).reshape(k, heads * LANES)


def kernel(x_prompt, x_sample, cache_kv_latent, cache_k_rope, cache_mem_k, cache_mem_v, state_conv, page_table, mem_prompt, w_in_a, conv_w, w_out_a, w_in_b, g_q, w_uq, w_out_b, w_kv_down, g_kv, w_uk, w_uv, w_mem_k, w_mem_v, ln1_g, ln1_b, ln2_g, ln2_b, w_router, b_router, w_gate, b_gate, w_up, b_up, w_down, b_down):
    bp, sp, d = x_prompt.shape
    bs, ss, _ = x_sample.shape
    depth = ln1_g.shape[0]
    n_a = w_in_a.shape[0]
    d_conv = conv_w.shape[2]
    assert conv_w.shape[1] == 3
    n_mem = mem_prompt.shape[1]
    mem_heads = cache_mem_k.shape[3]
    mem_w = mem_heads * cache_mem_k.shape[4]
    kv_rank, heads, nope = w_uk.shape
    v_dim = w_uv.shape[2]
    rope_dim = cache_k_rope.shape[2]
    q_rank = g_q.shape[1]
    n_past = page_table.shape[1] * cache_kv_latent.shape[1]
    alpha = (2 * depth) ** 0.25
    scale = (nope + rope_dim) ** -0.5
    half = rope_dim // 2

    xp = x_prompt.reshape(bp * sp, d)
    xs = x_sample.reshape(bs * ss, d)

    w_mem = jnp.concatenate([w_mem_k[l] for l in range(depth)] + [w_mem_v[l] for l in range(depth)],
                            axis=1).astype(BF16)
    mem_k_flat, mem_v_flat = _mem_kv(mem_prompt.reshape(bp * n_mem, d), w_mem, depth)
    mem_k_p = mem_k_flat.reshape(depth, bp, n_mem, mem_w)
    mem_v_p = mem_v_flat.reshape(depth, bp, n_mem, mem_w)
    mem_k_s = cache_mem_k.reshape(depth, bs, n_mem, mem_w)
    mem_v_s = cache_mem_v.reshape(depth, bs, n_mem, mem_w)

    conv_p, conv_s = [], []
    c_p = kr_p = c_s = kr_s = None
    k_p = v_p = None
    tables = None

    for l in range(depth):
        g1, b1 = ln1_g[l].reshape(1, d), ln1_b[l].reshape(1, d)
        if l < n_a:
            w_in = w_in_a[l].astype(BF16)
            w_out = w_out_a[l].astype(BF16)
            splits = [d_conv, d_conv, d_conv, mem_w]
            y, qm, st = _proj_conv(xp, w_in, jnp.zeros((bp, 2, d_conv), F32), conv_w[l], bp)
            conv_p.append(st)
            ym = _mem_attn(qm.reshape(bp, sp, mem_w), mem_k_p, mem_v_p, l, mem_heads).reshape(bp * sp, mem_w)
            xp = _out_ln(y, ym, xp, w_out[:d_conv], w_out[d_conv:], g1, b1, alpha)

            gb, gc, hv, qm = _proj(xs, w_in, splits, "in_proj_a")
            r3 = lambda a: a.reshape(bs, ss, d_conv)
            y, st = _conv_short(r3(gb), r3(gc), r3(hv), state_conv[l], conv_w[l])
            conv_s.append(st)
            ym = _mem_attn(qm.reshape(bs, ss, mem_w), mem_k_s, mem_v_s, l, mem_heads).reshape(bs * ss, mem_w)
            xs = _out_ln(y.reshape(bs * ss, d_conv), ym, xs, w_out[:d_conv], w_out[d_conv:], g1, b1, alpha)
        else:
            j = l - n_a
            if tables is None:
                cos_p, sin_p = _rope_table(0, sp, rope_dim)
                cos_s, sin_s = _rope_table(n_past, ss, rope_dim)
                cos_s, sin_s = jnp.tile(cos_s, (bs, 1)), jnp.tile(sin_s, (bs, 1))
                pad = LANES - nope - rope_dim
                cos_slot = scale * jnp.concatenate([jnp.ones((sp, nope), F32), cos_p, jnp.zeros((sp, pad), F32)], axis=1)
                sin_slot = scale * jnp.concatenate([jnp.zeros((sp, nope), F32), sin_p, jnp.zeros((sp, pad), F32)], axis=1)
                tables = True
                wc = w_kv_down[:, :kv_rank].astype(BF16)
                wr = w_kv_down[:, kv_rank:]
                wrs = _swap_halves_signed(wr).astype(BF16)
                wr = wr.astype(BF16)
                gk = g_kv.reshape(1, kv_rank)
                c_p, kr_p = _shared_kv(xp, wc, wr, wrs, gk, cos_p, sin_p)
                c_s, kr_s = _shared_kv(xs, wc, wr, wrs, gk, cos_s, sin_s)
                wk_pad = _head_slots(w_uk.reshape(kv_rank, heads * nope), heads, nope).astype(BF16)
                rep = jnp.pad(jnp.eye(rope_dim, dtype=F32), ((0, 0), (nope, pad)))
                rep = jnp.tile(rep, (1, heads)).astype(BF16)
                k_p, v_p = _kv_up(c_p, kr_p, wk_pad, rep, w_uv.reshape(kv_rank, heads * v_dim).astype(BF16))
                pad_rows = -ss % SUBLANES
                c_new = jnp.pad(c_s.reshape(bs, ss, kv_rank), ((0, 0), (0, pad_rows), (0, 0)))
                kr_new = jnp.pad(kr_s.reshape(bs, ss, rope_dim), ((0, 0), (0, pad_rows), (0, 0)))
                cache_kr_t = jnp.transpose(cache_k_rope, (0, 2, 1))
                wuk_t = jnp.transpose(w_uk, (1, 2, 0)).astype(BF16)
                wuv_h = jnp.transpose(w_uv, (1, 0, 2)).astype(BF16)

            w_in = w_in_b[j]
            wcq = w_in[:, :q_rank].astype(BF16)
            wqm = w_in[:, q_rank:].astype(BF16)
            gq = g_q[j].reshape(1, q_rank)
            wq = w_uq[j].reshape(q_rank, heads, nope + rope_dim)
            wq_nope = wq[:, :, :nope]
            wq_rope = wq[:, :, nope:]
            wq_rope_sw = _swap_halves_signed(wq_rope)
            w_out = w_out_b[j].astype(BF16)
            hv_w = heads * v_dim

            slot = lambda a, b: _head_slots(jnp.concatenate([a, b], axis=2).reshape(q_rank, -1), heads,
                                            nope + rope_dim).astype(BF16)
            wq_pad = slot(wq_nope, wq_rope)
            wqs_pad = slot(jnp.zeros_like(wq_nope), wq_rope_sw)
            q, qm = _q_prompt(xp, wcq, wqm, gq, wq_pad, wqs_pad, cos_slot, sin_slot)
            att = _mla_prompt(q, k_p, v_p, bp)
            ym = _mem_attn(qm.reshape(bp, sp, mem_w), mem_k_p, mem_v_p, l, mem_heads).reshape(bp * sp, mem_w)
            xp = _out_ln(att, ym, xp, w_out[:hv_w], w_out[hv_w:], g1, b1, alpha)

            th = lambda a: jnp.transpose(a, (1, 0, 2)).astype(BF16)
            ql, qr, qm = _q_sample(xs, wcq, wqm, gq, th(wq_nope), th(wq_rope), th(wq_rope_sw), wuk_t,
                                   cos_s, sin_s)
            per_req = lambda a: jnp.transpose(a.reshape(heads, bs, ss, -1), (1, 0, 2, 3)).reshape(bs, heads * ss, -1)
            lat = _mla_sample(page_table, per_req(ql), per_req(qr), cache_kv_latent, cache_kr_t,
                              c_new, kr_new, ss, scale)
            lat = jnp.transpose(lat.reshape(bs, heads, ss, kv_rank), (1, 0, 2, 3)).reshape(heads, bs * ss, kv_rank)
            att = jnp.transpose(_head_up(lat, wuv_h), (1, 0, 2)).reshape(bs * ss, hv_w)
            ym = _mem_attn(qm.reshape(bs, ss, mem_w), mem_k_s, mem_v_s, l, mem_heads).reshape(bs * ss, mem_w)
            xs = _out_ln(att, ym, xs, w_out[:hv_w], w_out[hv_w:], g1, b1, alpha)

        xp, xs = _moe_ln([xp, xs], l, w_router[l].T, b_router[l].reshape(-1, 1), w_gate, b_gate,
                         w_up, b_up, w_down, b_down,
                         ln2_g[l].reshape(1, d), ln2_b[l].reshape(1, d), alpha)

    ne_shape = (depth, bp, n_mem, mem_heads, mem_w // mem_heads)
    return (xp.reshape(bp, sp, d), xs.reshape(bs, ss, d),
            c_p.reshape(bp, sp, kv_rank), kr_p.reshape(bp, sp, rope_dim),
            c_s.reshape(bs, ss, kv_rank), kr_s.reshape(bs, ss, rope_dim),
            mem_k_p.reshape(ne_shape), mem_v_p.reshape(ne_shape),
            jnp.stack(conv_p), jnp.stack(conv_s))
```

```python
import functools
import math

import jax
import jax.numpy as jnp
from jax import lax
from jax.experimental import pallas as pl
from jax.experimental.pallas import tpu as pltpu

F32 = jnp.float32
BF16 = jnp.bfloat16
I32 = jnp.int32
U32 = jnp.uint32

TOP_K = 4
SWIGLU_LIMIT = 7.0
SWIGLU_ALPHA = 1.702
ROPE_THETA = 10000.0
LN_EPS = 1e-5
RMS_EPS = 1e-6
MASK_VALUE = -1e30

LANES = 128
SUBLANES = 8
VMEM_LIMIT = 56 * 1024 * 1024

ROW_TILE = 512
MOE_TILE = 512
EXPERT_ROWS = 512
KEY_CHUNK = 2048
MEM_ATTN_ROWS = 32
ATTN_Q_TILE = 512
ATTN_K_TILE = 512
NUM_EXPERTS = 32

_NT = (((1,), (1,)), ((), ()))


def _cparams(sem):
    return pltpu.CompilerParams(dimension_semantics=sem, vmem_limit_bytes=VMEM_LIMIT)


def _tile(n, pref):
    t = min(n, pref)
    assert n % t == 0, (n, pref)
    return t


def _layer_norm(z, g, b):
    mu = jnp.mean(z, axis=-1, keepdims=True)
    zc = z - mu
    var = jnp.mean(zc * zc, axis=-1, keepdims=True)
    return zc * lax.rsqrt(var + LN_EPS) * g + b


def _rms_norm(x, g):
    return x * lax.rsqrt(jnp.mean(x * x, axis=-1, keepdims=True) + RMS_EPS) * g


def _proj_kernel(x_ref, w_ref, *o_refs):
    h = jnp.dot(x_ref[...].astype(BF16), w_ref[...], preferred_element_type=F32)
    off = 0
    for o_ref in o_refs:
        n = o_ref.shape[-1]
        o_ref[...] = h[:, off:off + n].astype(o_ref.dtype)
        off += n


def _proj(x, w, splits, name):
    m, k = x.shape
    n = w.shape[1]
    assert sum(splits) == n
    tm = _tile(m, ROW_TILE)
    return pl.pallas_call(
        _proj_kernel,
        grid=(m // tm,),
        in_specs=[pl.BlockSpec((tm, k), lambda i: (i, 0)), pl.BlockSpec((k, n), lambda i: (0, 0))],
        out_specs=[pl.BlockSpec((tm, s), lambda i: (i, 0)) for s in splits],
        out_shape=[jax.ShapeDtypeStruct((m, s), F32) for s in splits],
        compiler_params=_cparams(("parallel",)),
        name=name,
    )(x, w)


def _memkv_kernel(x_ref, w_ref, k_ref, v_ref):
    depth, _, f = k_ref.shape
    h = jnp.dot(x_ref[...].astype(BF16), w_ref[...], preferred_element_type=F32)
    for l in range(depth):
        k_ref[l] = h[:, l * f:(l + 1) * f]
        v_ref[l] = h[:, (depth + l) * f:(depth + l + 1) * f]


def _mem_kv(mem, w_cat, depth):
    r, d = mem.shape
    f = w_cat.shape[1] // (2 * depth)
    tm = _tile(r, ROW_TILE)
    shp = jax.ShapeDtypeStruct((depth, r, f), F32)
    spec = pl.BlockSpec((depth, tm, f), lambda i: (0, i, 0))
    return pl.pallas_call(
        _memkv_kernel,
        grid=(r // tm,),
        in_specs=[pl.BlockSpec((tm, d), lambda i: (i, 0)), pl.BlockSpec(w_cat.shape, lambda i: (0, 0))],
        out_specs=[spec, spec],
        out_shape=[shp, shp],
        compiler_params=_cparams(("parallel",)),
        name="mem_kv",
    )(mem, w_cat)


def _proj_conv_kernel(x_ref, w_ref, pre_ref, cw_ref, y_ref, qm_ref, st_ref, carry):
    j = pl.program_id(1)

    @pl.when(j == 0)
    def _():
        carry[0:2, :] = pre_ref[0]

    c = y_ref.shape[1]
    h = jnp.dot(x_ref[...].astype(BF16), w_ref[...], preferred_element_type=F32)
    qm_ref[...] = h[:, 3 * c:].astype(qm_ref.dtype)
    u = h[:, c:2 * c] * h[:, 2 * c:3 * c]
    ts = u.shape[0]
    c0 = carry[0:1, :]
    c1 = carry[1:2, :]
    row = lax.broadcasted_iota(I32, u.shape, 0)
    u1 = jnp.where(row == 0, c1, pltpu.roll(u, 1, 0))
    u2 = jnp.where(row == 0, c0, jnp.where(row == 1, c1, pltpu.roll(u, 2, 0)))
    w = cw_ref[...]
    y_ref[...] = (h[:, :c] * (w[0:1] * u2 + w[1:2] * u1 + w[2:3] * u)).astype(y_ref.dtype)
    last = u[ts - 2:ts, :]
    carry[0:2, :] = last
    st_ref[0] = last


def _proj_conv(x, w_in, prefix, cw, batch):
    m, d = x.shape
    c = cw.shape[1]
    n = w_in.shape[1]
    s = m // batch
    ts = _tile(s, ROW_TILE)
    nt = s // ts
    row = lambda width: pl.BlockSpec((ts, width), lambda b, j: (b * nt + j, 0))
    state = pl.BlockSpec((1, 2, c), lambda b, j: (b, 0, 0))
    return pl.pallas_call(
        _proj_conv_kernel,
        grid=(batch, nt),
        in_specs=[row(d), pl.BlockSpec((d, n), lambda b, j: (0, 0)), state, pl.BlockSpec((3, c), lambda b, j: (0, 0))],
        out_specs=[row(c), row(n - 3 * c), state],
        out_shape=[jax.ShapeDtypeStruct((m, c), BF16), jax.ShapeDtypeStruct((m, n - 3 * c), BF16),
                   jax.ShapeDtypeStruct((batch, 2, c), F32)],
        scratch_shapes=[pltpu.VMEM((SUBLANES, c), F32)],
        compiler_params=_cparams(("parallel", "arbitrary")),
        name="proj_conv",
    )(x, w_in, prefix, cw)


def _conv_short_kernel(gb_ref, gc_ref, hv_ref, pre_ref, w_ref, y_ref, st_ref):
    s_len = gb_ref.shape[1]
    w = w_ref[...]

    def u(k):
        if k < 0:
            return pre_ref[:, 2 + k, :]
        return gc_ref[:, k, :] * hv_ref[:, k, :]

    for s in range(s_len):
        y_ref[:, s, :] = gb_ref[:, s, :] * (w[0:1] * u(s - 2) + w[1:2] * u(s - 1) + w[2:3] * u(s))
    for k in range(2):
        st_ref[:, k, :] = u(s_len - 2 + k)


def _conv_short(gb, gc, hv, prefix, w):
    batch, s, c = gb.shape
    full = lambda shp: pl.BlockSpec(shp, lambda i: (0,) * len(shp))
    return pl.pallas_call(
        _conv_short_kernel,
        grid=(1,),
        in_specs=[full(gb.shape)] * 3 + [full(prefix.shape), full(w.shape)],
        out_specs=[full(gb.shape), full(prefix.shape)],
        out_shape=[jax.ShapeDtypeStruct(gb.shape, F32), jax.ShapeDtypeStruct(prefix.shape, F32)],
        compiler_params=_cparams(("arbitrary",)),
        name="conv_short",
    )(gb, gc, hv, prefix, w)


def _mem_attn_kernel(q_ref, k_ref, v_ref, o_ref, *, heads):
    hd = q_ref.shape[2] // heads
    for i in range(q_ref.shape[0]):
        q = q_ref[i].astype(BF16)
        k = k_ref[0, i].astype(BF16)
        v = v_ref[0, i].astype(BF16)
        outs = []
        for h in range(heads):
            sl = slice(h * hd, (h + 1) * hd)
            s = lax.dot_general(q[:, sl], k[:, sl], _NT, preferred_element_type=F32) * (hd ** -0.5)
            m = jnp.max(s, axis=-1, keepdims=True)
            p = jnp.exp(s - m)
            l = jnp.sum(p, axis=-1, keepdims=True)
            outs.append(jnp.dot(p.astype(BF16), v[:, sl], preferred_element_type=F32) / l)
        o_ref[i] = jnp.concatenate(outs, axis=-1).astype(o_ref.dtype)


def _mem_attn(q, mk, mv, layer, heads):
    b, s, w = q.shape
    nm = mk.shape[2]
    tq = _tile(s, ROW_TILE)
    bb = math.gcd(b, max(1, MEM_ATTN_ROWS // tq))
    return pl.pallas_call(
        functools.partial(_mem_attn_kernel, heads=heads),
        grid=(b // bb, s // tq),
        in_specs=[pl.BlockSpec((bb, tq, w), lambda i, j: (i, j, 0)),
                  pl.BlockSpec((1, bb, nm, w), lambda i, j: (layer, i, 0, 0)),
                  pl.BlockSpec((1, bb, nm, w), lambda i, j: (layer, i, 0, 0))],
        out_specs=pl.BlockSpec((bb, tq, w), lambda i, j: (i, j, 0)),
        out_shape=jax.ShapeDtypeStruct((b, s, w), BF16),
        compiler_params=_cparams(("parallel", "parallel")),
        name="mem_attn",
    )(q, mk, mv)


def _out_ln_route_kernel(a_ref, m_ref, x_ref, wa_ref, wm_ref, g_ref, b_ref, wt_ref, br_ref,
                         o_ref, pos_ref, gate_ref, cnt_ref, *, alpha):
    o = jnp.dot(a_ref[...].astype(BF16), wa_ref[...], preferred_element_type=F32)
    o = o + jnp.dot(m_ref[...].astype(BF16), wm_ref[...], preferred_element_type=F32)
    x = _layer_norm(alpha * x_ref[...] + o, g_ref[...], b_ref[...])
    o_ref[...] = x
    _route(x, wt_ref[...], br_ref[...], pos_ref, gate_ref, cnt_ref)


def _out_ln_route(a, mem, x, wa, wm, g, b, wt, br, alpha):
    m, d = x.shape
    ne = wt.shape[0]
    tt = _tile(m, MOE_TILE)
    nt = m // tt
    row = lambda n: pl.BlockSpec((tt, n), lambda i: (i, 0))
    full = lambda arr: pl.BlockSpec(arr.shape, lambda i: (0, 0))
    return pl.pallas_call(
        functools.partial(_out_ln_route_kernel, alpha=alpha),
        grid=(nt,),
        in_specs=[row(a.shape[1]), row(mem.shape[1]), row(d), full(wa), full(wm), full(g), full(b), full(wt), full(br)],
        out_specs=[row(d),
                   pl.BlockSpec((TOP_K, tt), lambda i: (0, i)),
                   pl.BlockSpec((TOP_K, tt), lambda i: (0, i)),
                   pl.BlockSpec((1, ne, 1), lambda i: (i, 0, 0))],
        out_shape=[jax.ShapeDtypeStruct((m, d), F32),
                   jax.ShapeDtypeStruct((TOP_K, m), I32),
                   jax.ShapeDtypeStruct((TOP_K, m), F32),
                   jax.ShapeDtypeStruct((nt, ne, 1), I32)],
        compiler_params=_cparams(("parallel",)),
        name="out_ln_route",
    )(a, mem, x, wa, wm, g, b, wt, br)


def _route(x, wt, bias, pos_ref, gate_ref, cnt_ref):
    logits = lax.dot_general(wt, x, _NT, precision=lax.Precision.HIGHEST, preferred_element_type=F32) + bias
    ne, tt = logits.shape
    eidx = lax.broadcasted_iota(I32, (ne, tt), 0)
    sels, vals = [], []
    l = logits
    for _ in range(TOP_K):
        m = jnp.max(l, axis=0, keepdims=True)
        first = jnp.min(jnp.where(l == m, eidx, ne), axis=0, keepdims=True)
        sel = eidx == first
        sels.append(sel)
        vals.append(m)
        l = jnp.where(sel, -jnp.inf, l)
    ex = [jnp.exp(v - vals[0]) for v in vals]
    den = ex[0]
    for e in ex[1:]:
        den = den + e
    gate_ref[...] = jnp.concatenate([e / den for e in ex], axis=0)

    chosen = jnp.zeros((ne, tt), F32)
    for sel in sels:
        chosen = jnp.where(sel, 1.0, chosen)
    cnt = jnp.sum(chosen, axis=1, keepdims=True)
    cnt_ref[0] = cnt.astype(I32)
    cnt8 = jnp.floor((cnt + (SUBLANES - 1)) * (1.0 / SUBLANES)) * SUBLANES
    er = lax.broadcasted_iota(I32, (ne, ne), 0)
    ec = lax.broadcasted_iota(I32, (ne, ne), 1)
    below = jnp.where(ec < er, 1.0, 0.0)
    seg_start = jnp.dot(below, jnp.broadcast_to(cnt8, (ne, LANES)), precision=lax.Precision.HIGHEST,
                        preferred_element_type=F32)[:, 0:1]
    tr = lax.broadcasted_iota(I32, (tt, tt), 0)
    tc = lax.broadcasted_iota(I32, (tt, tt), 1)
    earlier = jnp.where(tr < tc, 1.0, 0.0).astype(BF16)
    rank = jnp.dot(chosen.astype(BF16), earlier, preferred_element_type=F32)
    slot = seg_start + rank
    pos_ref[...] = jnp.concatenate(
        [jnp.sum(jnp.where(sel, slot, 0.0), axis=0, keepdims=True) for sel in sels], axis=0).astype(I32)


def _sorted_rows(tt):
    return TOP_K * tt + NUM_EXPERTS * SUBLANES


def _for_each_segment(n_ref, local_ref, global_ref, tile, ne, fn):
    def body(e, c):
        k = tile * ne + e
        n = pl.multiple_of(n_ref[k], SUBLANES)

        @pl.when(n > 0)
        def _():
            fn(n, pl.multiple_of(local_ref[k], SUBLANES), pl.multiple_of(global_ref[k], SUBLANES))
        return c

    lax.fori_loop(0, ne, body, 0)


def _dispatch_kernel(n_ref, local_ref, global_ref, fill_n_ref, fill_at_ref, *refs, ne, group_tiles):
    xs_ref, buf, zeros, sem = refs[-4:]
    step = pl.program_id(0)
    last = pl.num_programs(0) - 1
    tile = step
    cur = step % 2

    def sort_tile(x_ref, slot_ref):
        x = x_ref[...].astype(BF16)
        slot = slot_ref[...]
        rows = _sorted_rows(x.shape[0])
        r = lax.broadcasted_iota(I32, (rows, x.shape[0]), 0)
        onehot = jnp.zeros(r.shape, F32)
        for k in range(TOP_K):
            onehot = jnp.where(slot[k:k + 1, :] == r, 1.0, onehot)
        buf[cur, :rows] = jnp.dot(onehot.astype(BF16), x, preferred_element_type=F32)

    first = 0
    for g, nt in enumerate(group_tiles):
        pl.when((step >= first) & (step < first + nt))(
            functools.partial(sort_tile, refs[2 * g], refs[2 * g + 1]))
        first += nt

    def copy(b):
        return lambda n, lo, go: pltpu.make_async_copy(buf.at[b, pl.ds(lo, n)], xs_ref.at[pl.ds(go, n)], sem.at[b])

    _for_each_segment(n_ref, local_ref, global_ref, tile, ne, lambda *a: copy(cur)(*a).start())

    def fill_copies(act):
        def body(k, c):
            n = pl.multiple_of(fill_n_ref[k], SUBLANES)
            at = pl.multiple_of(fill_at_ref[k], SUBLANES)

            @pl.when(n > 0)
            def _():
                act(pltpu.make_async_copy(zeros.at[pl.ds(0, n)], xs_ref.at[pl.ds(at, n)], sem.at[2]))
            return c

        lax.fori_loop(0, fill_n_ref.shape[0], body, 0)

    @pl.when(step == 0)
    def _():
        zeros[...] = jnp.zeros(zeros.shape, F32)
        fill_copies(lambda c: c.start())

    @pl.when(step > 0)
    def _():
        _for_each_segment(n_ref, local_ref, global_ref, tile - 1, ne, lambda *a: copy(1 - cur)(*a).wait())

    @pl.when(step == last)
    def _():
        _for_each_segment(n_ref, local_ref, global_ref, tile, ne, lambda *a: copy(cur)(*a).wait())
        fill_copies(lambda c: c.wait())


def _dispatch(seg, fills, groups, tiles, slots, rows_total):
    d = groups[0].shape[1]
    group_tiles = [x.shape[0] // tt for x, tt in zip(groups, tiles)]
    in_specs, args, first = [], [], 0
    for x, tt, slot, nt in zip(groups, tiles, slots, group_tiles):
        own = lambda i, first=first, nt=nt: jnp.clip(i - first, 0, nt - 1)
        in_specs += [pl.BlockSpec((tt, d), lambda i, *_, own=own: (own(i), 0)),
                     pl.BlockSpec((TOP_K, tt), lambda i, *_, own=own: (0, own(i)))]
        args += [x, slot]
        first += nt
    return pl.pallas_call(
        functools.partial(_dispatch_kernel, ne=NUM_EXPERTS, group_tiles=tuple(group_tiles)),
        grid_spec=pltpu.PrefetchScalarGridSpec(
            num_scalar_prefetch=5,
            grid=(sum(group_tiles),),
            in_specs=in_specs,
            out_specs=pl.BlockSpec(memory_space=pl.ANY),
            scratch_shapes=[pltpu.VMEM((2, _sorted_rows(max(tiles)), d), F32), pltpu.VMEM((EXPERT_ROWS, d), F32),
                            pltpu.SemaphoreType.DMA((3,))]),
        out_shape=jax.ShapeDtypeStruct((rows_total, d), F32),
        compiler_params=_cparams(("arbitrary",)),
        name="moe_dispatch",
    )(*seg, *fills, *args)


def _expert_kernel(be_ref, nu_ref, nxt_ref, xs_ref, bg_ref, bu_ref, bd_ref, wg_hbm, wu_hbm, wd_hbm, ys_ref,
                   wg_f32, wu_f32, wd_f32, wg_bf, wu_bf, wd_bf, sem, *, first_weight):
    b = pl.program_id(0)

    def weight_copies(e, act):
        w = first_weight + e
        act(pltpu.make_async_copy(wg_hbm.at[w], wg_f32, sem.at[0]))
        act(pltpu.make_async_copy(wu_hbm.at[w], wu_f32, sem.at[1]))
        act(pltpu.make_async_copy(wd_hbm.at[w], wd_f32, sem.at[2]))

    @pl.when(b < nu_ref[0])
    def _():
        prev = be_ref[jnp.maximum(b - 1, 0)]

        @pl.when((b == 0) | (be_ref[b] != prev))
        def _():
            @pl.when(b == 0)
            def _():
                weight_copies(be_ref[b], lambda c: c.start())

            weight_copies(be_ref[b], lambda c: c.wait())
            wg_bf[...] = wg_f32[...].astype(BF16)
            wu_bf[...] = wu_f32[...].astype(BF16)
            wd_bf[...] = wd_f32[...].astype(BF16)

            @pl.when(nxt_ref[b] >= 0)
            def _():
                weight_copies(nxt_ref[b], lambda c: c.start())

        x = xs_ref[...].astype(BF16)
        g = jnp.dot(x, wg_bf[...], preferred_element_type=F32) + bg_ref[0]
        u = jnp.dot(x, wu_bf[...], preferred_element_type=F32) + bu_ref[0]
        g = jnp.minimum(g, SWIGLU_LIMIT)
        u = jnp.clip(u, -SWIGLU_LIMIT, SWIGLU_LIMIT)
        h = (u + 1.0) * (g * jax.nn.sigmoid(SWIGLU_ALPHA * g))
        ys_ref[...] = jnp.dot(h.astype(BF16), wd_bf[...], preferred_element_type=F32) + bd_ref[0]

    @pl.when(b >= nu_ref[0])
    def _():
        ys_ref[...] = jnp.zeros(ys_ref.shape, F32)


def _experts(block_expert, n_used, next_expert, xs, layer, wg, bg, wu, bu, wd, bd):
    rows = xs.shape[0]
    depth, ne, d, f = wg.shape
    nb = rows // EXPERT_ROWS
    blk = lambda b, be, nu, nxt: (jnp.maximum(jnp.minimum(b, nu[0] - 1), 0), 0)
    wsel = lambda b, be, nu, nxt: (layer * ne + be[b], 0, 0)
    stack = lambda w: w.reshape((depth * ne,) + w.shape[2:])
    bias = lambda v: v.reshape(depth * ne, 1, v.shape[2])
    anywhere = pl.BlockSpec(memory_space=pl.ANY)
    return pl.pallas_call(
        functools.partial(_expert_kernel, first_weight=layer * ne),
        grid_spec=pltpu.PrefetchScalarGridSpec(
            num_scalar_prefetch=3,
            grid=(nb,),
            in_specs=[pl.BlockSpec((EXPERT_ROWS, d), blk),
                      pl.BlockSpec((1, 1, f), wsel), pl.BlockSpec((1, 1, f), wsel), pl.BlockSpec((1, 1, d), wsel),
                      anywhere, anywhere, anywhere],
            out_specs=pl.BlockSpec((EXPERT_ROWS, d), lambda b, be, nu, nxt: (b, 0)),
            scratch_shapes=[pltpu.VMEM((d, f), F32), pltpu.VMEM((d, f), F32), pltpu.VMEM((f, d), F32),
                            pltpu.VMEM((d, f), BF16), pltpu.VMEM((d, f), BF16), pltpu.VMEM((f, d), BF16),
                            pltpu.SemaphoreType.DMA((3,))]),
        out_shape=jax.ShapeDtypeStruct((rows, d), F32),
        compiler_params=_cparams(("arbitrary",)),
        name="moe_experts",
    )(block_expert, n_used, next_expert, xs, bias(bg), bias(bu), bias(bd), stack(wg), stack(wu), stack(wd))


def _combine_kernel(n_ref, local_ref, global_ref, ys_ref, slot_ref, gate_ref, x_ref, g_ref, b_ref, o_ref,
                    buf, sem, *, tile0, ne, alpha):
    step = pl.program_id(0)
    tile = step + tile0
    cur = step % 2

    def copy(b):
        return lambda n, lo, go: pltpu.make_async_copy(ys_ref.at[pl.ds(go, n)], buf.at[b, pl.ds(lo, n)], sem.at[b])

    def fetch(t, b):
        covered = TOP_K * slot_ref.shape[0]
        buf[b, covered:] = jnp.zeros((buf.shape[1] - covered, buf.shape[2]), F32)
        _for_each_segment(n_ref, local_ref, global_ref, t, ne, lambda *a: copy(b)(*a).start())

    @pl.when(step == 0)
    def _():
        fetch(tile, cur)

    @pl.when(step < pl.num_programs(0) - 1)
    def _():
        fetch(tile + 1, 1 - cur)

    slot = slot_ref[...]
    gate = gate_ref[...]
    rows = buf.shape[1]
    r = lax.broadcasted_iota(I32, (slot.shape[0], rows), 1)
    weights = jnp.zeros(r.shape, F32)
    for k in range(TOP_K):
        weights = jnp.where(slot[:, k:k + 1] == r, gate[:, k:k + 1], weights)
    _for_each_segment(n_ref, local_ref, global_ref, tile, ne, lambda *a: copy(cur)(*a).wait())
    y = jnp.dot(weights.astype(BF16), buf[cur].astype(BF16), preferred_element_type=F32)
    o_ref[...] = _layer_norm(alpha * x_ref[...] + y, g_ref[...], b_ref[...])


def _combine(seg, ys, slot_t, gate_t, x, g, b, tile0, tt, alpha):
    m, d = x.shape
    return pl.pallas_call(
        functools.partial(_combine_kernel, tile0=tile0, ne=NUM_EXPERTS, alpha=alpha),
        grid_spec=pltpu.PrefetchScalarGridSpec(
            num_scalar_prefetch=3,
            grid=(m // tt,),
            in_specs=[pl.BlockSpec(memory_space=pl.ANY),
                      pl.BlockSpec((tt, TOP_K), lambda i, *_: (i, 0)),
                      pl.BlockSpec((tt, TOP_K), lambda i, *_: (i, 0)),
                      pl.BlockSpec((tt, d), lambda i, *_: (i, 0)),
                      pl.BlockSpec((1, d), lambda i, *_: (0, 0)),
                      pl.BlockSpec((1, d), lambda i, *_: (0, 0))],
            out_specs=pl.BlockSpec((tt, d), lambda i, *_: (i, 0)),
            scratch_shapes=[pltpu.VMEM((2, _sorted_rows(tt), d), F32), pltpu.SemaphoreType.DMA((2,))]),
        out_shape=jax.ShapeDtypeStruct((m, d), F32),
        compiler_params=_cparams(("arbitrary",)),
        name="moe_combine",
    )(*seg, ys, slot_t, gate_t, x, g, b)


def _moe_ln(xs_groups, routed, layer, wg, bg, wu, bu, wd, bd, g, b, alpha):
    ne = wg.shape[1]
    assert ne == NUM_EXPERTS
    d = xs_groups[0].shape[1]
    tiles = [_tile(x.shape[0], MOE_TILE) for x in xs_groups]
    cnt = jnp.concatenate([r[2][:, :, 0] for r in routed], axis=0)
    n_tiles = cnt.shape[0]
    n8 = (cnt + (SUBLANES - 1)) // SUBLANES * SUBLANES
    local = jnp.cumsum(n8, axis=1) - n8
    per_expert = jnp.sum(n8, axis=0)
    per_expert_pad = (per_expert + (EXPERT_ROWS - 1)) // EXPERT_ROWS * EXPERT_ROWS
    expert_base = jnp.cumsum(per_expert_pad) - per_expert_pad
    glob = expert_base[None, :] + jnp.cumsum(n8, axis=0) - n8
    seg = tuple(a.reshape(-1).astype(I32) for a in (n8, local, glob))
    max_rows = sum(_sorted_rows(tt) * (x.shape[0] // tt) for x, tt in zip(xs_groups, tiles)) + ne * (EXPERT_ROWS - 1)
    nb = -(-max_rows // EXPERT_ROWS)
    blocks_per_expert = per_expert_pad // EXPERT_ROWS
    n_used = jnp.sum(blocks_per_expert).astype(I32).reshape(1)
    ends = jnp.cumsum(blocks_per_expert)
    blocks = jnp.arange(nb, dtype=I32)
    block_expert = jnp.minimum(jnp.sum((ends[None, :] <= blocks[:, None]).astype(I32), axis=1), ne - 1)
    following = ends[block_expert]
    next_expert = jnp.where(following < n_used[0], block_expert[jnp.minimum(following, nb - 1)], -1).astype(I32)
    fills = (jnp.concatenate([per_expert_pad - per_expert, jnp.where(blocks >= n_used[0], EXPERT_ROWS, 0)]),
             jnp.concatenate([expert_base + per_expert, blocks * EXPERT_ROWS]))
    fills = tuple(a.astype(I32) for a in fills)

    xs = _dispatch(seg, fills, xs_groups, tiles, [r[0] for r in routed], nb * EXPERT_ROWS)
    ys = _experts(block_expert, n_used, next_expert, xs, layer, wg, bg, wu, bu, wd, bd)
    outs = []
    tile0 = 0
    for x, tt, r in zip(xs_groups, tiles, routed):
        outs.append(_combine(seg, ys, r[0].T, r[1].T, x, g, b, tile0, tt, alpha))
        tile0 += x.shape[0] // tt
    return outs


def _rope_table_kernel(freq_ref, cos_ref, sin_ref, *, start):
    pos = (lax.broadcasted_iota(I32, cos_ref.shape, 0) + start).astype(F32)
    ang = pos * freq_ref[...]
    cos_ref[...] = jnp.cos(ang)
    sin_ref[...] = jnp.sin(ang)


def _rope_table(start, n, rope_dim):
    half = rope_dim // 2
    freqs = ROPE_THETA ** (-jnp.arange(half, dtype=F32) / half)
    freq_row = jnp.concatenate([freqs, freqs]).reshape(1, rope_dim)
    rows = -(-n // SUBLANES) * SUBLANES
    shp = jax.ShapeDtypeStruct((rows, rope_dim), F32)
    full = pl.BlockSpec((rows, rope_dim), lambda i: (0, 0))
    cos, sin = pl.pallas_call(
        functools.partial(_rope_table_kernel, start=start),
        grid=(1,),
        in_specs=[pl.BlockSpec((1, rope_dim), lambda i: (0, 0))],
        out_specs=[full, full],
        out_shape=[shp, shp],
        name="rope_table",
    )(freq_row)
    return cos[:n], sin[:n]


def _swap_halves_signed(w):
    half = w.shape[-1] // 2
    return jnp.concatenate([-w[..., half:], w[..., :half]], axis=-1)


def _kv_kernel(x_ref, wc_ref, wr_ref, wrs_ref, g_ref, cos_ref, sin_ref, c_ref, kr_ref):
    x = x_ref[...].astype(BF16)
    c = jnp.dot(x, wc_ref[...], preferred_element_type=F32)
    c_ref[...] = _rms_norm(c, g_ref[...])
    r = jnp.dot(x, wr_ref[...], preferred_element_type=F32)
    rs = jnp.dot(x, wrs_ref[...], preferred_element_type=F32)
    kr_ref[...] = r * cos_ref[...] + rs * sin_ref[...]


def _shared_kv(x, wc, wr, wrs, g, cos, sin):
    m, d = x.shape
    rk = wc.shape[1]
    e = wr.shape[1]
    tm = _tile(min(m, cos.shape[0]), ROW_TILE)
    period = cos.shape[0] // tm
    full = lambda arr: pl.BlockSpec(arr.shape, lambda i: (0, 0))
    tab = pl.BlockSpec((tm, e), lambda i: (i % period, 0))
    return pl.pallas_call(
        _kv_kernel,
        grid=(m // tm,),
        in_specs=[pl.BlockSpec((tm, d), lambda i: (i, 0)), full(wc), full(wr), full(wrs), full(g), tab, tab],
        out_specs=[pl.BlockSpec((tm, rk), lambda i: (i, 0)), pl.BlockSpec((tm, e), lambda i: (i, 0))],
        out_shape=[jax.ShapeDtypeStruct((m, rk), F32), jax.ShapeDtypeStruct((m, e), F32)],
        compiler_params=_cparams(("parallel",)),
        name="shared_kv",
    )(x, wc, wr, wrs, g, cos, sin)


def _kv_up_kernel(c_ref, kr_ref, wk_ref, rep_ref, wv_ref, k_ref, v_ref):
    c = c_ref[...].astype(BF16)
    k = jnp.dot(c, wk_ref[...], preferred_element_type=F32)
    k = k + jnp.dot(kr_ref[...].astype(BF16), rep_ref[...], preferred_element_type=F32)
    k_ref[...] = k.astype(BF16)
    v_ref[...] = jnp.dot(c, wv_ref[...], preferred_element_type=F32).astype(BF16)


def _kv_up(c, kr, wk_pad, rep, wv):
    m, rk = c.shape
    tm = _tile(m, ROW_TILE)
    full = lambda arr: pl.BlockSpec(arr.shape, lambda i: (0, 0))
    row = lambda n: pl.BlockSpec((tm, n), lambda i: (i, 0))
    return pl.pallas_call(
        _kv_up_kernel,
        grid=(m // tm,),
        in_specs=[row(rk), row(kr.shape[1]), full(wk_pad), full(rep), full(wv)],
        out_specs=[row(wk_pad.shape[1]), row(wv.shape[1])],
        out_shape=[jax.ShapeDtypeStruct((m, wk_pad.shape[1]), BF16), jax.ShapeDtypeStruct((m, wv.shape[1]), BF16)],
        compiler_params=_cparams(("parallel",)),
        name="kv_up",
    )(c, kr, wk_pad, rep, wv)


def _q_prompt_kernel(x_ref, wc_ref, wm_ref, g_ref, wq_ref, wqs_ref, cos_ref, sin_ref, q_ref, qm_ref):
    x = x_ref[...].astype(BF16)
    qm_ref[...] = jnp.dot(x, wm_ref[...], preferred_element_type=F32).astype(BF16)
    cq = _rms_norm(jnp.dot(x, wc_ref[...], preferred_element_type=F32), g_ref[...]).astype(BF16)
    qa = jnp.dot(cq, wq_ref[...], preferred_element_type=F32)
    qb = jnp.dot(cq, wqs_ref[...], preferred_element_type=F32)
    cos = cos_ref[...]
    sin = sin_ref[...]
    for h in range(q_ref.shape[1] // LANES):
        sl = slice(h * LANES, (h + 1) * LANES)
        q_ref[:, sl] = (qa[:, sl] * cos + qb[:, sl] * sin).astype(BF16)


def _q_prompt(x, wc, wm, g, wq_pad, wqs_pad, cos_slot, sin_slot):
    m, d = x.shape
    tm = _tile(min(m, cos_slot.shape[0]), ROW_TILE)
    period = cos_slot.shape[0] // tm
    full = lambda arr: pl.BlockSpec(arr.shape, lambda i: (0, 0))
    tab = pl.BlockSpec((tm, LANES), lambda i: (i % period, 0))
    row = lambda n: pl.BlockSpec((tm, n), lambda i: (i, 0))
    return pl.pallas_call(
        _q_prompt_kernel,
        grid=(m // tm,),
        in_specs=[row(d), full(wc), full(wm), full(g), full(wq_pad), full(wqs_pad), tab, tab],
        out_specs=[row(wq_pad.shape[1]), row(wm.shape[1])],
        out_shape=[jax.ShapeDtypeStruct((m, wq_pad.shape[1]), BF16), jax.ShapeDtypeStruct((m, wm.shape[1]), BF16)],
        compiler_params=_cparams(("parallel",)),
        name="q_prompt",
    )(x, wc, wm, g, wq_pad, wqs_pad, cos_slot, sin_slot)


def _mla_prompt_kernel(q_ref, k_ref, v_ref, o_ref, *, v_dim, tk):
    qi = pl.program_id(2)
    tq = q_ref.shape[0]
    heads = [slice(hh * LANES, (hh + 1) * LANES) for hh in range(2)]
    q = [q_ref[:, sl] for sl in heads]

    def attend(carry, start, visible):
        kb = k_ref[pl.ds(start, tk), :]
        vb = v_ref[pl.ds(start, tk), :]
        out = []
        for hh in range(2):
            m_old, l_old, acc = carry[hh]
            s = lax.dot_general(q[hh], kb[:, heads[hh]], _NT, preferred_element_type=F32)
            if visible is not None:
                s = jnp.where(visible, s, MASK_VALUE)
            m_new = jnp.maximum(m_old, jnp.max(s, axis=-1, keepdims=True))
            a = jnp.exp2(m_old - m_new)
            p = jnp.exp2(s - m_new)
            out.append((m_new, a * l_old + jnp.sum(p, axis=-1, keepdims=True),
                        a * acc + jnp.dot(p.astype(BF16), vb, preferred_element_type=F32)))
        return tuple(out)

    init = tuple((jnp.full((tq, 1), -jnp.inf, F32), jnp.zeros((tq, 1), F32), jnp.zeros((tq, 2 * v_dim), F32))
                 for _ in range(2))
    per_tile = tq // tk
    carry = lax.fori_loop(0, qi * per_tile, lambda ki, c: attend(c, pl.multiple_of(ki * tk, tk), None), init)
    row = lax.broadcasted_iota(I32, (tq, tk), 0)
    col = lax.broadcasted_iota(I32, (tq, tk), 1)
    for j in range(per_tile):
        carry = attend(carry, pl.multiple_of(qi * tq + j * tk, tk), col + j * tk <= row)
    (_, l0, acc0), (_, l1, acc1) = carry
    lane = lax.broadcasted_iota(I32, (tq, 2 * v_dim), 1)
    o_ref[...] = jnp.where(lane < v_dim, acc0 / l0, acc1 / l1).astype(o_ref.dtype)


def _mla_prompt(q, k, v, batch):
    m, hw = q.shape
    heads = hw // LANES
    v_dim = v.shape[1] // heads
    assert heads % 2 == 0 and 2 * v_dim == LANES
    s = m // batch
    tq = _tile(s, ATTN_Q_TILE)
    tk = _tile(tq, ATTN_K_TILE)
    nq = s // tq
    return pl.pallas_call(
        functools.partial(_mla_prompt_kernel, v_dim=v_dim, tk=tk),
        grid=(batch, heads // 2, nq),
        in_specs=[pl.BlockSpec((tq, 2 * LANES), lambda b, h, qi: (b * nq + qi, h)),
                  pl.BlockSpec((s, 2 * LANES), lambda b, h, qi: (b, h)),
                  pl.BlockSpec((s, 2 * v_dim), lambda b, h, qi: (b, h))],
        out_specs=pl.BlockSpec((tq, 2 * v_dim), lambda b, h, qi: (b * nq + qi, h)),
        out_shape=jax.ShapeDtypeStruct((m, heads * v_dim), BF16),
        compiler_params=_cparams(("parallel", "parallel", "arbitrary")),
        name="mla_prompt",
    )(q, k, v)


def _q_sample_kernel(x_ref, wc_ref, wm_ref, g_ref, wn_ref, wr_ref, wrs_ref, wuk_ref, cos_ref, sin_ref,
                     ql_ref, qr_ref, qm_ref):
    x = x_ref[...].astype(BF16)
    qm_ref[...] = jnp.dot(x, wm_ref[...], preferred_element_type=F32)
    cq = _rms_norm(jnp.dot(x, wc_ref[...], preferred_element_type=F32), g_ref[...]).astype(BF16)
    cos = cos_ref[...]
    sin = sin_ref[...]
    for h in range(wn_ref.shape[0]):
        qn = jnp.dot(cq, wn_ref[h], preferred_element_type=F32).astype(BF16)
        ql_ref[h] = jnp.dot(qn, wuk_ref[h], preferred_element_type=F32).astype(BF16)
        qr = jnp.dot(cq, wr_ref[h], preferred_element_type=F32) * cos
        qr_ref[h] = (qr + jnp.dot(cq, wrs_ref[h], preferred_element_type=F32) * sin).astype(BF16)


def _q_sample(x, wc, wm, g, wn, wr, wrs, wuk_t, cos, sin):
    m, d = x.shape
    heads = wn.shape[0]
    rk = wuk_t.shape[2]
    e = wr.shape[2]
    full = lambda arr: pl.BlockSpec(arr.shape, lambda i: (0,) * arr.ndim)
    return pl.pallas_call(
        _q_sample_kernel,
        grid=(1,),
        in_specs=[full(a) for a in (x, wc, wm, g, wn, wr, wrs, wuk_t, cos, sin)],
        out_specs=[pl.BlockSpec((heads, m, rk), lambda i: (0, 0, 0)),
                   pl.BlockSpec((heads, m, e), lambda i: (0, 0, 0)),
                   pl.BlockSpec((m, wm.shape[1]), lambda i: (0, 0))],
        out_shape=[jax.ShapeDtypeStruct((heads, m, rk), BF16), jax.ShapeDtypeStruct((heads, m, e), BF16),
                   jax.ShapeDtypeStruct((m, wm.shape[1]), F32)],
        compiler_params=_cparams(("arbitrary",)),
        name="q_sample",
    )(x, wc, wm, g, wn, wr, wrs, wuk_t, cos, sin)


def _mla_sample_kernel(pt_ref, ql_ref, qr_ref, cn_ref, kn_ref, cache_c, cache_kr, o_ref,
                       kv_buf, kr_buf, s_ref, sem, *, n_pages, new_len, scale, chunk):
    b = pl.program_id(0)
    cur = b % 2
    page = cache_c.shape[1]
    past = kv_buf.shape[1]

    def for_each_page(req, buf, act):
        def body(p, c):
            src = pt_ref[req * n_pages + p]
            at = pl.multiple_of(p * page, page)
            act(pltpu.make_async_copy(cache_c.at[src], kv_buf.at[buf, pl.ds(at, page)], sem.at[buf]))
            act(pltpu.make_async_copy(cache_kr.at[src], kr_buf.at[buf, :, pl.ds(at, page)], sem.at[buf]))
            return c

        lax.fori_loop(0, n_pages, body, 0)

    @pl.when(b == 0)
    def _():
        for_each_page(b, cur, lambda c: c.start())

    @pl.when(b + 1 < pl.num_programs(0))
    def _():
        for_each_page(b + 1, 1 - cur, lambda c: c.start())

    ql = ql_ref[0]
    qr = qr_ref[0]
    nq, nk = ql.shape[0], cn_ref.shape[1]
    cn = cn_ref[0].astype(BF16)
    s_of_row = lax.broadcasted_iota(I32, (nq, nk), 0) % new_len
    j = lax.broadcasted_iota(I32, (nq, nk), 1)
    s_new = lax.dot_general(ql, cn, _NT, preferred_element_type=F32)
    s_new = (s_new + lax.dot_general(qr, kn_ref[0].astype(BF16), _NT, preferred_element_type=F32)) * scale
    s_new = jnp.where(j <= s_of_row, s_new, MASK_VALUE)

    for_each_page(b, cur, lambda c: c.wait())
    for c in range(past // chunk):
        keys = slice(c * chunk, (c + 1) * chunk)
        s = lax.dot_general(ql, kv_buf[cur, keys, :].astype(BF16), _NT, preferred_element_type=F32)
        s = s + jnp.dot(qr, kr_buf[cur, :, keys].astype(BF16), preferred_element_type=F32)
        s_ref[:, keys] = s * scale
    s = s_ref[...]
    m = jnp.maximum(jnp.max(s, axis=-1, keepdims=True), jnp.max(s_new, axis=-1, keepdims=True))
    p = jnp.exp(s - m)
    p_new = jnp.exp(s_new - m)
    l = jnp.sum(p, axis=-1, keepdims=True) + jnp.sum(p_new, axis=-1, keepdims=True)
    acc = jnp.dot(p_new.astype(BF16), cn, preferred_element_type=F32)
    p = p.astype(BF16)
    for c in range(past // chunk):
        keys = slice(c * chunk, (c + 1) * chunk)
        acc = acc + jnp.dot(p[:, keys], kv_buf[cur, keys, :].astype(BF16), preferred_element_type=F32)
    o_ref[0] = acc / l


def _mla_sample(page_table, ql, qr, cache_c, cache_kr_t, c_new, kr_new, new_len, scale):
    bs, n_pages = page_table.shape
    _, nq, rk = ql.shape
    e = qr.shape[2]
    page = cache_c.shape[1]
    past = n_pages * page
    chunk = math.gcd(past, KEY_CHUNK)
    per_b = lambda n, w: pl.BlockSpec((1, n, w), lambda b, pt: (b, 0, 0))
    s8 = c_new.shape[1]
    return pl.pallas_call(
        functools.partial(_mla_sample_kernel, n_pages=n_pages, new_len=new_len, scale=scale, chunk=chunk),
        grid_spec=pltpu.PrefetchScalarGridSpec(
            num_scalar_prefetch=1,
            grid=(bs,),
            in_specs=[per_b(nq, rk), per_b(nq, e), per_b(s8, rk), per_b(s8, e),
                      pl.BlockSpec(memory_space=pl.ANY), pl.BlockSpec(memory_space=pl.ANY)],
            out_specs=per_b(nq, rk),
            scratch_shapes=[pltpu.VMEM((2, past, rk), F32), pltpu.VMEM((2, e, past), F32),
                            pltpu.VMEM((nq, past), F32), pltpu.SemaphoreType.DMA((2,))]),
        out_shape=jax.ShapeDtypeStruct((bs, nq, rk), F32),
        compiler_params=_cparams(("arbitrary",)),
        name="mla_sample",
    )(page_table.reshape(-1), ql, qr, c_new, kr_new, cache_c, cache_kr_t)


def _head_up_kernel(x_ref, w_ref, o_ref):
    o_ref[0] = jnp.dot(x_ref[0].astype(BF16), w_ref[0], preferred_element_type=F32)


def _head_up(lat, wv):
    h, m, rk = lat.shape
    v = wv.shape[2]
    return pl.pallas_call(
        _head_up_kernel,
        grid=(h,),
        in_specs=[pl.BlockSpec((1, m, rk), lambda i: (i, 0, 0)), pl.BlockSpec((1, rk, v), lambda i: (i, 0, 0))],
        out_specs=pl.BlockSpec((1, m, v), lambda i: (i, 0, 0)),
        out_shape=jax.ShapeDtypeStruct((h, m, v), F32),
        compiler_params=_cparams(("parallel",)),
        name="head_up",
    )(lat, wv)


def _head_slots(w, heads, width):
    k = w.shape[0]
    w = w.reshape(k, heads, width)
    return jnp.pad(w, ((0, 0), (0, 0), (0, LANES - width))).reshape(k, heads * LANES)


def kernel(x_prompt, x_sample, cache_kv_latent, cache_k_rope, cache_mem_k, cache_mem_v, state_conv, page_table, mem_prompt, w_in_a, conv_w, w_out_a, w_in_b, g_q, w_uq, w_out_b, w_kv_down, g_kv, w_uk, w_uv, w_mem_k, w_mem_v, ln1_g, ln1_b, ln2_g, ln2_b, w_router, b_router, w_gate, b_gate, w_up, b_up, w_down, b_down):
    bp, sp, d = x_prompt.shape
    bs, ss, _ = x_sample.shape
    depth = ln1_g.shape[0]
    n_a = w_in_a.shape[0]
    d_conv = conv_w.shape[2]
    assert conv_w.shape[1] == 3
    n_mem = mem_prompt.shape[1]
    mem_heads = cache_mem_k.shape[3]
    mem_w = mem_heads * cache_mem_k.shape[4]
    kv_rank, heads, nope = w_uk.shape
    v_dim = w_uv.shape[2]
    rope_dim = cache_k_rope.shape[2]
    q_rank = g_q.shape[1]
    n_past = page_table.shape[1] * cache_kv_latent.shape[1]
    alpha = (2 * depth) ** 0.25
    scale = (nope + rope_dim) ** -0.5
    half = rope_dim // 2

    xp = x_prompt.reshape(bp * sp, d)
    xs = x_sample.reshape(bs * ss, d)

    w_mem = jnp.concatenate([w_mem_k[l] for l in range(depth)] + [w_mem_v[l] for l in range(depth)],
                            axis=1).astype(BF16)
    mem_k_flat, mem_v_flat = _mem_kv(mem_prompt.reshape(bp * n_mem, d), w_mem, depth)
    mem_k_p = mem_k_flat.reshape(depth, bp, n_mem, mem_w)
    mem_v_p = mem_v_flat.reshape(depth, bp, n_mem, mem_w)
    mem_k_s = cache_mem_k.reshape(depth, bs, n_mem, mem_w)
    mem_v_s = cache_mem_v.reshape(depth, bs, n_mem, mem_w)

    conv_p, conv_s = [], []
    c_p = kr_p = c_s = kr_s = None
    k_p = v_p = None
    tables = None

    for l in range(depth):
        g1, b1 = ln1_g[l].reshape(1, d), ln1_b[l].reshape(1, d)
        ln_route = functools.partial(_out_ln_route, g=g1, b=b1, wt=w_router[l].T, br=b_router[l].reshape(-1, 1),
                                     alpha=alpha)
        if l < n_a:
            w_in = w_in_a[l].astype(BF16)
            w_out = w_out_a[l].astype(BF16)
            splits = [d_conv, d_conv, d_conv, mem_w]
            y, qm, st = _proj_conv(xp, w_in, jnp.zeros((bp, 2, d_conv), F32), conv_w[l], bp)
            conv_p.append(st)
            ym = _mem_attn(qm.reshape(bp, sp, mem_w), mem_k_p, mem_v_p, l, mem_heads).reshape(bp * sp, mem_w)
            xp, *route_p = ln_route(y, ym, xp, w_out[:d_conv], w_out[d_conv:])

            gb, gc, hv, qm = _proj(xs, w_in, splits, "in_proj_a")
            r3 = lambda a: a.reshape(bs, ss, d_conv)
            y, st = _conv_short(r3(gb), r3(gc), r3(hv), state_conv[l], conv_w[l])
            conv_s.append(st)
            ym = _mem_attn(qm.reshape(bs, ss, mem_w), mem_k_s, mem_v_s, l, mem_heads).reshape(bs * ss, mem_w)
            xs, *route_s = ln_route(y.reshape(bs * ss, d_conv), ym, xs, w_out[:d_conv], w_out[d_conv:])
        else:
            j = l - n_a
            if tables is None:
                cos_p, sin_p = _rope_table(0, sp, rope_dim)
                cos_s, sin_s = _rope_table(n_past, ss, rope_dim)
                cos_s, sin_s = jnp.tile(cos_s, (bs, 1)), jnp.tile(sin_s, (bs, 1))
                pad = LANES - nope - rope_dim
                c2 = scale * math.log2(math.e)
                cos_slot = c2 * jnp.concatenate([jnp.ones((sp, nope), F32), cos_p, jnp.zeros((sp, pad), F32)], axis=1)
                sin_slot = c2 * jnp.concatenate([jnp.zeros((sp, nope), F32), sin_p, jnp.zeros((sp, pad), F32)], axis=1)
                tables = True
                wc = w_kv_down[:, :kv_rank].astype(BF16)
                wr = w_kv_down[:, kv_rank:]
                wrs = _swap_halves_signed(wr).astype(BF16)
                wr = wr.astype(BF16)
                gk = g_kv.reshape(1, kv_rank)
                c_p, kr_p = _shared_kv(xp, wc, wr, wrs, gk, cos_p, sin_p)
                c_s, kr_s = _shared_kv(xs, wc, wr, wrs, gk, cos_s, sin_s)
                wk_pad = _head_slots(w_uk.reshape(kv_rank, heads * nope), heads, nope).astype(BF16)
                rep = jnp.pad(jnp.eye(rope_dim, dtype=F32), ((0, 0), (nope, pad)))
                rep = jnp.tile(rep, (1, heads)).astype(BF16)
                k_p, v_p = _kv_up(c_p, kr_p, wk_pad, rep, w_uv.reshape(kv_rank, heads * v_dim).astype(BF16))
                pad_rows = -ss % SUBLANES
                c_new = jnp.pad(c_s.reshape(bs, ss, kv_rank), ((0, 0), (0, pad_rows), (0, 0)))
                kr_new = jnp.pad(kr_s.reshape(bs, ss, rope_dim), ((0, 0), (0, pad_rows), (0, 0)))
                cache_kr_t = jnp.transpose(cache_k_rope, (0, 2, 1))
                wuk_t = jnp.transpose(w_uk, (1, 2, 0)).astype(BF16)
                wuv_h = jnp.transpose(w_uv, (1, 0, 2)).astype(BF16)

            w_in = w_in_b[j]
            wcq = w_in[:, :q_rank].astype(BF16)
            wqm = w_in[:, q_rank:].astype(BF16)
            gq = g_q[j].reshape(1, q_rank)
            wq = w_uq[j].reshape(q_rank, heads, nope + rope_dim)
            wq_nope = wq[:, :, :nope]
            wq_rope = wq[:, :, nope:]
            wq_rope_sw = _swap_halves_signed(wq_rope)
            w_out = w_out_b[j].astype(BF16)
            hv_w = heads * v_dim

            slot = lambda a, b: _head_slots(jnp.concatenate([a, b], axis=2).reshape(q_rank, -1), heads,
                                            nope + rope_dim).astype(BF16)
            wq_pad = slot(wq_nope, wq_rope)
            wqs_pad = slot(jnp.zeros_like(wq_nope), wq_rope_sw)
            q, qm = _q_prompt(xp, wcq, wqm, gq, wq_pad, wqs_pad, cos_slot, sin_slot)
            att = _mla_prompt(q, k_p, v_p, bp)
            ym = _mem_attn(qm.reshape(bp, sp, mem_w), mem_k_p, mem_v_p, l, mem_heads).reshape(bp * sp, mem_w)
            xp, *route_p = ln_route(att, ym, xp, w_out[:hv_w], w_out[hv_w:])

            th = lambda a: jnp.transpose(a, (1, 0, 2)).astype(BF16)
            ql, qr, qm = _q_sample(xs, wcq, wqm, gq, th(wq_nope), th(wq_rope), th(wq_rope_sw), wuk_t,
                                   cos_s, sin_s)
            per_req = lambda a: jnp.transpose(a.reshape(heads, bs, ss, -1), (1, 0, 2, 3)).reshape(bs, heads * ss, -1)
            lat = _mla_sample(page_table, per_req(ql), per_req(qr), cache_kv_latent, cache_kr_t,
                              c_new, kr_new, ss, scale)
            lat = jnp.transpose(lat.reshape(bs, heads, ss, kv_rank), (1, 0, 2, 3)).reshape(heads, bs * ss, kv_rank)
            att = jnp.transpose(_head_up(lat, wuv_h), (1, 0, 2)).reshape(bs * ss, hv_w)
            ym = _mem_attn(qm.reshape(bs, ss, mem_w), mem_k_s, mem_v_s, l, mem_heads).reshape(bs * ss, mem_w)
            xs, *route_s = ln_route(att, ym, xs, w_out[:hv_w], w_out[hv_w:])

        xp, xs = _moe_ln([xp, xs], [route_p, route_s], l, w_gate, b_gate, w_up, b_up, w_down, b_down,
                         ln2_g[l].reshape(1, d), ln2_b[l].reshape(1, d), alpha)

    ne_shape = (depth, bp, n_mem, mem_heads, mem_w // mem_heads)
    return (xp.reshape(bp, sp, d), xs.reshape(bs, ss, d),
            c_p.reshape(bp, sp, kv_rank), kr_p.reshape(bp, sp, rope_dim),
            c_s.reshape(bs, ss, kv_rank), kr_s.reshape(bs, ss, rope_dim),
            mem_k_p.reshape(ne_shape), mem_v_p.reshape(ne_shape),
            jnp.stack(conv_p), jnp.stack(conv_s))
```

```python
import functools
import math

import jax
import jax.numpy as jnp
from jax import lax
from jax.experimental import pallas as pl
from jax.experimental.pallas import tpu as pltpu

F32 = jnp.float32
BF16 = jnp.bfloat16
I32 = jnp.int32
U32 = jnp.uint32

TOP_K = 4
SWIGLU_LIMIT = 7.0
SWIGLU_ALPHA = 1.702
ROPE_THETA = 10000.0
LN_EPS = 1e-5
RMS_EPS = 1e-6
MASK_VALUE = -1e30

LANES = 128
SUBLANES = 8
VMEM_LIMIT = 56 * 1024 * 1024

ROW_TILE = 512
MOE_TILE = 512
EXPERT_ROWS = 512
KEY_CHUNK = 2048
MEM_ATTN_ROWS = 32
ATTN_Q_TILE = 512
ATTN_K_TILE = 512
NUM_EXPERTS = 32

_NT = (((1,), (1,)), ((), ()))


def _cparams(sem):
    return pltpu.CompilerParams(dimension_semantics=sem, vmem_limit_bytes=VMEM_LIMIT)


def _tile(n, pref):
    t = min(n, pref)
    assert n % t == 0, (n, pref)
    return t


def _layer_norm(z, g, b):
    mu = jnp.mean(z, axis=-1, keepdims=True)
    zc = z - mu
    var = jnp.mean(zc * zc, axis=-1, keepdims=True)
    return zc * lax.rsqrt(var + LN_EPS) * g + b


def _rms_norm(x, g):
    return x * lax.rsqrt(jnp.mean(x * x, axis=-1, keepdims=True) + RMS_EPS) * g


def _proj_kernel(x_ref, w_ref, *o_refs):
    h = jnp.dot(x_ref[...].astype(BF16), w_ref[...], preferred_element_type=F32)
    off = 0
    for o_ref in o_refs:
        n = o_ref.shape[-1]
        o_ref[...] = h[:, off:off + n].astype(o_ref.dtype)
        off += n


def _proj(x, w, splits, name):
    m, k = x.shape
    n = w.shape[1]
    assert sum(splits) == n
    tm = _tile(m, ROW_TILE)
    return pl.pallas_call(
        _proj_kernel,
        grid=(m // tm,),
        in_specs=[pl.BlockSpec((tm, k), lambda i: (i, 0)), pl.BlockSpec((k, n), lambda i: (0, 0))],
        out_specs=[pl.BlockSpec((tm, s), lambda i: (i, 0)) for s in splits],
        out_shape=[jax.ShapeDtypeStruct((m, s), F32) for s in splits],
        compiler_params=_cparams(("parallel",)),
        name=name,
    )(x, w)


def _memkv_kernel(x_ref, w_ref, k_ref, v_ref):
    depth, _, f = k_ref.shape
    h = jnp.dot(x_ref[...].astype(BF16), w_ref[...], preferred_element_type=F32)
    for l in range(depth):
        k_ref[l] = h[:, l * f:(l + 1) * f]
        v_ref[l] = h[:, (depth + l) * f:(depth + l + 1) * f]


def _mem_kv(mem, w_cat, depth):
    r, d = mem.shape
    f = w_cat.shape[1] // (2 * depth)
    tm = _tile(r, ROW_TILE)
    shp = jax.ShapeDtypeStruct((depth, r, f), F32)
    spec = pl.BlockSpec((depth, tm, f), lambda i: (0, i, 0))
    return pl.pallas_call(
        _memkv_kernel,
        grid=(r // tm,),
        in_specs=[pl.BlockSpec((tm, d), lambda i: (i, 0)), pl.BlockSpec(w_cat.shape, lambda i: (0, 0))],
        out_specs=[spec, spec],
        out_shape=[shp, shp],
        compiler_params=_cparams(("parallel",)),
        name="mem_kv",
    )(mem, w_cat)


def _proj_conv_kernel(x_ref, w_ref, pre_ref, cw_ref, y_ref, qm_ref, st_ref, carry):
    j = pl.program_id(1)

    @pl.when(j == 0)
    def _():
        carry[0:2, :] = pre_ref[0]

    c = y_ref.shape[1]
    h = jnp.dot(x_ref[...].astype(BF16), w_ref[...], preferred_element_type=F32)
    qm_ref[...] = h[:, 3 * c:].astype(qm_ref.dtype)
    u = h[:, c:2 * c] * h[:, 2 * c:3 * c]
    ts = u.shape[0]
    c0 = carry[0:1, :]
    c1 = carry[1:2, :]
    row = lax.broadcasted_iota(I32, u.shape, 0)
    u1 = jnp.where(row == 0, c1, pltpu.roll(u, 1, 0))
    u2 = jnp.where(row == 0, c0, jnp.where(row == 1, c1, pltpu.roll(u, 2, 0)))
    w = cw_ref[...]
    y_ref[...] = (h[:, :c] * (w[0:1] * u2 + w[1:2] * u1 + w[2:3] * u)).astype(y_ref.dtype)
    last = u[ts - 2:ts, :]
    carry[0:2, :] = last
    st_ref[0] = last


def _proj_conv(x, w_in, prefix, cw, batch):
    m, d = x.shape
    c = cw.shape[1]
    n = w_in.shape[1]
    s = m // batch
    ts = _tile(s, ROW_TILE)
    nt = s // ts
    row = lambda width: pl.BlockSpec((ts, width), lambda b, j: (b * nt + j, 0))
    state = pl.BlockSpec((1, 2, c), lambda b, j: (b, 0, 0))
    return pl.pallas_call(
        _proj_conv_kernel,
        grid=(batch, nt),
        in_specs=[row(d), pl.BlockSpec((d, n), lambda b, j: (0, 0)), state, pl.BlockSpec((3, c), lambda b, j: (0, 0))],
        out_specs=[row(c), row(n - 3 * c), state],
        out_shape=[jax.ShapeDtypeStruct((m, c), BF16), jax.ShapeDtypeStruct((m, n - 3 * c), BF16),
                   jax.ShapeDtypeStruct((batch, 2, c), F32)],
        scratch_shapes=[pltpu.VMEM((SUBLANES, c), F32)],
        compiler_params=_cparams(("parallel", "arbitrary")),
        name="proj_conv",
    )(x, w_in, prefix, cw)


def _conv_short_kernel(gb_ref, gc_ref, hv_ref, pre_ref, w_ref, y_ref, st_ref):
    s_len = gb_ref.shape[1]
    w = w_ref[...]

    def u(k):
        if k < 0:
            return pre_ref[:, 2 + k, :]
        return gc_ref[:, k, :] * hv_ref[:, k, :]

    for s in range(s_len):
        y_ref[:, s, :] = gb_ref[:, s, :] * (w[0:1] * u(s - 2) + w[1:2] * u(s - 1) + w[2:3] * u(s))
    for k in range(2):
        st_ref[:, k, :] = u(s_len - 2 + k)


def _conv_short(gb, gc, hv, prefix, w):
    batch, s, c = gb.shape
    full = lambda shp: pl.BlockSpec(shp, lambda i: (0,) * len(shp))
    return pl.pallas_call(
        _conv_short_kernel,
        grid=(1,),
        in_specs=[full(gb.shape)] * 3 + [full(prefix.shape), full(w.shape)],
        out_specs=[full(gb.shape), full(prefix.shape)],
        out_shape=[jax.ShapeDtypeStruct(gb.shape, F32), jax.ShapeDtypeStruct(prefix.shape, F32)],
        compiler_params=_cparams(("arbitrary",)),
        name="conv_short",
    )(gb, gc, hv, prefix, w)


def _mem_attn_kernel(q_ref, k_ref, v_ref, o_ref, *, heads):
    hd = q_ref.shape[2] // heads
    for i in range(q_ref.shape[0]):
        q = q_ref[i].astype(BF16)
        k = k_ref[0, i].astype(BF16)
        v = v_ref[0, i].astype(BF16)
        outs = []
        for h in range(heads):
            sl = slice(h * hd, (h + 1) * hd)
            s = lax.dot_general(q[:, sl], k[:, sl], _NT, preferred_element_type=F32) * (hd ** -0.5)
            m = jnp.max(s, axis=-1, keepdims=True)
            p = jnp.exp(s - m)
            l = jnp.sum(p, axis=-1, keepdims=True)
            outs.append(jnp.dot(p.astype(BF16), v[:, sl], preferred_element_type=F32) / l)
        o_ref[i] = jnp.concatenate(outs, axis=-1).astype(o_ref.dtype)


def _mem_attn(q, mk, mv, layer, heads):
    b, s, w = q.shape
    nm = mk.shape[2]
    tq = _tile(s, ROW_TILE)
    bb = math.gcd(b, max(1, MEM_ATTN_ROWS // tq))
    return pl.pallas_call(
        functools.partial(_mem_attn_kernel, heads=heads),
        grid=(b // bb, s // tq),
        in_specs=[pl.BlockSpec((bb, tq, w), lambda i, j: (i, j, 0)),
                  pl.BlockSpec((1, bb, nm, w), lambda i, j: (layer, i, 0, 0)),
                  pl.BlockSpec((1, bb, nm, w), lambda i, j: (layer, i, 0, 0))],
        out_specs=pl.BlockSpec((bb, tq, w), lambda i, j: (i, j, 0)),
        out_shape=jax.ShapeDtypeStruct((b, s, w), BF16),
        compiler_params=_cparams(("parallel", "parallel")),
        name="mem_attn",
    )(q, mk, mv)


def _out_ln_route_kernel(a_ref, m_ref, x_ref, wa_ref, wm_ref, g_ref, b_ref, wt_ref, br_ref,
                         o_ref, pos_ref, gate_ref, cnt_ref, *, alpha):
    o = jnp.dot(a_ref[...].astype(BF16), wa_ref[...], preferred_element_type=F32)
    o = o + jnp.dot(m_ref[...].astype(BF16), wm_ref[...], preferred_element_type=F32)
    x = _layer_norm(alpha * x_ref[...] + o, g_ref[...], b_ref[...])
    o_ref[...] = x
    _route(x, wt_ref[...], br_ref[...], pos_ref, gate_ref, cnt_ref)


def _out_ln_route(a, mem, x, wa, wm, g, b, wt, br, alpha):
    m, d = x.shape
    ne = wt.shape[0]
    tt = _tile(m, MOE_TILE)
    nt = m // tt
    row = lambda n: pl.BlockSpec((tt, n), lambda i: (i, 0))
    full = lambda arr: pl.BlockSpec(arr.shape, lambda i: (0, 0))
    return pl.pallas_call(
        functools.partial(_out_ln_route_kernel, alpha=alpha),
        grid=(nt,),
        in_specs=[row(a.shape[1]), row(mem.shape[1]), row(d), full(wa), full(wm), full(g), full(b), full(wt), full(br)],
        out_specs=[row(d),
                   pl.BlockSpec((TOP_K, tt), lambda i: (0, i)),
                   pl.BlockSpec((TOP_K, tt), lambda i: (0, i)),
                   pl.BlockSpec((1, ne, 1), lambda i: (i, 0, 0))],
        out_shape=[jax.ShapeDtypeStruct((m, d), F32),
                   jax.ShapeDtypeStruct((TOP_K, m), I32),
                   jax.ShapeDtypeStruct((TOP_K, m), F32),
                   jax.ShapeDtypeStruct((nt, ne, 1), I32)],
        compiler_params=_cparams(("parallel",)),
        name="out_ln_route",
    )(a, mem, x, wa, wm, g, b, wt, br)


def _route(x, wt, bias, pos_ref, gate_ref, cnt_ref):
    logits = lax.dot_general(wt, x, _NT, precision=lax.Precision.HIGHEST, preferred_element_type=F32) + bias
    ne, tt = logits.shape
    eidx = lax.broadcasted_iota(I32, (ne, tt), 0)
    sels, vals = [], []
    l = logits
    for _ in range(TOP_K):
        m = jnp.max(l, axis=0, keepdims=True)
        first = jnp.min(jnp.where(l == m, eidx, ne), axis=0, keepdims=True)
        sel = eidx == first
        sels.append(sel)
        vals.append(m)
        l = jnp.where(sel, -jnp.inf, l)
    ex = [jnp.exp(v - vals[0]) for v in vals]
    den = ex[0]
    for e in ex[1:]:
        den = den + e
    gate_ref[...] = jnp.concatenate([e / den for e in ex], axis=0)

    chosen = jnp.zeros((ne, tt), F32)
    for sel in sels:
        chosen = jnp.where(sel, 1.0, chosen)
    cnt = jnp.sum(chosen, axis=1, keepdims=True)
    cnt_ref[0] = cnt.astype(I32)
    cnt8 = jnp.floor((cnt + (SUBLANES - 1)) * (1.0 / SUBLANES)) * SUBLANES
    er = lax.broadcasted_iota(I32, (ne, ne), 0)
    ec = lax.broadcasted_iota(I32, (ne, ne), 1)
    below = jnp.where(ec < er, 1.0, 0.0)
    seg_start = jnp.dot(below, jnp.broadcast_to(cnt8, (ne, LANES)), precision=lax.Precision.HIGHEST,
                        preferred_element_type=F32)[:, 0:1]
    tr = lax.broadcasted_iota(I32, (tt, tt), 0)
    tc = lax.broadcasted_iota(I32, (tt, tt), 1)
    earlier = jnp.where(tr < tc, 1.0, 0.0).astype(BF16)
    rank = jnp.dot(chosen.astype(BF16), earlier, preferred_element_type=F32)
    slot = seg_start + rank
    pos_ref[...] = jnp.concatenate(
        [jnp.sum(jnp.where(sel, slot, 0.0), axis=0, keepdims=True) for sel in sels], axis=0).astype(I32)


def _sorted_rows(tt):
    return TOP_K * tt + NUM_EXPERTS * SUBLANES


def _for_each_segment(n_ref, local_ref, global_ref, tile, ne, fn):
    def body(e, c):
        k = tile * ne + e
        n = pl.multiple_of(n_ref[k], SUBLANES)

        @pl.when(n > 0)
        def _():
            fn(n, pl.multiple_of(local_ref[k], SUBLANES), pl.multiple_of(global_ref[k], SUBLANES))
        return c

    lax.fori_loop(0, ne, body, 0)


def _dispatch_kernel(n_ref, local_ref, global_ref, fill_n_ref, fill_at_ref, *refs, ne, group_tiles):
    xs_ref, buf, zeros, sem = refs[-4:]
    step = pl.program_id(0)
    last = pl.num_programs(0) - 1
    tile = step
    cur = step % 2

    def sort_tile(x_ref, slot_ref):
        x = x_ref[...].astype(BF16)
        slot = slot_ref[...]
        rows = _sorted_rows(x.shape[0])
        r = lax.broadcasted_iota(I32, (rows, x.shape[0]), 0)
        onehot = jnp.zeros(r.shape, F32)
        for k in range(TOP_K):
            onehot = jnp.where(slot[k:k + 1, :] == r, 1.0, onehot)
        buf[cur, :rows] = jnp.dot(onehot.astype(BF16), x, preferred_element_type=F32)

    first = 0
    for g, nt in enumerate(group_tiles):
        pl.when((step >= first) & (step < first + nt))(
            functools.partial(sort_tile, refs[2 * g], refs[2 * g + 1]))
        first += nt

    def copy(b):
        return lambda n, lo, go: pltpu.make_async_copy(buf.at[b, pl.ds(lo, n)], xs_ref.at[pl.ds(go, n)], sem.at[b])

    _for_each_segment(n_ref, local_ref, global_ref, tile, ne, lambda *a: copy(cur)(*a).start())

    def fill_copies(act):
        def body(k, c):
            n = pl.multiple_of(fill_n_ref[k], SUBLANES)
            at = pl.multiple_of(fill_at_ref[k], SUBLANES)

            @pl.when(n > 0)
            def _():
                act(pltpu.make_async_copy(zeros.at[pl.ds(0, n)], xs_ref.at[pl.ds(at, n)], sem.at[2]))
            return c

        lax.fori_loop(0, fill_n_ref.shape[0], body, 0)

    @pl.when(step == 0)
    def _():
        zeros[...] = jnp.zeros(zeros.shape, F32)
        fill_copies(lambda c: c.start())

    @pl.when(step > 0)
    def _():
        _for_each_segment(n_ref, local_ref, global_ref, tile - 1, ne, lambda *a: copy(1 - cur)(*a).wait())

    @pl.when(step == last)
    def _():
        _for_each_segment(n_ref, local_ref, global_ref, tile, ne, lambda *a: copy(cur)(*a).wait())
        fill_copies(lambda c: c.wait())


def _dispatch(seg, fills, groups, tiles, slots, rows_total):
    d = groups[0].shape[1]
    group_tiles = [x.shape[0] // tt for x, tt in zip(groups, tiles)]
    in_specs, args, first = [], [], 0
    for x, tt, slot, nt in zip(groups, tiles, slots, group_tiles):
        own = lambda i, first=first, nt=nt: jnp.clip(i - first, 0, nt - 1)
        in_specs += [pl.BlockSpec((tt, d), lambda i, *_, own=own: (own(i), 0)),
                     pl.BlockSpec((TOP_K, tt), lambda i, *_, own=own: (0, own(i)))]
        args += [x, slot]
        first += nt
    return pl.pallas_call(
        functools.partial(_dispatch_kernel, ne=NUM_EXPERTS, group_tiles=tuple(group_tiles)),
        grid_spec=pltpu.PrefetchScalarGridSpec(
            num_scalar_prefetch=5,
            grid=(sum(group_tiles),),
            in_specs=in_specs,
            out_specs=pl.BlockSpec(memory_space=pl.ANY),
            scratch_shapes=[pltpu.VMEM((2, _sorted_rows(max(tiles)), d), F32), pltpu.VMEM((EXPERT_ROWS, d), F32),
                            pltpu.SemaphoreType.DMA((3,))]),
        out_shape=jax.ShapeDtypeStruct((rows_total, d), F32),
        compiler_params=_cparams(("arbitrary",)),
        name="moe_dispatch",
    )(*seg, *fills, *args)


def _expert_kernel(be_ref, nu_ref, nxt_ref, xs_ref, bg_ref, bu_ref, bd_ref, wg_hbm, wu_hbm, wd_hbm, ys_ref,
                   wg_f32, wu_f32, wd_f32, wg_bf, wu_bf, wd_bf, sem, *, first_weight):
    half = xs_ref.shape[0] // 2
    h0 = 2 * pl.program_id(0)
    h1 = h0 + 1
    n_halves = nu_ref[0]
    used0 = h0 < n_halves
    used1 = h1 < n_halves
    e0 = be_ref[h0]
    e1 = be_ref[h1]
    first0 = used0 & ((h0 == 0) | (e0 != be_ref[jnp.maximum(h0 - 1, 0)]))
    first1 = used1 & (e1 != e0)

    def weight_copies(e, act):
        w = first_weight + e
        act(pltpu.make_async_copy(wg_hbm.at[w], wg_f32, sem.at[0]))
        act(pltpu.make_async_copy(wu_hbm.at[w], wu_f32, sem.at[1]))
        act(pltpu.make_async_copy(wd_hbm.at[w], wd_f32, sem.at[2]))

    def stage(h):
        @pl.when(h == 0)
        def _():
            weight_copies(be_ref[h], lambda c: c.start())

        weight_copies(be_ref[h], lambda c: c.wait())
        wg_bf[...] = wg_f32[...].astype(BF16)
        wu_bf[...] = wu_f32[...].astype(BF16)
        wd_bf[...] = wd_f32[...].astype(BF16)

        @pl.when(nxt_ref[h] >= 0)
        def _():
            weight_copies(nxt_ref[h], lambda c: c.start())

    def run(rows, e):
        w = pl.ds(first_weight + e, 1)
        x = xs_ref[rows, :].astype(BF16)
        g = jnp.dot(x, wg_bf[...], preferred_element_type=F32) + bg_ref[w, :]
        u = jnp.dot(x, wu_bf[...], preferred_element_type=F32) + bu_ref[w, :]
        g = jnp.minimum(g, SWIGLU_LIMIT)
        u = jnp.clip(u, -SWIGLU_LIMIT, SWIGLU_LIMIT)
        h = (u + 1.0) * (g * jax.nn.sigmoid(SWIGLU_ALPHA * g))
        ys_ref[rows, :] = jnp.dot(h.astype(BF16), wd_bf[...], preferred_element_type=F32) + bd_ref[w, :]

    def clear(rows):
        ys_ref[rows, :] = jnp.zeros((half, ys_ref.shape[1]), F32)

    lower, upper = slice(0, half), slice(half, 2 * half)
    whole = used1 & jnp.logical_not(first1)
    pl.when(first0)(functools.partial(stage, h0))
    pl.when(whole)(functools.partial(run, slice(None), e0))
    pl.when(used0 & jnp.logical_not(whole))(functools.partial(run, lower, e0))

    @pl.when(first1)
    def _():
        stage(h1)
        run(upper, e1)

    pl.when(jnp.logical_not(used0))(functools.partial(clear, lower))
    pl.when(jnp.logical_not(used1))(functools.partial(clear, upper))


def _experts(half_expert, n_halves, next_expert, xs, layer, wg, bg, wu, bu, wd, bd):
    rows = xs.shape[0]
    depth, ne, d, f = wg.shape
    nb = rows // EXPERT_ROWS
    blk = lambda b, be, nu, nxt: (jnp.clip(b, 0, (nu[0] + 1) // 2 - 1), 0)
    stack = lambda w: w.reshape((depth * ne,) + w.shape[2:])
    bias = lambda v: pl.BlockSpec((depth * ne, v.shape[2]), lambda b, be, nu, nxt: (0, 0))
    anywhere = pl.BlockSpec(memory_space=pl.ANY)
    return pl.pallas_call(
        functools.partial(_expert_kernel, first_weight=layer * ne),
        grid_spec=pltpu.PrefetchScalarGridSpec(
            num_scalar_prefetch=3,
            grid=(nb,),
            in_specs=[pl.BlockSpec((EXPERT_ROWS, d), blk), bias(bg), bias(bu), bias(bd),
                      anywhere, anywhere, anywhere],
            out_specs=pl.BlockSpec((EXPERT_ROWS, d), lambda b, be, nu, nxt: (b, 0)),
            scratch_shapes=[pltpu.VMEM((d, f), F32), pltpu.VMEM((d, f), F32), pltpu.VMEM((f, d), F32),
                            pltpu.VMEM((d, f), BF16), pltpu.VMEM((d, f), BF16), pltpu.VMEM((f, d), BF16),
                            pltpu.SemaphoreType.DMA((3,))]),
        out_shape=jax.ShapeDtypeStruct((rows, d), F32),
        compiler_params=_cparams(("arbitrary",)),
        name="moe_experts",
    )(half_expert, n_halves, next_expert, xs, stack(bg), stack(bu), stack(bd), stack(wg), stack(wu), stack(wd))


def _combine_kernel(n_ref, local_ref, global_ref, ys_ref, slot_ref, gate_ref, x_ref, g_ref, b_ref, o_ref,
                    buf, sem, *, tile0, ne, alpha):
    step = pl.program_id(0)
    tile = step + tile0
    cur = step % 2

    def copy(b):
        return lambda n, lo, go: pltpu.make_async_copy(ys_ref.at[pl.ds(go, n)], buf.at[b, pl.ds(lo, n)], sem.at[b])

    def fetch(t, b):
        covered = TOP_K * slot_ref.shape[0]
        buf[b, covered:] = jnp.zeros((buf.shape[1] - covered, buf.shape[2]), F32)
        _for_each_segment(n_ref, local_ref, global_ref, t, ne, lambda *a: copy(b)(*a).start())

    @pl.when(step == 0)
    def _():
        fetch(tile, cur)

    @pl.when(step < pl.num_programs(0) - 1)
    def _():
        fetch(tile + 1, 1 - cur)

    slot = slot_ref[...]
    gate = gate_ref[...]
    rows = buf.shape[1]
    r = lax.broadcasted_iota(I32, (slot.shape[0], rows), 1)
    weights = jnp.zeros(r.shape, F32)
    for k in range(TOP_K):
        weights = jnp.where(slot[:, k:k + 1] == r, gate[:, k:k + 1], weights)
    _for_each_segment(n_ref, local_ref, global_ref, tile, ne, lambda *a: copy(cur)(*a).wait())
    y = jnp.dot(weights.astype(BF16), buf[cur].astype(BF16), preferred_element_type=F32)
    o_ref[...] = _layer_norm(alpha * x_ref[...] + y, g_ref[...], b_ref[...])


def _combine(seg, ys, slot_t, gate_t, x, g, b, tile0, tt, alpha):
    m, d = x.shape
    return pl.pallas_call(
        functools.partial(_combine_kernel, tile0=tile0, ne=NUM_EXPERTS, alpha=alpha),
        grid_spec=pltpu.PrefetchScalarGridSpec(
            num_scalar_prefetch=3,
            grid=(m // tt,),
            in_specs=[pl.BlockSpec(memory_space=pl.ANY),
                      pl.BlockSpec((tt, TOP_K), lambda i, *_: (i, 0)),
                      pl.BlockSpec((tt, TOP_K), lambda i, *_: (i, 0)),
                      pl.BlockSpec((tt, d), lambda i, *_: (i, 0)),
                      pl.BlockSpec((1, d), lambda i, *_: (0, 0)),
                      pl.BlockSpec((1, d), lambda i, *_: (0, 0))],
            out_specs=pl.BlockSpec((tt, d), lambda i, *_: (i, 0)),
            scratch_shapes=[pltpu.VMEM((2, _sorted_rows(tt), d), F32), pltpu.SemaphoreType.DMA((2,))]),
        out_shape=jax.ShapeDtypeStruct((m, d), F32),
        compiler_params=_cparams(("arbitrary",)),
        name="moe_combine",
    )(*seg, ys, slot_t, gate_t, x, g, b)


def _moe_ln(xs_groups, routed, layer, wg, bg, wu, bu, wd, bd, g, b, alpha):
    ne = wg.shape[1]
    assert ne == NUM_EXPERTS
    d = xs_groups[0].shape[1]
    tiles = [_tile(x.shape[0], MOE_TILE) for x in xs_groups]
    cnt = jnp.concatenate([r[2][:, :, 0] for r in routed], axis=0)
    n_tiles = cnt.shape[0]
    n8 = (cnt + (SUBLANES - 1)) // SUBLANES * SUBLANES
    local = jnp.cumsum(n8, axis=1) - n8
    per_expert = jnp.sum(n8, axis=0)
    half = EXPERT_ROWS // 2
    per_expert_pad = (per_expert + (half - 1)) // half * half
    expert_base = jnp.cumsum(per_expert_pad) - per_expert_pad
    glob = expert_base[None, :] + jnp.cumsum(n8, axis=0) - n8
    seg = tuple(a.reshape(-1).astype(I32) for a in (n8, local, glob))
    max_rows = sum(_sorted_rows(tt) * (x.shape[0] // tt) for x, tt in zip(xs_groups, tiles)) + ne * (half - 1)
    nb = -(-max_rows // EXPERT_ROWS)
    halves_per_expert = per_expert_pad // half
    n_halves = jnp.sum(halves_per_expert).astype(I32).reshape(1)
    ends = jnp.cumsum(halves_per_expert)
    halves = jnp.arange(2 * nb, dtype=I32)
    half_expert = jnp.minimum(jnp.sum((ends[None, :] <= halves[:, None]).astype(I32), axis=1), ne - 1)
    following = ends[half_expert]
    next_expert = jnp.where(following < n_halves[0], half_expert[jnp.minimum(following, 2 * nb - 1)], -1).astype(I32)
    fills = (jnp.concatenate([per_expert_pad - per_expert, jnp.where(halves >= n_halves[0], half, 0)]),
             jnp.concatenate([expert_base + per_expert, halves * half]))
    fills = tuple(a.astype(I32) for a in fills)

    xs = _dispatch(seg, fills, xs_groups, tiles, [r[0] for r in routed], nb * EXPERT_ROWS)
    ys = _experts(half_expert, n_halves, next_expert, xs, layer, wg, bg, wu, bu, wd, bd)
    outs = []
    tile0 = 0
    for x, tt, r in zip(xs_groups, tiles, routed):
        outs.append(_combine(seg, ys, r[0].T, r[1].T, x, g, b, tile0, tt, alpha))
        tile0 += x.shape[0] // tt
    return outs


def _rope_table_kernel(freq_ref, cos_ref, sin_ref, *, start):
    pos = (lax.broadcasted_iota(I32, cos_ref.shape, 0) + start).astype(F32)
    ang = pos * freq_ref[...]
    cos_ref[...] = jnp.cos(ang)
    sin_ref[...] = jnp.sin(ang)


def _rope_table(start, n, rope_dim):
    half = rope_dim // 2
    freqs = ROPE_THETA ** (-jnp.arange(half, dtype=F32) / half)
    freq_row = jnp.concatenate([freqs, freqs]).reshape(1, rope_dim)
    rows = -(-n // SUBLANES) * SUBLANES
    shp = jax.ShapeDtypeStruct((rows, rope_dim), F32)
    full = pl.BlockSpec((rows, rope_dim), lambda i: (0, 0))
    cos, sin = pl.pallas_call(
        functools.partial(_rope_table_kernel, start=start),
        grid=(1,),
        in_specs=[pl.BlockSpec((1, rope_dim), lambda i: (0, 0))],
        out_specs=[full, full],
        out_shape=[shp, shp],
        name="rope_table",
    )(freq_row)
    return cos[:n], sin[:n]


def _swap_halves_signed(w):
    half = w.shape[-1] // 2
    return jnp.concatenate([-w[..., half:], w[..., :half]], axis=-1)


def _kv_kernel(x_ref, wc_ref, wr_ref, wrs_ref, g_ref, cos_ref, sin_ref, c_ref, kr_ref):
    x = x_ref[...].astype(BF16)
    c = jnp.dot(x, wc_ref[...], preferred_element_type=F32)
    c_ref[...] = _rms_norm(c, g_ref[...])
    r = jnp.dot(x, wr_ref[...], preferred_element_type=F32)
    rs = jnp.dot(x, wrs_ref[...], preferred_element_type=F32)
    kr_ref[...] = r * cos_ref[...] + rs * sin_ref[...]


def _shared_kv(x, wc, wr, wrs, g, cos, sin):
    m, d = x.shape
    rk = wc.shape[1]
    e = wr.shape[1]
    tm = _tile(min(m, cos.shape[0]), ROW_TILE)
    period = cos.shape[0] // tm
    full = lambda arr: pl.BlockSpec(arr.shape, lambda i: (0, 0))
    tab = pl.BlockSpec((tm, e), lambda i: (i % period, 0))
    return pl.pallas_call(
        _kv_kernel,
        grid=(m // tm,),
        in_specs=[pl.BlockSpec((tm, d), lambda i: (i, 0)), full(wc), full(wr), full(wrs), full(g), tab, tab],
        out_specs=[pl.BlockSpec((tm, rk), lambda i: (i, 0)), pl.BlockSpec((tm, e), lambda i: (i, 0))],
        out_shape=[jax.ShapeDtypeStruct((m, rk), F32), jax.ShapeDtypeStruct((m, e), F32)],
        compiler_params=_cparams(("parallel",)),
        name="shared_kv",
    )(x, wc, wr, wrs, g, cos, sin)


def _kv_up_kernel(c_ref, kr_ref, wk_ref, rep_ref, wv_ref, k_ref, v_ref):
    c = c_ref[...].astype(BF16)
    k = jnp.dot(c, wk_ref[...], preferred_element_type=F32)
    k = k + jnp.dot(kr_ref[...].astype(BF16), rep_ref[...], preferred_element_type=F32)
    k_ref[...] = k.astype(BF16)
    v_ref[...] = jnp.dot(c, wv_ref[...], preferred_element_type=F32).astype(BF16)


def _kv_up(c, kr, wk_pad, rep, wv):
    m, rk = c.shape
    tm = _tile(m, ROW_TILE)
    full = lambda arr: pl.BlockSpec(arr.shape, lambda i: (0, 0))
    row = lambda n: pl.BlockSpec((tm, n), lambda i: (i, 0))
    return pl.pallas_call(
        _kv_up_kernel,
        grid=(m // tm,),
        in_specs=[row(rk), row(kr.shape[1]), full(wk_pad), full(rep), full(wv)],
        out_specs=[row(wk_pad.shape[1]), row(wv.shape[1])],
        out_shape=[jax.ShapeDtypeStruct((m, wk_pad.shape[1]), BF16), jax.ShapeDtypeStruct((m, wv.shape[1]), BF16)],
        compiler_params=_cparams(("parallel",)),
        name="kv_up",
    )(c, kr, wk_pad, rep, wv)


def _q_prompt_kernel(x_ref, wc_ref, wm_ref, g_ref, wq_ref, wqs_ref, cos_ref, sin_ref, q_ref, qm_ref):
    x = x_ref[...].astype(BF16)
    qm_ref[...] = jnp.dot(x, wm_ref[...], preferred_element_type=F32).astype(BF16)
    cq = _rms_norm(jnp.dot(x, wc_ref[...], preferred_element_type=F32), g_ref[...]).astype(BF16)
    qa = jnp.dot(cq, wq_ref[...], preferred_element_type=F32)
    qb = jnp.dot(cq, wqs_ref[...], preferred_element_type=F32)
    cos = cos_ref[...]
    sin = sin_ref[...]
    for h in range(q_ref.shape[1] // LANES):
        sl = slice(h * LANES, (h + 1) * LANES)
        q_ref[:, sl] = (qa[:, sl] * cos + qb[:, sl] * sin).astype(BF16)


def _q_prompt(x, wc, wm, g, wq_pad, wqs_pad, cos_slot, sin_slot):
    m, d = x.shape
    tm = _tile(min(m, cos_slot.shape[0]), ROW_TILE)
    period = cos_slot.shape[0] // tm
    full = lambda arr: pl.BlockSpec(arr.shape, lambda i: (0, 0))
    tab = pl.BlockSpec((tm, LANES), lambda i: (i % period, 0))
    row = lambda n: pl.BlockSpec((tm, n), lambda i: (i, 0))
    return pl.pallas_call(
        _q_prompt_kernel,
        grid=(m // tm,),
        in_specs=[row(d), full(wc), full(wm), full(g), full(wq_pad), full(wqs_pad), tab, tab],
        out_specs=[row(wq_pad.shape[1]), row(wm.shape[1])],
        out_shape=[jax.ShapeDtypeStruct((m, wq_pad.shape[1]), BF16), jax.ShapeDtypeStruct((m, wm.shape[1]), BF16)],
        compiler_params=_cparams(("parallel",)),
        name="q_prompt",
    )(x, wc, wm, g, wq_pad, wqs_pad, cos_slot, sin_slot)


def _mla_prompt_kernel(q_ref, k_ref, v_ref, o_ref, *, v_dim, tk):
    qi = pl.program_id(2)
    tq = q_ref.shape[0]
    heads = [slice(hh * LANES, (hh + 1) * LANES) for hh in range(2)]
    q = [q_ref[:, sl] for sl in heads]

    def attend(carry, start, visible):
        kb = k_ref[pl.ds(start, tk), :]
        vb = v_ref[pl.ds(start, tk), :]
        out = []
        for hh in range(2):
            m_old, l_old, acc = carry[hh]
            s = lax.dot_general(q[hh], kb[:, heads[hh]], _NT, preferred_element_type=F32)
            if visible is not None:
                s = jnp.where(visible, s, MASK_VALUE)
            m_new = jnp.maximum(m_old, jnp.max(s, axis=-1, keepdims=True))
            a = jnp.exp2(m_old - m_new)
            p = jnp.exp2(s - m_new)
            out.append((m_new, a * l_old + jnp.sum(p, axis=-1, keepdims=True),
                        a * acc + jnp.dot(p.astype(BF16), vb, preferred_element_type=F32)))
        return tuple(out)

    init = tuple((jnp.full((tq, 1), -jnp.inf, F32), jnp.zeros((tq, 1), F32), jnp.zeros((tq, 2 * v_dim), F32))
                 for _ in range(2))
    per_tile = tq // tk
    carry = lax.fori_loop(0, qi * per_tile, lambda ki, c: attend(c, pl.multiple_of(ki * tk, tk), None), init)
    row = lax.broadcasted_iota(I32, (tq, tk), 0)
    col = lax.broadcasted_iota(I32, (tq, tk), 1)
    for j in range(per_tile):
        carry = attend(carry, pl.multiple_of(qi * tq + j * tk, tk), col + j * tk <= row)
    (_, l0, acc0), (_, l1, acc1) = carry
    lane = lax.broadcasted_iota(I32, (tq, 2 * v_dim), 1)
    o_ref[...] = jnp.where(lane < v_dim, acc0 / l0, acc1 / l1).astype(o_ref.dtype)


def _mla_prompt(q, k, v, batch):
    m, hw = q.shape
    heads = hw // LANES
    v_dim = v.shape[1] // heads
    assert heads % 2 == 0 and 2 * v_dim == LANES
    s = m // batch
    tq = _tile(s, ATTN_Q_TILE)
    tk = _tile(tq, ATTN_K_TILE)
    nq = s // tq
    return pl.pallas_call(
        functools.partial(_mla_prompt_kernel, v_dim=v_dim, tk=tk),
        grid=(batch, heads // 2, nq),
        in_specs=[pl.BlockSpec((tq, 2 * LANES), lambda b, h, qi: (b * nq + qi, h)),
                  pl.BlockSpec((s, 2 * LANES), lambda b, h, qi: (b, h)),
                  pl.BlockSpec((s, 2 * v_dim), lambda b, h, qi: (b, h))],
        out_specs=pl.BlockSpec((tq, 2 * v_dim), lambda b, h, qi: (b * nq + qi, h)),
        out_shape=jax.ShapeDtypeStruct((m, heads * v_dim), BF16),
        compiler_params=_cparams(("parallel", "parallel", "arbitrary")),
        name="mla_prompt",
    )(q, k, v)


def _q_sample_kernel(x_ref, wc_ref, wm_ref, g_ref, wn_ref, wr_ref, wrs_ref, wuk_ref, cos_ref, sin_ref,
                     ql_ref, qr_ref, qm_ref):
    x = x_ref[...].astype(BF16)
    qm_ref[...] = jnp.dot(x, wm_ref[...], preferred_element_type=F32)
    cq = _rms_norm(jnp.dot(x, wc_ref[...], preferred_element_type=F32), g_ref[...]).astype(BF16)
    cos = cos_ref[...]
    sin = sin_ref[...]
    for h in range(wn_ref.shape[0]):
        qn = jnp.dot(cq, wn_ref[h], preferred_element_type=F32).astype(BF16)
        ql_ref[h] = jnp.dot(qn, wuk_ref[h], preferred_element_type=F32).astype(BF16)
        qr = jnp.dot(cq, wr_ref[h], preferred_element_type=F32) * cos
        qr_ref[h] = (qr + jnp.dot(cq, wrs_ref[h], preferred_element_type=F32) * sin).astype(BF16)


def _q_sample(x, wc, wm, g, wn, wr, wrs, wuk_t, cos, sin):
    m, d = x.shape
    heads = wn.shape[0]
    rk = wuk_t.shape[2]
    e = wr.shape[2]
    full = lambda arr: pl.BlockSpec(arr.shape, lambda i: (0,) * arr.ndim)
    return pl.pallas_call(
        _q_sample_kernel,
        grid=(1,),
        in_specs=[full(a) for a in (x, wc, wm, g, wn, wr, wrs, wuk_t, cos, sin)],
        out_specs=[pl.BlockSpec((heads, m, rk), lambda i: (0, 0, 0)),
                   pl.BlockSpec((heads, m, e), lambda i: (0, 0, 0)),
                   pl.BlockSpec((m, wm.shape[1]), lambda i: (0, 0))],
        out_shape=[jax.ShapeDtypeStruct((heads, m, rk), BF16), jax.ShapeDtypeStruct((heads, m, e), BF16),
                   jax.ShapeDtypeStruct((m, wm.shape[1]), F32)],
        compiler_params=_cparams(("arbitrary",)),
        name="q_sample",
    )(x, wc, wm, g, wn, wr, wrs, wuk_t, cos, sin)


def _mla_sample_kernel(pt_ref, ql_ref, qr_ref, cn_ref, kn_ref, cache_c, cache_kr, o_ref,
                       kv_buf, kr_buf, s_ref, sem, *, n_pages, new_len, scale, chunk):
    b = pl.program_id(0)
    cur = b % 2
    page = cache_c.shape[1]
    past = kv_buf.shape[1]

    def for_each_page(req, buf, act):
        def body(p, c):
            src = pt_ref[req * n_pages + p]
            at = pl.multiple_of(p * page, page)
            act(pltpu.make_async_copy(cache_c.at[src], kv_buf.at[buf, pl.ds(at, page)], sem.at[buf]))
            act(pltpu.make_async_copy(cache_kr.at[src], kr_buf.at[buf, :, pl.ds(at, page)], sem.at[buf]))
            return c

        lax.fori_loop(0, n_pages, body, 0)

    @pl.when(b == 0)
    def _():
        for_each_page(b, cur, lambda c: c.start())

    @pl.when(b + 1 < pl.num_programs(0))
    def _():
        for_each_page(b + 1, 1 - cur, lambda c: c.start())

    ql = ql_ref[0]
    qr = qr_ref[0]
    nq, nk = ql.shape[0], cn_ref.shape[1]
    cn = cn_ref[0].astype(BF16)
    s_of_row = lax.broadcasted_iota(I32, (nq, nk), 0) % new_len
    j = lax.broadcasted_iota(I32, (nq, nk), 1)
    s_new = lax.dot_general(ql, cn, _NT, preferred_element_type=F32)
    s_new = (s_new + lax.dot_general(qr, kn_ref[0].astype(BF16), _NT, preferred_element_type=F32)) * scale
    s_new = jnp.where(j <= s_of_row, s_new, MASK_VALUE)

    for_each_page(b, cur, lambda c: c.wait())
    for c in range(past // chunk):
        keys = slice(c * chunk, (c + 1) * chunk)
        s = lax.dot_general(ql, kv_buf[cur, keys, :].astype(BF16), _NT, preferred_element_type=F32)
        s = s + jnp.dot(qr, kr_buf[cur, :, keys].astype(BF16), preferred_element_type=F32)
        s_ref[:, keys] = s * scale
    s = s_ref[...]
    m = jnp.maximum(jnp.max(s, axis=-1, keepdims=True), jnp.max(s_new, axis=-1, keepdims=True))
    p = jnp.exp(s - m)
    p_new = jnp.exp(s_new - m)
    l = jnp.sum(p, axis=-1, keepdims=True) + jnp.sum(p_new, axis=-1, keepdims=True)
    acc = jnp.dot(p_new.astype(BF16), cn, preferred_element_type=F32)
    p = p.astype(BF16)
    for c in range(past // chunk):
        keys = slice(c * chunk, (c + 1) * chunk)
        acc = acc + jnp.dot(p[:, keys], kv_buf[cur, keys, :].astype(BF16), preferred_element_type=F32)
    o_ref[0] = acc / l


def _mla_sample(page_table, ql, qr, cache_c, cache_kr_t, c_new, kr_new, new_len, scale):
    bs, n_pages = page_table.shape
    _, nq, rk = ql.shape
    e = qr.shape[2]
    page = cache_c.shape[1]
    past = n_pages * page
    chunk = math.gcd(past, KEY_CHUNK)
    per_b = lambda n, w: pl.BlockSpec((1, n, w), lambda b, pt: (b, 0, 0))
    s8 = c_new.shape[1]
    return pl.pallas_call(
        functools.partial(_mla_sample_kernel, n_pages=n_pages, new_len=new_len, scale=scale, chunk=chunk),
        grid_spec=pltpu.PrefetchScalarGridSpec(
            num_scalar_prefetch=1,
            grid=(bs,),
            in_specs=[per_b(nq, rk), per_b(nq, e), per_b(s8, rk), per_b(s8, e),
                      pl.BlockSpec(memory_space=pl.ANY), pl.BlockSpec(memory_space=pl.ANY)],
            out_specs=per_b(nq, rk),
            scratch_shapes=[pltpu.VMEM((2, past, rk), F32), pltpu.VMEM((2, e, past), F32),
                            pltpu.VMEM((nq, past), F32), pltpu.SemaphoreType.DMA((2,))]),
        out_shape=jax.ShapeDtypeStruct((bs, nq, rk), F32),
        compiler_params=_cparams(("arbitrary",)),
        name="mla_sample",
    )(page_table.reshape(-1), ql, qr, c_new, kr_new, cache_c, cache_kr_t)


def _head_up_kernel(x_ref, w_ref, o_ref):
    o_ref[0] = jnp.dot(x_ref[0].astype(BF16), w_ref[0], preferred_element_type=F32)


def _head_up(lat, wv):
    h, m, rk = lat.shape
    v = wv.shape[2]
    return pl.pallas_call(
        _head_up_kernel,
        grid=(h,),
        in_specs=[pl.BlockSpec((1, m, rk), lambda i: (i, 0, 0)), pl.BlockSpec((1, rk, v), lambda i: (i, 0, 0))],
        out_specs=pl.BlockSpec((1, m, v), lambda i: (i, 0, 0)),
        out_shape=jax.ShapeDtypeStruct((h, m, v), F32),
        compiler_params=_cparams(("parallel",)),
        name="head_up",
    )(lat, wv)


def _head_slots(w, heads, width):
    k = w.shape[0]
    w = w.reshape(k, heads, width)
    return jnp.pad(w, ((0, 0), (0, 0), (0, LANES - width))).reshape(k, heads * LANES)


def kernel(x_prompt, x_sample, cache_kv_latent, cache_k_rope, cache_mem_k, cache_mem_v, state_conv, page_table, mem_prompt, w_in_a, conv_w, w_out_a, w_in_b, g_q, w_uq, w_out_b, w_kv_down, g_kv, w_uk, w_uv, w_mem_k, w_mem_v, ln1_g, ln1_b, ln2_g, ln2_b, w_router, b_router, w_gate, b_gate, w_up, b_up, w_down, b_down):
    bp, sp, d = x_prompt.shape
    bs, ss, _ = x_sample.shape
    depth = ln1_g.shape[0]
    n_a = w_in_a.shape[0]
    d_conv = conv_w.shape[2]
    assert conv_w.shape[1] == 3
    n_mem = mem_prompt.shape[1]
    mem_heads = cache_mem_k.shape[3]
    mem_w = mem_heads * cache_mem_k.shape[4]
    kv_rank, heads, nope = w_uk.shape
    v_dim = w_uv.shape[2]
    rope_dim = cache_k_rope.shape[2]
    q_rank = g_q.shape[1]
    n_past = page_table.shape[1] * cache_kv_latent.shape[1]
    alpha = (2 * depth) ** 0.25
    scale = (nope + rope_dim) ** -0.5
    half = rope_dim // 2

    xp = x_prompt.reshape(bp * sp, d)
    xs = x_sample.reshape(bs * ss, d)

    w_mem = jnp.concatenate([w_mem_k[l] for l in range(depth)] + [w_mem_v[l] for l in range(depth)],
                            axis=1).astype(BF16)
    mem_k_flat, mem_v_flat = _mem_kv(mem_prompt.reshape(bp * n_mem, d), w_mem, depth)
    mem_k_p = mem_k_flat.reshape(depth, bp, n_mem, mem_w)
    mem_v_p = mem_v_flat.reshape(depth, bp, n_mem, mem_w)
    mem_k_s = cache_mem_k.reshape(depth, bs, n_mem, mem_w)
    mem_v_s = cache_mem_v.reshape(depth, bs, n_mem, mem_w)

    conv_p, conv_s = [], []
    c_p = kr_p = c_s = kr_s = None
    k_p = v_p = None
    tables = None

    for l in range(depth):
        g1, b1 = ln1_g[l].reshape(1, d), ln1_b[l].reshape(1, d)
        ln_route = functools.partial(_out_ln_route, g=g1, b=b1, wt=w_router[l].T, br=b_router[l].reshape(-1, 1),
                                     alpha=alpha)
        if l < n_a:
            w_in = w_in_a[l].astype(BF16)
            w_out = w_out_a[l].astype(BF16)
            splits = [d_conv, d_conv, d_conv, mem_w]
            y, qm, st = _proj_conv(xp, w_in, jnp.zeros((bp, 2, d_conv), F32), conv_w[l], bp)
            conv_p.append(st)
            ym = _mem_attn(qm.reshape(bp, sp, mem_w), mem_k_p, mem_v_p, l, mem_heads).reshape(bp * sp, mem_w)
            xp, *route_p = ln_route(y, ym, xp, w_out[:d_conv], w_out[d_conv:])

            gb, gc, hv, qm = _proj(xs, w_in, splits, "in_proj_a")
            r3 = lambda a: a.reshape(bs, ss, d_conv)
            y, st = _conv_short(r3(gb), r3(gc), r3(hv), state_conv[l], conv_w[l])
            conv_s.append(st)
            ym = _mem_attn(qm.reshape(bs, ss, mem_w), mem_k_s, mem_v_s, l, mem_heads).reshape(bs * ss, mem_w)
            xs, *route_s = ln_route(y.reshape(bs * ss, d_conv), ym, xs, w_out[:d_conv], w_out[d_conv:])
        else:
            j = l - n_a
            if tables is None:
                cos_p, sin_p = _rope_table(0, sp, rope_dim)
                cos_s, sin_s = _rope_table(n_past, ss, rope_dim)
                cos_s, sin_s = jnp.tile(cos_s, (bs, 1)), jnp.tile(sin_s, (bs, 1))
                pad = LANES - nope - rope_dim
                c2 = scale * math.log2(math.e)
                cos_slot = c2 * jnp.concatenate([jnp.ones((sp, nope), F32), cos_p, jnp.zeros((sp, pad), F32)], axis=1)
                sin_slot = c2 * jnp.concatenate([jnp.zeros((sp, nope), F32), sin_p, jnp.zeros((sp, pad), F32)], axis=1)
                tables = True
                wc = w_kv_down[:, :kv_rank].astype(BF16)
                wr = w_kv_down[:, kv_rank:]
                wrs = _swap_halves_signed(wr).astype(BF16)
                wr = wr.astype(BF16)
                gk = g_kv.reshape(1, kv_rank)
                c_p, kr_p = _shared_kv(xp, wc, wr, wrs, gk, cos_p, sin_p)
                c_s, kr_s = _shared_kv(xs, wc, wr, wrs, gk, cos_s, sin_s)
                wk_pad = _head_slots(w_uk.reshape(kv_rank, heads * nope), heads, nope).astype(BF16)
                rep = jnp.pad(jnp.eye(rope_dim, dtype=F32), ((0, 0), (nope, pad)))
                rep = jnp.tile(rep, (1, heads)).astype(BF16)
                k_p, v_p = _kv_up(c_p, kr_p, wk_pad, rep, w_uv.reshape(kv_rank, heads * v_dim).astype(BF16))
                pad_rows = -ss % SUBLANES
                c_new = jnp.pad(c_s.reshape(bs, ss, kv_rank), ((0, 0), (0, pad_rows), (0, 0)))
                kr_new = jnp.pad(kr_s.reshape(bs, ss, rope_dim), ((0, 0), (0, pad_rows), (0, 0)))
                cache_kr_t = jnp.transpose(cache_k_rope, (0, 2, 1))
                wuk_t = jnp.transpose(w_uk, (1, 2, 0)).astype(BF16)
                wuv_h = jnp.transpose(w_uv, (1, 0, 2)).astype(BF16)

            w_in = w_in_b[j]
            wcq = w_in[:, :q_rank].astype(BF16)
            wqm = w_in[:, q_rank:].astype(BF16)
            gq = g_q[j].reshape(1, q_rank)
            wq = w_uq[j].reshape(q_rank, heads, nope + rope_dim)
            wq_nope = wq[:, :, :nope]
            wq_rope = wq[:, :, nope:]
            wq_rope_sw = _swap_halves_signed(wq_rope)
            w_out = w_out_b[j].astype(BF16)
            hv_w = heads * v_dim

            slot = lambda a, b: _head_slots(jnp.concatenate([a, b], axis=2).reshape(q_rank, -1), heads,
                                            nope + rope_dim).astype(BF16)
            wq_pad = slot(wq_nope, wq_rope)
            wqs_pad = slot(jnp.zeros_like(wq_nope), wq_rope_sw)
            q, qm = _q_prompt(xp, wcq, wqm, gq, wq_pad, wqs_pad, cos_slot, sin_slot)
            att = _mla_prompt(q, k_p, v_p, bp)
            ym = _mem_attn(qm.reshape(bp, sp, mem_w), mem_k_p, mem_v_p, l, mem_heads).reshape(bp * sp, mem_w)
            xp, *route_p = ln_route(att, ym, xp, w_out[:hv_w], w_out[hv_w:])

            th = lambda a: jnp.transpose(a, (1, 0, 2)).astype(BF16)
            ql, qr, qm = _q_sample(xs, wcq, wqm, gq, th(wq_nope), th(wq_rope), th(wq_rope_sw), wuk_t,
                                   cos_s, sin_s)
            per_req = lambda a: jnp.transpose(a.reshape(heads, bs, ss, -1), (1, 0, 2, 3)).reshape(bs, heads * ss, -1)
            lat = _mla_sample(page_table, per_req(ql), per_req(qr), cache_kv_latent, cache_kr_t,
                              c_new, kr_new, ss, scale)
            lat = jnp.transpose(lat.reshape(bs, heads, ss, kv_rank), (1, 0, 2, 3)).reshape(heads, bs * ss, kv_rank)
            att = jnp.transpose(_head_up(lat, wuv_h), (1, 0, 2)).reshape(bs * ss, hv_w)
            ym = _mem_attn(qm.reshape(bs, ss, mem_w), mem_k_s, mem_v_s, l, mem_heads).reshape(bs * ss, mem_w)
            xs, *route_s = ln_route(att, ym, xs, w_out[:hv_w], w_out[hv_w:])

        xp, xs = _moe_ln([xp, xs], [route_p, route_s], l, w_gate, b_gate, w_up, b_up, w_down, b_down,
                         ln2_g[l].reshape(1, d), ln2_b[l].reshape(1, d), alpha)

    ne_shape = (depth, bp, n_mem, mem_heads, mem_w // mem_heads)
    return (xp.reshape(bp, sp, d), xs.reshape(bs, ss, d),
            c_p.reshape(bp, sp, kv_rank), kr_p.reshape(bp, sp, rope_dim),
            c_s.reshape(bs, ss, kv_rank), kr_s.reshape(bs, ss, rope_dim),
            mem_k_p.reshape(ne_shape), mem_v_p.reshape(ne_shape),
            jnp.stack(conv_p), jnp.stack(conv_s))
```

```python
import functools
import math

import jax
import jax.numpy as jnp
from jax import lax
from jax.experimental import pallas as pl
from jax.experimental.pallas import tpu as pltpu

F32 = jnp.float32
BF16 = jnp.bfloat16
I32 = jnp.int32
U32 = jnp.uint32

TOP_K = 4
SWIGLU_LIMIT = 7.0
SWIGLU_ALPHA = 1.702
ROPE_THETA = 10000.0
LN_EPS = 1e-5
RMS_EPS = 1e-6
MASK_VALUE = -1e30

LANES = 128
SUBLANES = 8
VMEM_LIMIT = 56 * 1024 * 1024

ROW_TILE = 512
MOE_TILE = 512
EXPERT_ROWS = 512
KEY_CHUNK = 2048
MEM_ATTN_ROWS = 32
ATTN_Q_TILE = 512
ATTN_K_TILE = 512
NUM_EXPERTS = 32

_NT = (((1,), (1,)), ((), ()))


def _cparams(sem):
    return pltpu.CompilerParams(dimension_semantics=sem, vmem_limit_bytes=VMEM_LIMIT)


def _tile(n, pref):
    t = min(n, pref)
    assert n % t == 0, (n, pref)
    return t


def _layer_norm(z, g, b):
    mu = jnp.mean(z, axis=-1, keepdims=True)
    zc = z - mu
    var = jnp.mean(zc * zc, axis=-1, keepdims=True)
    return zc * lax.rsqrt(var + LN_EPS) * g + b


def _rms_norm(x, g):
    return x * lax.rsqrt(jnp.mean(x * x, axis=-1, keepdims=True) + RMS_EPS) * g


def _proj_kernel(x_ref, w_ref, *o_refs):
    h = jnp.dot(x_ref[...].astype(BF16), w_ref[...], preferred_element_type=F32)
    off = 0
    for o_ref in o_refs:
        n = o_ref.shape[-1]
        o_ref[...] = h[:, off:off + n].astype(o_ref.dtype)
        off += n


def _proj(x, w, splits, name):
    m, k = x.shape
    n = w.shape[1]
    assert sum(splits) == n
    tm = _tile(m, ROW_TILE)
    return pl.pallas_call(
        _proj_kernel,
        grid=(m // tm,),
        in_specs=[pl.BlockSpec((tm, k), lambda i: (i, 0)), pl.BlockSpec((k, n), lambda i: (0, 0))],
        out_specs=[pl.BlockSpec((tm, s), lambda i: (i, 0)) for s in splits],
        out_shape=[jax.ShapeDtypeStruct((m, s), F32) for s in splits],
        compiler_params=_cparams(("parallel",)),
        name=name,
    )(x, w)


def _memkv_kernel(x_ref, w_ref, k_ref, v_ref):
    depth, _, f = k_ref.shape
    h = jnp.dot(x_ref[...].astype(BF16), w_ref[...], preferred_element_type=F32)
    for l in range(depth):
        k_ref[l] = h[:, l * f:(l + 1) * f]
        v_ref[l] = h[:, (depth + l) * f:(depth + l + 1) * f]


def _mem_kv(mem, w_cat, depth):
    r, d = mem.shape
    f = w_cat.shape[1] // (2 * depth)
    tm = _tile(r, ROW_TILE)
    shp = jax.ShapeDtypeStruct((depth, r, f), F32)
    spec = pl.BlockSpec((depth, tm, f), lambda i: (0, i, 0))
    return pl.pallas_call(
        _memkv_kernel,
        grid=(r // tm,),
        in_specs=[pl.BlockSpec((tm, d), lambda i: (i, 0)), pl.BlockSpec(w_cat.shape, lambda i: (0, 0))],
        out_specs=[spec, spec],
        out_shape=[shp, shp],
        compiler_params=_cparams(("parallel",)),
        name="mem_kv",
    )(mem, w_cat)


def _proj_conv_kernel(x_ref, w_ref, pre_ref, cw_ref, y_ref, qm_ref, st_ref, carry):
    j = pl.program_id(1)

    @pl.when(j == 0)
    def _():
        carry[0:2, :] = pre_ref[0]

    c = y_ref.shape[1]
    h = jnp.dot(x_ref[...].astype(BF16), w_ref[...], preferred_element_type=F32)
    qm_ref[...] = h[:, 3 * c:].astype(qm_ref.dtype)
    u = h[:, c:2 * c] * h[:, 2 * c:3 * c]
    ts = u.shape[0]
    c0 = carry[0:1, :]
    c1 = carry[1:2, :]
    row = lax.broadcasted_iota(I32, u.shape, 0)
    u1 = jnp.where(row == 0, c1, pltpu.roll(u, 1, 0))
    u2 = jnp.where(row == 0, c0, jnp.where(row == 1, c1, pltpu.roll(u, 2, 0)))
    w = cw_ref[...]
    y_ref[...] = (h[:, :c] * (w[0:1] * u2 + w[1:2] * u1 + w[2:3] * u)).astype(y_ref.dtype)
    last = u[ts - 2:ts, :]
    carry[0:2, :] = last
    st_ref[0] = last


def _proj_conv(x, w_in, prefix, cw, batch):
    m, d = x.shape
    c = cw.shape[1]
    n = w_in.shape[1]
    s = m // batch
    ts = _tile(s, ROW_TILE)
    nt = s // ts
    row = lambda width: pl.BlockSpec((ts, width), lambda b, j: (b * nt + j, 0))
    state = pl.BlockSpec((1, 2, c), lambda b, j: (b, 0, 0))
    return pl.pallas_call(
        _proj_conv_kernel,
        grid=(batch, nt),
        in_specs=[row(d), pl.BlockSpec((d, n), lambda b, j: (0, 0)), state, pl.BlockSpec((3, c), lambda b, j: (0, 0))],
        out_specs=[row(c), row(n - 3 * c), state],
        out_shape=[jax.ShapeDtypeStruct((m, c), BF16), jax.ShapeDtypeStruct((m, n - 3 * c), BF16),
                   jax.ShapeDtypeStruct((batch, 2, c), F32)],
        scratch_shapes=[pltpu.VMEM((SUBLANES, c), F32)],
        compiler_params=_cparams(("parallel", "arbitrary")),
        name="proj_conv",
    )(x, w_in, prefix, cw)


def _conv_short_kernel(gb_ref, gc_ref, hv_ref, pre_ref, w_ref, y_ref, st_ref):
    s_len = gb_ref.shape[1]
    w = w_ref[...]

    def u(k):
        if k < 0:
            return pre_ref[:, 2 + k, :]
        return gc_ref[:, k, :] * hv_ref[:, k, :]

    for s in range(s_len):
        y_ref[:, s, :] = gb_ref[:, s, :] * (w[0:1] * u(s - 2) + w[1:2] * u(s - 1) + w[2:3] * u(s))
    for k in range(2):
        st_ref[:, k, :] = u(s_len - 2 + k)


def _conv_short(gb, gc, hv, prefix, w):
    batch, s, c = gb.shape
    full = lambda shp: pl.BlockSpec(shp, lambda i: (0,) * len(shp))
    return pl.pallas_call(
        _conv_short_kernel,
        grid=(1,),
        in_specs=[full(gb.shape)] * 3 + [full(prefix.shape), full(w.shape)],
        out_specs=[full(gb.shape), full(prefix.shape)],
        out_shape=[jax.ShapeDtypeStruct(gb.shape, F32), jax.ShapeDtypeStruct(prefix.shape, F32)],
        compiler_params=_cparams(("arbitrary",)),
        name="conv_short",
    )(gb, gc, hv, prefix, w)


def _mem_attn_kernel(q_ref, k_ref, v_ref, o_ref, *, heads):
    _, tq, w = q_ref.shape
    hd = w // heads
    lane = lax.broadcasted_iota(I32, (tq, w), 1)
    own = [(lane >= h * hd) & (lane < (h + 1) * hd) for h in range(heads)]
    for i in range(q_ref.shape[0]):
        q = q_ref[i].astype(F32)
        q_heads = jnp.concatenate([jnp.where(m, q, 0.0) for m in own], axis=0).astype(BF16)
        k = k_ref[0, i].astype(BF16)
        v = v_ref[0, i].astype(BF16)
        s = lax.dot_general(q_heads, k, _NT, preferred_element_type=F32) * (hd ** -0.5)
        m = jnp.max(s, axis=-1, keepdims=True)
        p = jnp.exp(s - m)
        l = jnp.sum(p, axis=-1, keepdims=True)
        pv = jnp.dot(p.astype(BF16), v, preferred_element_type=F32) / l
        out = jnp.zeros((tq, w), F32)
        for h in range(heads):
            out = jnp.where(own[h], pv[h * tq:(h + 1) * tq], out)
        o_ref[i] = out.astype(o_ref.dtype)


def _mem_attn(q, mk, mv, layer, heads):
    b, s, w = q.shape
    nm = mk.shape[2]
    tq = _tile(s, ROW_TILE)
    bb = math.gcd(b, max(1, MEM_ATTN_ROWS // tq))
    return pl.pallas_call(
        functools.partial(_mem_attn_kernel, heads=heads),
        grid=(b // bb, s // tq),
        in_specs=[pl.BlockSpec((bb, tq, w), lambda i, j: (i, j, 0)),
                  pl.BlockSpec((1, bb, nm, w), lambda i, j: (layer, i, 0, 0)),
                  pl.BlockSpec((1, bb, nm, w), lambda i, j: (layer, i, 0, 0))],
        out_specs=pl.BlockSpec((bb, tq, w), lambda i, j: (i, j, 0)),
        out_shape=jax.ShapeDtypeStruct((b, s, w), BF16),
        compiler_params=_cparams(("parallel", "parallel")),
        name="mem_attn",
    )(q, mk, mv)


def _out_ln_route_kernel(a_ref, m_ref, x_ref, wa_ref, wm_ref, g_ref, b_ref, wt_ref, br_ref,
                         o_ref, pos_ref, gate_ref, cnt_ref, *, alpha):
    o = jnp.dot(a_ref[...].astype(BF16), wa_ref[...], preferred_element_type=F32)
    o = o + jnp.dot(m_ref[...].astype(BF16), wm_ref[...], preferred_element_type=F32)
    x = _layer_norm(alpha * x_ref[...] + o, g_ref[...], b_ref[...])
    o_ref[...] = x
    _route(x, wt_ref[...], br_ref[...], pos_ref, gate_ref, cnt_ref)


def _out_ln_route(a, mem, x, wa, wm, g, b, wt, br, alpha):
    m, d = x.shape
    ne = br.shape[0]
    tt = _tile(m, MOE_TILE)
    nt = m // tt
    row = lambda n: pl.BlockSpec((tt, n), lambda i: (i, 0))
    full = lambda arr: pl.BlockSpec(arr.shape, lambda i: (0, 0))
    return pl.pallas_call(
        functools.partial(_out_ln_route_kernel, alpha=alpha),
        grid=(nt,),
        in_specs=[row(a.shape[1]), row(mem.shape[1]), row(d), full(wa), full(wm), full(g), full(b), full(wt), full(br)],
        out_specs=[row(d),
                   pl.BlockSpec((TOP_K, tt), lambda i: (0, i)),
                   pl.BlockSpec((TOP_K, tt), lambda i: (0, i)),
                   pl.BlockSpec((1, ne, 1), lambda i: (i, 0, 0))],
        out_shape=[jax.ShapeDtypeStruct((m, d), F32),
                   jax.ShapeDtypeStruct((TOP_K, m), I32),
                   jax.ShapeDtypeStruct((TOP_K, m), F32),
                   jax.ShapeDtypeStruct((nt, ne, 1), I32)],
        compiler_params=_cparams(("parallel",)),
        name="out_ln_route",
    )(a, mem, x, wa, wm, g, b, wt, br)


def _route(x, wt2, bias, pos_ref, gate_ref, cnt_ref):
    ne = wt2.shape[0] // 2
    x_hi = x.astype(BF16)
    x_lo = (x - x_hi.astype(F32)).astype(BF16)
    by_hi = lax.dot_general(wt2, x_hi, _NT, preferred_element_type=F32)
    logits = (by_hi[:ne] + by_hi[ne:]) + lax.dot_general(wt2[:ne], x_lo, _NT, preferred_element_type=F32) + bias
    tt = logits.shape[1]
    eidx = lax.broadcasted_iota(I32, (ne, tt), 0)
    sels, vals = [], []
    l = logits
    for _ in range(TOP_K):
        m = jnp.max(l, axis=0, keepdims=True)
        first = jnp.min(jnp.where(l == m, eidx, ne), axis=0, keepdims=True)
        sel = eidx == first
        sels.append(sel)
        vals.append(m)
        l = jnp.where(sel, -jnp.inf, l)
    ex = [jnp.exp(v - vals[0]) for v in vals]
    den = ex[0]
    for e in ex[1:]:
        den = den + e
    gate_ref[...] = jnp.concatenate([e / den for e in ex], axis=0)

    chosen = jnp.zeros((ne, tt), F32)
    for sel in sels:
        chosen = jnp.where(sel, 1.0, chosen)
    cnt = jnp.sum(chosen, axis=1, keepdims=True)
    cnt_ref[0] = cnt.astype(I32)
    cnt8 = jnp.floor((cnt + (SUBLANES - 1)) * (1.0 / SUBLANES)) * SUBLANES
    er = lax.broadcasted_iota(I32, (ne, ne), 0)
    ec = lax.broadcasted_iota(I32, (ne, ne), 1)
    below = jnp.where(ec < er, 1.0, 0.0)
    seg_start = jnp.dot(below, jnp.broadcast_to(cnt8, (ne, LANES)), precision=lax.Precision.HIGHEST,
                        preferred_element_type=F32)[:, 0:1]
    tr = lax.broadcasted_iota(I32, (tt, tt), 0)
    tc = lax.broadcasted_iota(I32, (tt, tt), 1)
    earlier = jnp.where(tr < tc, 1.0, 0.0).astype(BF16)
    rank = jnp.dot(chosen.astype(BF16), earlier, preferred_element_type=F32)
    slot = seg_start + rank
    pos_ref[...] = jnp.concatenate(
        [jnp.sum(jnp.where(sel, slot, 0.0), axis=0, keepdims=True) for sel in sels], axis=0).astype(I32)


def _sorted_rows(tt):
    return TOP_K * tt + NUM_EXPERTS * SUBLANES


def _for_each_segment(n_ref, local_ref, global_ref, tile, ne, fn):
    def body(e, c):
        k = tile * ne + e
        n = pl.multiple_of(n_ref[k], SUBLANES)

        @pl.when(n > 0)
        def _():
            fn(n, pl.multiple_of(local_ref[k], SUBLANES), pl.multiple_of(global_ref[k], SUBLANES))
        return c

    lax.fori_loop(0, ne, body, 0)


def _dispatch_kernel(n_ref, local_ref, global_ref, fill_n_ref, fill_at_ref, *refs, ne, group_tiles):
    xs_ref, buf, zeros, sem = refs[-4:]
    step = pl.program_id(0)
    last = pl.num_programs(0) - 1
    tile = step
    cur = step % 2

    def sort_tile(x_ref, slot_ref):
        x = x_ref[...].astype(BF16)
        slot = slot_ref[...]
        rows = _sorted_rows(x.shape[0])
        r = lax.broadcasted_iota(I32, (rows, x.shape[0]), 0)
        onehot = jnp.zeros(r.shape, F32)
        for k in range(TOP_K):
            onehot = jnp.where(slot[k:k + 1, :] == r, 1.0, onehot)
        buf[cur, :rows] = jnp.dot(onehot.astype(BF16), x, preferred_element_type=F32)

    first = 0
    for g, nt in enumerate(group_tiles):
        pl.when((step >= first) & (step < first + nt))(
            functools.partial(sort_tile, refs[2 * g], refs[2 * g + 1]))
        first += nt

    def copy(b):
        return lambda n, lo, go: pltpu.make_async_copy(buf.at[b, pl.ds(lo, n)], xs_ref.at[pl.ds(go, n)], sem.at[b])

    _for_each_segment(n_ref, local_ref, global_ref, tile, ne, lambda *a: copy(cur)(*a).start())

    def fill_copies(act):
        def body(k, c):
            n = pl.multiple_of(fill_n_ref[k], SUBLANES)
            at = pl.multiple_of(fill_at_ref[k], SUBLANES)

            @pl.when(n > 0)
            def _():
                act(pltpu.make_async_copy(zeros.at[pl.ds(0, n)], xs_ref.at[pl.ds(at, n)], sem.at[2]))
            return c

        lax.fori_loop(0, fill_n_ref.shape[0], body, 0)

    @pl.when(step == 0)
    def _():
        zeros[...] = jnp.zeros(zeros.shape, F32)
        fill_copies(lambda c: c.start())

    @pl.when(step > 0)
    def _():
        _for_each_segment(n_ref, local_ref, global_ref, tile - 1, ne, lambda *a: copy(1 - cur)(*a).wait())

    @pl.when(step == last)
    def _():
        _for_each_segment(n_ref, local_ref, global_ref, tile, ne, lambda *a: copy(cur)(*a).wait())
        fill_copies(lambda c: c.wait())


def _dispatch(seg, fills, groups, tiles, slots, rows_total):
    d = groups[0].shape[1]
    group_tiles = [x.shape[0] // tt for x, tt in zip(groups, tiles)]
    in_specs, args, first = [], [], 0
    for x, tt, slot, nt in zip(groups, tiles, slots, group_tiles):
        own = lambda i, first=first, nt=nt: jnp.clip(i - first, 0, nt - 1)
        in_specs += [pl.BlockSpec((tt, d), lambda i, *_, own=own: (own(i), 0)),
                     pl.BlockSpec((TOP_K, tt), lambda i, *_, own=own: (0, own(i)))]
        args += [x, slot]
        first += nt
    return pl.pallas_call(
        functools.partial(_dispatch_kernel, ne=NUM_EXPERTS, group_tiles=tuple(group_tiles)),
        grid_spec=pltpu.PrefetchScalarGridSpec(
            num_scalar_prefetch=5,
            grid=(sum(group_tiles),),
            in_specs=in_specs,
            out_specs=pl.BlockSpec(memory_space=pl.ANY),
            scratch_shapes=[pltpu.VMEM((2, _sorted_rows(max(tiles)), d), F32), pltpu.VMEM((EXPERT_ROWS, d), F32),
                            pltpu.SemaphoreType.DMA((3,))]),
        out_shape=jax.ShapeDtypeStruct((rows_total, d), F32),
        compiler_params=_cparams(("arbitrary",)),
        name="moe_dispatch",
    )(*seg, *fills, *args)


def _expert_kernel(be_ref, nu_ref, nxt_ref, xs_ref, bg_ref, bu_ref, bd_ref, wg_hbm, wu_hbm, wd_hbm, ys_ref,
                   wg_f32, wu_f32, wd_f32, wg_bf, wu_bf, wd_bf, sem, *, first_weight):
    half = xs_ref.shape[0] // 2
    h0 = 2 * pl.program_id(0)
    h1 = h0 + 1
    n_halves = nu_ref[0]
    used0 = h0 < n_halves
    used1 = h1 < n_halves
    e0 = be_ref[h0]
    e1 = be_ref[h1]
    first0 = used0 & ((h0 == 0) | (e0 != be_ref[jnp.maximum(h0 - 1, 0)]))
    first1 = used1 & (e1 != e0)

    def weight_copies(e, act):
        w = first_weight + e
        act(pltpu.make_async_copy(wg_hbm.at[w], wg_f32, sem.at[0]))
        act(pltpu.make_async_copy(wu_hbm.at[w], wu_f32, sem.at[1]))
        act(pltpu.make_async_copy(wd_hbm.at[w], wd_f32, sem.at[2]))

    def stage(h):
        @pl.when(h == 0)
        def _():
            weight_copies(be_ref[h], lambda c: c.start())

        weight_copies(be_ref[h], lambda c: c.wait())
        wg_bf[...] = wg_f32[...].astype(BF16)
        wu_bf[...] = wu_f32[...].astype(BF16)
        wd_bf[...] = wd_f32[...].astype(BF16)

        @pl.when(nxt_ref[h] >= 0)
        def _():
            weight_copies(nxt_ref[h], lambda c: c.start())

    def run(rows, e):
        w = pl.ds(first_weight + e, 1)
        x = xs_ref[rows, :].astype(BF16)
        g = jnp.dot(x, wg_bf[...], preferred_element_type=F32) + bg_ref[w, :]
        u = jnp.dot(x, wu_bf[...], preferred_element_type=F32) + bu_ref[w, :]
        g = jnp.minimum(g, SWIGLU_LIMIT)
        u = jnp.clip(u, -SWIGLU_LIMIT, SWIGLU_LIMIT)
        h = (u + 1.0) * (g * jax.nn.sigmoid(SWIGLU_ALPHA * g))
        ys_ref[rows, :] = jnp.dot(h.astype(BF16), wd_bf[...], preferred_element_type=F32) + bd_ref[w, :]

    def clear(rows):
        ys_ref[rows, :] = jnp.zeros((half, ys_ref.shape[1]), F32)

    lower, upper = slice(0, half), slice(half, 2 * half)
    whole = used1 & jnp.logical_not(first1)
    pl.when(first0)(functools.partial(stage, h0))
    pl.when(whole)(functools.partial(run, slice(None), e0))
    pl.when(used0 & jnp.logical_not(whole))(functools.partial(run, lower, e0))

    @pl.when(first1)
    def _():
        stage(h1)
        run(upper, e1)

    pl.when(jnp.logical_not(used0))(functools.partial(clear, lower))
    pl.when(jnp.logical_not(used1))(functools.partial(clear, upper))


def _experts(half_expert, n_halves, next_expert, xs, layer, wg, bg, wu, bu, wd, bd):
    rows = xs.shape[0]
    depth, ne, d, f = wg.shape
    nb = rows // EXPERT_ROWS
    blk = lambda b, be, nu, nxt: (jnp.clip(b, 0, (nu[0] + 1) // 2 - 1), 0)
    stack = lambda w: w.reshape((depth * ne,) + w.shape[2:])
    bias = lambda v: pl.BlockSpec((depth * ne, v.shape[2]), lambda b, be, nu, nxt: (0, 0))
    anywhere = pl.BlockSpec(memory_space=pl.ANY)
    return pl.pallas_call(
        functools.partial(_expert_kernel, first_weight=layer * ne),
        grid_spec=pltpu.PrefetchScalarGridSpec(
            num_scalar_prefetch=3,
            grid=(nb,),
            in_specs=[pl.BlockSpec((EXPERT_ROWS, d), blk), bias(bg), bias(bu), bias(bd),
                      anywhere, anywhere, anywhere],
            out_specs=pl.BlockSpec((EXPERT_ROWS, d), lambda b, be, nu, nxt: (b, 0)),
            scratch_shapes=[pltpu.VMEM((d, f), F32), pltpu.VMEM((d, f), F32), pltpu.VMEM((f, d), F32),
                            pltpu.VMEM((d, f), BF16), pltpu.VMEM((d, f), BF16), pltpu.VMEM((f, d), BF16),
                            pltpu.SemaphoreType.DMA((3,))]),
        out_shape=jax.ShapeDtypeStruct((rows, d), F32),
        compiler_params=_cparams(("arbitrary",)),
        name="moe_experts",
    )(half_expert, n_halves, next_expert, xs, stack(bg), stack(bu), stack(bd), stack(wg), stack(wu), stack(wd))


def _combine_kernel(n_ref, local_ref, global_ref, ys_ref, slot_ref, gate_ref, x_ref, g_ref, b_ref, o_ref,
                    buf, sem, *, tile0, ne, alpha):
    step = pl.program_id(0)
    tile = step + tile0
    cur = step % 2

    def copy(b):
        return lambda n, lo, go: pltpu.make_async_copy(ys_ref.at[pl.ds(go, n)], buf.at[b, pl.ds(lo, n)], sem.at[b])

    def fetch(t, b):
        covered = TOP_K * slot_ref.shape[0]
        buf[b, covered:] = jnp.zeros((buf.shape[1] - covered, buf.shape[2]), F32)
        _for_each_segment(n_ref, local_ref, global_ref, t, ne, lambda *a: copy(b)(*a).start())

    @pl.when(step == 0)
    def _():
        fetch(tile, cur)

    @pl.when(step < pl.num_programs(0) - 1)
    def _():
        fetch(tile + 1, 1 - cur)

    slot = slot_ref[...]
    gate = gate_ref[...]
    rows = buf.shape[1]
    r = lax.broadcasted_iota(I32, (slot.shape[0], rows), 1)
    weights = jnp.zeros(r.shape, F32)
    for k in range(TOP_K):
        weights = jnp.where(slot[:, k:k + 1] == r, gate[:, k:k + 1], weights)
    _for_each_segment(n_ref, local_ref, global_ref, tile, ne, lambda *a: copy(cur)(*a).wait())
    y = jnp.dot(weights.astype(BF16), buf[cur].astype(BF16), preferred_element_type=F32)
    o_ref[...] = _layer_norm(alpha * x_ref[...] + y, g_ref[...], b_ref[...])


def _combine(seg, ys, slot_t, gate_t, x, g, b, tile0, tt, alpha):
    m, d = x.shape
    return pl.pallas_call(
        functools.partial(_combine_kernel, tile0=tile0, ne=NUM_EXPERTS, alpha=alpha),
        grid_spec=pltpu.PrefetchScalarGridSpec(
            num_scalar_prefetch=3,
            grid=(m // tt,),
            in_specs=[pl.BlockSpec(memory_space=pl.ANY),
                      pl.BlockSpec((tt, TOP_K), lambda i, *_: (i, 0)),
                      pl.BlockSpec((tt, TOP_K), lambda i, *_: (i, 0)),
                      pl.BlockSpec((tt, d), lambda i, *_: (i, 0)),
                      pl.BlockSpec((1, d), lambda i, *_: (0, 0)),
                      pl.BlockSpec((1, d), lambda i, *_: (0, 0))],
            out_specs=pl.BlockSpec((tt, d), lambda i, *_: (i, 0)),
            scratch_shapes=[pltpu.VMEM((2, _sorted_rows(tt), d), F32), pltpu.SemaphoreType.DMA((2,))]),
        out_shape=jax.ShapeDtypeStruct((m, d), F32),
        compiler_params=_cparams(("arbitrary",)),
        name="moe_combine",
    )(*seg, ys, slot_t, gate_t, x, g, b)


def _moe_ln(xs_groups, routed, layer, wg, bg, wu, bu, wd, bd, g, b, alpha):
    ne = wg.shape[1]
    assert ne == NUM_EXPERTS
    d = xs_groups[0].shape[1]
    tiles = [_tile(x.shape[0], MOE_TILE) for x in xs_groups]
    cnt = jnp.concatenate([r[2][:, :, 0] for r in routed], axis=0)
    n_tiles = cnt.shape[0]
    n8 = (cnt + (SUBLANES - 1)) // SUBLANES * SUBLANES
    local = jnp.cumsum(n8, axis=1) - n8
    per_expert = jnp.sum(n8, axis=0)
    half = EXPERT_ROWS // 2
    per_expert_pad = (per_expert + (half - 1)) // half * half
    expert_base = jnp.cumsum(per_expert_pad) - per_expert_pad
    glob = expert_base[None, :] + jnp.cumsum(n8, axis=0) - n8
    seg = tuple(a.reshape(-1).astype(I32) for a in (n8, local, glob))
    max_rows = sum(_sorted_rows(tt) * (x.shape[0] // tt) for x, tt in zip(xs_groups, tiles)) + ne * (half - 1)
    nb = -(-max_rows // EXPERT_ROWS)
    halves_per_expert = per_expert_pad // half
    n_halves = jnp.sum(halves_per_expert).astype(I32).reshape(1)
    ends = jnp.cumsum(halves_per_expert)
    halves = jnp.arange(2 * nb, dtype=I32)
    half_expert = jnp.minimum(jnp.sum((ends[None, :] <= halves[:, None]).astype(I32), axis=1), ne - 1)
    following = ends[half_expert]
    next_expert = jnp.where(following < n_halves[0], half_expert[jnp.minimum(following, 2 * nb - 1)], -1).astype(I32)
    fills = (jnp.concatenate([per_expert_pad - per_expert, jnp.where(halves >= n_halves[0], half, 0)]),
             jnp.concatenate([expert_base + per_expert, halves * half]))
    fills = tuple(a.astype(I32) for a in fills)

    xs = _dispatch(seg, fills, xs_groups, tiles, [r[0] for r in routed], nb * EXPERT_ROWS)
    ys = _experts(half_expert, n_halves, next_expert, xs, layer, wg, bg, wu, bu, wd, bd)
    outs = []
    tile0 = 0
    for x, tt, r in zip(xs_groups, tiles, routed):
        outs.append(_combine(seg, ys, r[0].T, r[1].T, x, g, b, tile0, tt, alpha))
        tile0 += x.shape[0] // tt
    return outs


def _rope_table_kernel(freq_ref, cos_ref, sin_ref, *, start):
    pos = (lax.broadcasted_iota(I32, cos_ref.shape, 0) + start).astype(F32)
    ang = pos * freq_ref[...]
    cos_ref[...] = jnp.cos(ang)
    sin_ref[...] = jnp.sin(ang)


def _rope_table(start, n, rope_dim):
    half = rope_dim // 2
    freqs = ROPE_THETA ** (-jnp.arange(half, dtype=F32) / half)
    freq_row = jnp.concatenate([freqs, freqs]).reshape(1, rope_dim)
    rows = -(-n // SUBLANES) * SUBLANES
    shp = jax.ShapeDtypeStruct((rows, rope_dim), F32)
    full = pl.BlockSpec((rows, rope_dim), lambda i: (0, 0))
    cos, sin = pl.pallas_call(
        functools.partial(_rope_table_kernel, start=start),
        grid=(1,),
        in_specs=[pl.BlockSpec((1, rope_dim), lambda i: (0, 0))],
        out_specs=[full, full],
        out_shape=[shp, shp],
        name="rope_table",
    )(freq_row)
    return cos[:n], sin[:n]


def _swap_halves_signed(w):
    half = w.shape[-1] // 2
    return jnp.concatenate([-w[..., half:], w[..., :half]], axis=-1)


def _kv_kernel(x_ref, wc_ref, wr_ref, wrs_ref, g_ref, cos_ref, sin_ref, c_ref, kr_ref):
    x = x_ref[...].astype(BF16)
    c = jnp.dot(x, wc_ref[...], preferred_element_type=F32)
    c_ref[...] = _rms_norm(c, g_ref[...])
    r = jnp.dot(x, wr_ref[...], preferred_element_type=F32)
    rs = jnp.dot(x, wrs_ref[...], preferred_element_type=F32)
    kr_ref[...] = r * cos_ref[...] + rs * sin_ref[...]


def _shared_kv(x, wc, wr, wrs, g, cos, sin):
    m, d = x.shape
    rk = wc.shape[1]
    e = wr.shape[1]
    tm = _tile(min(m, cos.shape[0]), ROW_TILE)
    period = cos.shape[0] // tm
    full = lambda arr: pl.BlockSpec(arr.shape, lambda i: (0, 0))
    tab = pl.BlockSpec((tm, e), lambda i: (i % period, 0))
    return pl.pallas_call(
        _kv_kernel,
        grid=(m // tm,),
        in_specs=[pl.BlockSpec((tm, d), lambda i: (i, 0)), full(wc), full(wr), full(wrs), full(g), tab, tab],
        out_specs=[pl.BlockSpec((tm, rk), lambda i: (i, 0)), pl.BlockSpec((tm, e), lambda i: (i, 0))],
        out_shape=[jax.ShapeDtypeStruct((m, rk), F32), jax.ShapeDtypeStruct((m, e), F32)],
        compiler_params=_cparams(("parallel",)),
        name="shared_kv",
    )(x, wc, wr, wrs, g, cos, sin)


def _kv_up_kernel(c_ref, kr_ref, wk_ref, rep_ref, wv_ref, k_ref, v_ref):
    c = c_ref[...].astype(BF16)
    k = jnp.dot(c, wk_ref[...], preferred_element_type=F32)
    k = k + jnp.dot(kr_ref[...].astype(BF16), rep_ref[...], preferred_element_type=F32)
    k_ref[...] = k.astype(BF16)
    v_ref[...] = jnp.dot(c, wv_ref[...], preferred_element_type=F32).astype(BF16)


def _kv_up(c, kr, wk_pad, rep, wv):
    m, rk = c.shape
    tm = _tile(m, ROW_TILE)
    full = lambda arr: pl.BlockSpec(arr.shape, lambda i: (0, 0))
    row = lambda n: pl.BlockSpec((tm, n), lambda i: (i, 0))
    return pl.pallas_call(
        _kv_up_kernel,
        grid=(m // tm,),
        in_specs=[row(rk), row(kr.shape[1]), full(wk_pad), full(rep), full(wv)],
        out_specs=[row(wk_pad.shape[1]), row(wv.shape[1])],
        out_shape=[jax.ShapeDtypeStruct((m, wk_pad.shape[1]), BF16), jax.ShapeDtypeStruct((m, wv.shape[1]), BF16)],
        compiler_params=_cparams(("parallel",)),
        name="kv_up",
    )(c, kr, wk_pad, rep, wv)


def _q_prompt_kernel(x_ref, wc_ref, wm_ref, g_ref, wq_ref, wqs_ref, cos_ref, sin_ref, q_ref, qm_ref):
    x = x_ref[...].astype(BF16)
    qm_ref[...] = jnp.dot(x, wm_ref[...], preferred_element_type=F32).astype(BF16)
    cq = _rms_norm(jnp.dot(x, wc_ref[...], preferred_element_type=F32), g_ref[...]).astype(BF16)
    qa = jnp.dot(cq, wq_ref[...], preferred_element_type=F32)
    qb = jnp.dot(cq, wqs_ref[...], preferred_element_type=F32)
    cos = cos_ref[...]
    sin = sin_ref[...]
    for h in range(q_ref.shape[1] // LANES):
        sl = slice(h * LANES, (h + 1) * LANES)
        q_ref[:, sl] = (qa[:, sl] * cos + qb[:, sl] * sin).astype(BF16)


def _q_prompt(x, wc, wm, g, wq_pad, wqs_pad, cos_slot, sin_slot):
    m, d = x.shape
    tm = _tile(min(m, cos_slot.shape[0]), ROW_TILE)
    period = cos_slot.shape[0] // tm
    full = lambda arr: pl.BlockSpec(arr.shape, lambda i: (0, 0))
    tab = pl.BlockSpec((tm, LANES), lambda i: (i % period, 0))
    row = lambda n: pl.BlockSpec((tm, n), lambda i: (i, 0))
    return pl.pallas_call(
        _q_prompt_kernel,
        grid=(m // tm,),
        in_specs=[row(d), full(wc), full(wm), full(g), full(wq_pad), full(wqs_pad), tab, tab],
        out_specs=[row(wq_pad.shape[1]), row(wm.shape[1])],
        out_shape=[jax.ShapeDtypeStruct((m, wq_pad.shape[1]), BF16), jax.ShapeDtypeStruct((m, wm.shape[1]), BF16)],
        compiler_params=_cparams(("parallel",)),
        name="q_prompt",
    )(x, wc, wm, g, wq_pad, wqs_pad, cos_slot, sin_slot)


def _mla_prompt_kernel(q_ref, k_ref, v_ref, o_ref, *, v_dim, tk):
    qi = pl.program_id(2)
    tq = q_ref.shape[0]
    heads = [slice(hh * LANES, (hh + 1) * LANES) for hh in range(2)]
    q = [q_ref[:, sl] for sl in heads]

    def attend(carry, start, visible):
        kb = k_ref[pl.ds(start, tk), :]
        vb = v_ref[pl.ds(start, tk), :]
        out = []
        for hh in range(2):
            m_old, l_old, acc = carry[hh]
            s = lax.dot_general(q[hh], kb[:, heads[hh]], _NT, preferred_element_type=F32)
            if visible is not None:
                s = jnp.where(visible, s, MASK_VALUE)
            m_new = jnp.maximum(m_old, jnp.max(s, axis=-1, keepdims=True))
            a = jnp.exp2(m_old - m_new)
            p = jnp.exp2(s - m_new)
            out.append((m_new, a * l_old + jnp.sum(p, axis=-1, keepdims=True),
                        a * acc + jnp.dot(p.astype(BF16), vb, preferred_element_type=F32)))
        return tuple(out)

    init = tuple((jnp.full((tq, 1), -jnp.inf, F32), jnp.zeros((tq, 1), F32), jnp.zeros((tq, 2 * v_dim), F32))
                 for _ in range(2))
    per_tile = tq // tk
    carry = lax.fori_loop(0, qi * per_tile, lambda ki, c: attend(c, pl.multiple_of(ki * tk, tk), None), init)
    row = lax.broadcasted_iota(I32, (tq, tk), 0)
    col = lax.broadcasted_iota(I32, (tq, tk), 1)
    for j in range(per_tile):
        carry = attend(carry, pl.multiple_of(qi * tq + j * tk, tk), col + j * tk <= row)
    (_, l0, acc0), (_, l1, acc1) = carry
    lane = lax.broadcasted_iota(I32, (tq, 2 * v_dim), 1)
    o_ref[...] = jnp.where(lane < v_dim, acc0 / l0, acc1 / l1).astype(o_ref.dtype)


def _mla_prompt(q, k, v, batch):
    m, hw = q.shape
    heads = hw // LANES
    v_dim = v.shape[1] // heads
    assert heads % 2 == 0 and 2 * v_dim == LANES
    s = m // batch
    tq = _tile(s, ATTN_Q_TILE)
    tk = _tile(tq, ATTN_K_TILE)
    nq = s // tq
    return pl.pallas_call(
        functools.partial(_mla_prompt_kernel, v_dim=v_dim, tk=tk),
        grid=(batch, heads // 2, nq),
        in_specs=[pl.BlockSpec((tq, 2 * LANES), lambda b, h, qi: (b * nq + qi, h)),
                  pl.BlockSpec((s, 2 * LANES), lambda b, h, qi: (b, h)),
                  pl.BlockSpec((s, 2 * v_dim), lambda b, h, qi: (b, h))],
        out_specs=pl.BlockSpec((tq, 2 * v_dim), lambda b, h, qi: (b * nq + qi, h)),
        out_shape=jax.ShapeDtypeStruct((m, heads * v_dim), BF16),
        compiler_params=_cparams(("parallel", "parallel", "arbitrary")),
        name="mla_prompt",
    )(q, k, v)


def _q_sample_kernel(x_ref, wc_ref, wm_ref, g_ref, wn_ref, wr_ref, wrs_ref, wuk_ref, cos_ref, sin_ref,
                     ql_ref, qr_ref, qm_ref):
    x = x_ref[...].astype(BF16)
    qm_ref[...] = jnp.dot(x, wm_ref[...], preferred_element_type=F32)
    cq = _rms_norm(jnp.dot(x, wc_ref[...], preferred_element_type=F32), g_ref[...]).astype(BF16)
    cos = cos_ref[...]
    sin = sin_ref[...]
    for h in range(wn_ref.shape[0]):
        qn = jnp.dot(cq, wn_ref[h], preferred_element_type=F32).astype(BF16)
        ql_ref[h] = jnp.dot(qn, wuk_ref[h], preferred_element_type=F32).astype(BF16)
        qr = jnp.dot(cq, wr_ref[h], preferred_element_type=F32) * cos
        qr_ref[h] = (qr + jnp.dot(cq, wrs_ref[h], preferred_element_type=F32) * sin).astype(BF16)


def _q_sample(x, wc, wm, g, wn, wr, wrs, wuk_t, cos, sin):
    m, d = x.shape
    heads = wn.shape[0]
    rk = wuk_t.shape[2]
    e = wr.shape[2]
    full = lambda arr: pl.BlockSpec(arr.shape, lambda i: (0,) * arr.ndim)
    return pl.pallas_call(
        _q_sample_kernel,
        grid=(1,),
        in_specs=[full(a) for a in (x, wc, wm, g, wn, wr, wrs, wuk_t, cos, sin)],
        out_specs=[pl.BlockSpec((heads, m, rk), lambda i: (0, 0, 0)),
                   pl.BlockSpec((heads, m, e), lambda i: (0, 0, 0)),
                   pl.BlockSpec((m, wm.shape[1]), lambda i: (0, 0))],
        out_shape=[jax.ShapeDtypeStruct((heads, m, rk), BF16), jax.ShapeDtypeStruct((heads, m, e), BF16),
                   jax.ShapeDtypeStruct((m, wm.shape[1]), F32)],
        compiler_params=_cparams(("arbitrary",)),
        name="q_sample",
    )(x, wc, wm, g, wn, wr, wrs, wuk_t, cos, sin)


def _mla_sample_kernel(pt_ref, ql_ref, qr_ref, cn_ref, kn_ref, cache_c, cache_kr, o_ref,
                       kv_buf, kr_buf, s_ref, sem, *, n_pages, new_len, scale, chunk):
    b = pl.program_id(0)
    cur = b % 2
    page = cache_c.shape[1]
    past = kv_buf.shape[1]

    def for_each_page(req, buf, act):
        def body(p, c):
            src = pt_ref[req * n_pages + p]
            at = pl.multiple_of(p * page, page)
            act(pltpu.make_async_copy(cache_c.at[src], kv_buf.at[buf, pl.ds(at, page)], sem.at[buf]))
            act(pltpu.make_async_copy(cache_kr.at[src], kr_buf.at[buf, :, pl.ds(at, page)], sem.at[buf]))
            return c

        lax.fori_loop(0, n_pages, body, 0)

    @pl.when(b == 0)
    def _():
        for_each_page(b, cur, lambda c: c.start())

    @pl.when(b + 1 < pl.num_programs(0))
    def _():
        for_each_page(b + 1, 1 - cur, lambda c: c.start())

    ql = ql_ref[0]
    qr = qr_ref[0]
    nq, nk = ql.shape[0], cn_ref.shape[1]
    cn = cn_ref[0].astype(BF16)
    s_of_row = lax.broadcasted_iota(I32, (nq, nk), 0) % new_len
    j = lax.broadcasted_iota(I32, (nq, nk), 1)
    s_new = lax.dot_general(ql, cn, _NT, preferred_element_type=F32)
    s_new = (s_new + lax.dot_general(qr, kn_ref[0].astype(BF16), _NT, preferred_element_type=F32)) * scale
    s_new = jnp.where(j <= s_of_row, s_new, MASK_VALUE)

    for_each_page(b, cur, lambda c: c.wait())
    for c in range(past // chunk):
        keys = slice(c * chunk, (c + 1) * chunk)
        s = lax.dot_general(ql, kv_buf[cur, keys, :].astype(BF16), _NT, preferred_element_type=F32)
        s = s + jnp.dot(qr, kr_buf[cur, :, keys].astype(BF16), preferred_element_type=F32)
        s_ref[:, keys] = s * scale
    s = s_ref[...]
    m = jnp.maximum(jnp.max(s, axis=-1, keepdims=True), jnp.max(s_new, axis=-1, keepdims=True))
    p = jnp.exp(s - m)
    p_new = jnp.exp(s_new - m)
    l = jnp.sum(p, axis=-1, keepdims=True) + jnp.sum(p_new, axis=-1, keepdims=True)
    acc = jnp.dot(p_new.astype(BF16), cn, preferred_element_type=F32)
    p = p.astype(BF16)
    for c in range(past // chunk):
        keys = slice(c * chunk, (c + 1) * chunk)
        acc = acc + jnp.dot(p[:, keys], kv_buf[cur, keys, :].astype(BF16), preferred_element_type=F32)
    o_ref[0] = acc / l


def _mla_sample(page_table, ql, qr, cache_c, cache_kr_t, c_new, kr_new, new_len, scale):
    bs, n_pages = page_table.shape
    _, nq, rk = ql.shape
    e = qr.shape[2]
    page = cache_c.shape[1]
    past = n_pages * page
    chunk = math.gcd(past, KEY_CHUNK)
    per_b = lambda n, w: pl.BlockSpec((1, n, w), lambda b, pt: (b, 0, 0))
    s8 = c_new.shape[1]
    return pl.pallas_call(
        functools.partial(_mla_sample_kernel, n_pages=n_pages, new_len=new_len, scale=scale, chunk=chunk),
        grid_spec=pltpu.PrefetchScalarGridSpec(
            num_scalar_prefetch=1,
            grid=(bs,),
            in_specs=[per_b(nq, rk), per_b(nq, e), per_b(s8, rk), per_b(s8, e),
                      pl.BlockSpec(memory_space=pl.ANY), pl.BlockSpec(memory_space=pl.ANY)],
            out_specs=per_b(nq, rk),
            scratch_shapes=[pltpu.VMEM((2, past, rk), F32), pltpu.VMEM((2, e, past), F32),
                            pltpu.VMEM((nq, past), F32), pltpu.SemaphoreType.DMA((2,))]),
        out_shape=jax.ShapeDtypeStruct((bs, nq, rk), F32),
        compiler_params=_cparams(("arbitrary",)),
        name="mla_sample",
    )(page_table.reshape(-1), ql, qr, c_new, kr_new, cache_c, cache_kr_t)


def _head_up_kernel(x_ref, w_ref, o_ref):
    o_ref[0] = jnp.dot(x_ref[0].astype(BF16), w_ref[0], preferred_element_type=F32)


def _head_up(lat, wv):
    h, m, rk = lat.shape
    v = wv.shape[2]
    return pl.pallas_call(
        _head_up_kernel,
        grid=(h,),
        in_specs=[pl.BlockSpec((1, m, rk), lambda i: (i, 0, 0)), pl.BlockSpec((1, rk, v), lambda i: (i, 0, 0))],
        out_specs=pl.BlockSpec((1, m, v), lambda i: (i, 0, 0)),
        out_shape=jax.ShapeDtypeStruct((h, m, v), F32),
        compiler_params=_cparams(("parallel",)),
        name="head_up",
    )(lat, wv)


def _head_slots(w, heads, width):
    k = w.shape[0]
    w = w.reshape(k, heads, width)
    return jnp.pad(w, ((0, 0), (0, 0), (0, LANES - width))).reshape(k, heads * LANES)


def kernel(x_prompt, x_sample, cache_kv_latent, cache_k_rope, cache_mem_k, cache_mem_v, state_conv, page_table, mem_prompt, w_in_a, conv_w, w_out_a, w_in_b, g_q, w_uq, w_out_b, w_kv_down, g_kv, w_uk, w_uv, w_mem_k, w_mem_v, ln1_g, ln1_b, ln2_g, ln2_b, w_router, b_router, w_gate, b_gate, w_up, b_up, w_down, b_down):
    bp, sp, d = x_prompt.shape
    bs, ss, _ = x_sample.shape
    depth = ln1_g.shape[0]
    n_a = w_in_a.shape[0]
    d_conv = conv_w.shape[2]
    assert conv_w.shape[1] == 3
    n_mem = mem_prompt.shape[1]
    mem_heads = cache_mem_k.shape[3]
    mem_w = mem_heads * cache_mem_k.shape[4]
    kv_rank, heads, nope = w_uk.shape
    v_dim = w_uv.shape[2]
    rope_dim = cache_k_rope.shape[2]
    q_rank = g_q.shape[1]
    n_past = page_table.shape[1] * cache_kv_latent.shape[1]
    alpha = (2 * depth) ** 0.25
    scale = (nope + rope_dim) ** -0.5
    half = rope_dim // 2

    xp = x_prompt.reshape(bp * sp, d)
    xs = x_sample.reshape(bs * ss, d)

    w_mem = jnp.concatenate([w_mem_k[l] for l in range(depth)] + [w_mem_v[l] for l in range(depth)],
                            axis=1).astype(BF16)
    mem_k_flat, mem_v_flat = _mem_kv(mem_prompt.reshape(bp * n_mem, d), w_mem, depth)
    mem_k_p = mem_k_flat.reshape(depth, bp, n_mem, mem_w)
    mem_v_p = mem_v_flat.reshape(depth, bp, n_mem, mem_w)
    mem_k_s = cache_mem_k.reshape(depth, bs, n_mem, mem_w)
    mem_v_s = cache_mem_v.reshape(depth, bs, n_mem, mem_w)

    conv_p, conv_s = [], []
    c_p = kr_p = c_s = kr_s = None
    k_p = v_p = None
    tables = None

    for l in range(depth):
        g1, b1 = ln1_g[l].reshape(1, d), ln1_b[l].reshape(1, d)
        wt = w_router[l].T
        wt_hi = wt.astype(BF16)
        wt2 = jnp.concatenate([wt_hi, (wt - wt_hi.astype(F32)).astype(BF16)], axis=0)
        ln_route = functools.partial(_out_ln_route, g=g1, b=b1, wt=wt2, br=b_router[l].reshape(-1, 1),
                                     alpha=alpha)
        if l < n_a:
            w_in = w_in_a[l].astype(BF16)
            w_out = w_out_a[l].astype(BF16)
            splits = [d_conv, d_conv, d_conv, mem_w]
            y, qm, st = _proj_conv(xp, w_in, jnp.zeros((bp, 2, d_conv), F32), conv_w[l], bp)
            conv_p.append(st)
            ym = _mem_attn(qm.reshape(bp, sp, mem_w), mem_k_p, mem_v_p, l, mem_heads).reshape(bp * sp, mem_w)
            xp, *route_p = ln_route(y, ym, xp, w_out[:d_conv], w_out[d_conv:])

            gb, gc, hv, qm = _proj(xs, w_in, splits, "in_proj_a")
            r3 = lambda a: a.reshape(bs, ss, d_conv)
            y, st = _conv_short(r3(gb), r3(gc), r3(hv), state_conv[l], conv_w[l])
            conv_s.append(st)
            ym = _mem_attn(qm.reshape(bs, ss, mem_w), mem_k_s, mem_v_s, l, mem_heads).reshape(bs * ss, mem_w)
            xs, *route_s = ln_route(y.reshape(bs * ss, d_conv), ym, xs, w_out[:d_conv], w_out[d_conv:])
        else:
            j = l - n_a
            if tables is None:
                cos_p, sin_p = _rope_table(0, sp, rope_dim)
                cos_s, sin_s = _rope_table(n_past, ss, rope_dim)
                cos_s, sin_s = jnp.tile(cos_s, (bs, 1)), jnp.tile(sin_s, (bs, 1))
                pad = LANES - nope - rope_dim
                c2 = scale * math.log2(math.e)
                cos_slot = c2 * jnp.concatenate([jnp.ones((sp, nope), F32), cos_p, jnp.zeros((sp, pad), F32)], axis=1)
                sin_slot = c2 * jnp.concatenate([jnp.zeros((sp, nope), F32), sin_p, jnp.zeros((sp, pad), F32)], axis=1)
                tables = True
                wc = w_kv_down[:, :kv_rank].astype(BF16)
                wr = w_kv_down[:, kv_rank:]
                wrs = _swap_halves_signed(wr).astype(BF16)
                wr = wr.astype(BF16)
                gk = g_kv.reshape(1, kv_rank)
                c_p, kr_p = _shared_kv(xp, wc, wr, wrs, gk, cos_p, sin_p)
                c_s, kr_s = _shared_kv(xs, wc, wr, wrs, gk, cos_s, sin_s)
                wk_pad = _head_slots(w_uk.reshape(kv_rank, heads * nope), heads, nope).astype(BF16)
                rep = jnp.pad(jnp.eye(rope_dim, dtype=F32), ((0, 0), (nope, pad)))
                rep = jnp.tile(rep, (1, heads)).astype(BF16)
                k_p, v_p = _kv_up(c_p, kr_p, wk_pad, rep, w_uv.reshape(kv_rank, heads * v_dim).astype(BF16))
                pad_rows = -ss % SUBLANES
                c_new = jnp.pad(c_s.reshape(bs, ss, kv_rank), ((0, 0), (0, pad_rows), (0, 0)))
                kr_new = jnp.pad(kr_s.reshape(bs, ss, rope_dim), ((0, 0), (0, pad_rows), (0, 0)))
                cache_kr_t = jnp.transpose(cache_k_rope, (0, 2, 1))
                wuk_t = jnp.transpose(w_uk, (1, 2, 0)).astype(BF16)
                wuv_h = jnp.transpose(w_uv, (1, 0, 2)).astype(BF16)

            w_in = w_in_b[j]
            wcq = w_in[:, :q_rank].astype(BF16)
            wqm = w_in[:, q_rank:].astype(BF16)
            gq = g_q[j].reshape(1, q_rank)
            wq = w_uq[j].reshape(q_rank, heads, nope + rope_dim)
            wq_nope = wq[:, :, :nope]
            wq_rope = wq[:, :, nope:]
            wq_rope_sw = _swap_halves_signed(wq_rope)
            w_out = w_out_b[j].astype(BF16)
            hv_w = heads * v_dim

            slot = lambda a, b: _head_slots(jnp.concatenate([a, b], axis=2).reshape(q_rank, -1), heads,
                                            nope + rope_dim).astype(BF16)
            wq_pad = slot(wq_nope, wq_rope)
            wqs_pad = slot(jnp.zeros_like(wq_nope), wq_rope_sw)
            q, qm = _q_prompt(xp, wcq, wqm, gq, wq_pad, wqs_pad, cos_slot, sin_slot)
            att = _mla_prompt(q, k_p, v_p, bp)
            ym = _mem_attn(qm.reshape(bp, sp, mem_w), mem_k_p, mem_v_p, l, mem_heads).reshape(bp * sp, mem_w)
            xp, *route_p = ln_route(att, ym, xp, w_out[:hv_w], w_out[hv_w:])

            th = lambda a: jnp.transpose(a, (1, 0, 2)).astype(BF16)
            ql, qr, qm = _q_sample(xs, wcq, wqm, gq, th(wq_nope), th(wq_rope), th(wq_rope_sw), wuk_t,
                                   cos_s, sin_s)
            per_req = lambda a: jnp.transpose(a.reshape(heads, bs, ss, -1), (1, 0, 2, 3)).reshape(bs, heads * ss, -1)
            lat = _mla_sample(page_table, per_req(ql), per_req(qr), cache_kv_latent, cache_kr_t,
                              c_new, kr_new, ss, scale)
            lat = jnp.transpose(lat.reshape(bs, heads, ss, kv_rank), (1, 0, 2, 3)).reshape(heads, bs * ss, kv_rank)
            att = jnp.transpose(_head_up(lat, wuv_h), (1, 0, 2)).reshape(bs * ss, hv_w)
            ym = _mem_attn(qm.reshape(bs, ss, mem_w), mem_k_s, mem_v_s, l, mem_heads).reshape(bs * ss, mem_w)
            xs, *route_s = ln_route(att, ym, xs, w_out[:hv_w], w_out[hv_w:])

        xp, xs = _moe_ln([xp, xs], [route_p, route_s], l, w_gate, b_gate, w_up, b_up, w_down, b_down,
                         ln2_g[l].reshape(1, d), ln2_b[l].reshape(1, d), alpha)

    ne_shape = (depth, bp, n_mem, mem_heads, mem_w // mem_heads)
    return (xp.reshape(bp, sp, d), xs.reshape(bs, ss, d),
            c_p.reshape(bp, sp, kv_rank), kr_p.reshape(bp, sp, rope_dim),
            c_s.reshape(bs, ss, kv_rank), kr_s.reshape(bs, ss, rope_dim),
            mem_k_p.reshape(ne_shape), mem_v_p.reshape(ne_shape),
            jnp.stack(conv_p), jnp.stack(conv_s))
```

```python
import functools
import math

import jax
import jax.numpy as jnp
from jax import lax
from jax.experimental import pallas as pl
from jax.experimental.pallas import tpu as pltpu

F32 = jnp.float32
BF16 = jnp.bfloat16
I32 = jnp.int32
U32 = jnp.uint32

TOP_K = 4
SWIGLU_LIMIT = 7.0
SWIGLU_ALPHA = 1.702
ROPE_THETA = 10000.0
LN_EPS = 1e-5
RMS_EPS = 1e-6
MASK_VALUE = -1e30

LANES = 128
SUBLANES = 8
VMEM_LIMIT = 56 * 1024 * 1024

ROW_TILE = 512
MOE_TILE = 512
EXPERT_ROWS = 512
KEY_CHUNK = 2048
MEM_ATTN_ROWS = 32
ATTN_Q_TILE = 512
ATTN_K_TILE = 512
NUM_EXPERTS = 32

_NT = (((1,), (1,)), ((), ()))


def _cparams(sem):
    return pltpu.CompilerParams(dimension_semantics=sem, vmem_limit_bytes=VMEM_LIMIT)


def _tile(n, pref):
    t = min(n, pref)
    assert n % t == 0, (n, pref)
    return t


def _layer_norm(z, g, b):
    mu = jnp.mean(z, axis=-1, keepdims=True)
    zc = z - mu
    var = jnp.mean(zc * zc, axis=-1, keepdims=True)
    return zc * lax.rsqrt(var + LN_EPS) * g + b


def _rms_norm(x, g):
    return x * lax.rsqrt(jnp.mean(x * x, axis=-1, keepdims=True) + RMS_EPS) * g


def _proj_kernel(x_ref, w_ref, *o_refs):
    h = jnp.dot(x_ref[...].astype(BF16), w_ref[...], preferred_element_type=F32)
    off = 0
    for o_ref in o_refs:
        n = o_ref.shape[-1]
        o_ref[...] = h[:, off:off + n].astype(o_ref.dtype)
        off += n


def _proj(x, w, splits, name):
    m, k = x.shape
    n = w.shape[1]
    assert sum(splits) == n
    tm = _tile(m, ROW_TILE)
    return pl.pallas_call(
        _proj_kernel,
        grid=(m // tm,),
        in_specs=[pl.BlockSpec((tm, k), lambda i: (i, 0)), pl.BlockSpec((k, n), lambda i: (0, 0))],
        out_specs=[pl.BlockSpec((tm, s), lambda i: (i, 0)) for s in splits],
        out_shape=[jax.ShapeDtypeStruct((m, s), F32) for s in splits],
        compiler_params=_cparams(("parallel",)),
        name=name,
    )(x, w)


def _memkv_kernel(x_ref, w_ref, k_ref, v_ref):
    depth, _, f = k_ref.shape
    h = jnp.dot(x_ref[...].astype(BF16), w_ref[...], preferred_element_type=F32)
    for l in range(depth):
        k_ref[l] = h[:, l * f:(l + 1) * f]
        v_ref[l] = h[:, (depth + l) * f:(depth + l + 1) * f]


def _mem_kv(mem, w_cat, depth):
    r, d = mem.shape
    f = w_cat.shape[1] // (2 * depth)
    tm = _tile(r, ROW_TILE)
    shp = jax.ShapeDtypeStruct((depth, r, f), F32)
    spec = pl.BlockSpec((depth, tm, f), lambda i: (0, i, 0))
    return pl.pallas_call(
        _memkv_kernel,
        grid=(r // tm,),
        in_specs=[pl.BlockSpec((tm, d), lambda i: (i, 0)), pl.BlockSpec(w_cat.shape, lambda i: (0, 0))],
        out_specs=[spec, spec],
        out_shape=[shp, shp],
        compiler_params=_cparams(("parallel",)),
        name="mem_kv",
    )(mem, w_cat)


def _proj_conv_kernel(x_ref, w_ref, pre_ref, cw_ref, y_ref, qm_ref, st_ref, carry):
    j = pl.program_id(1)

    @pl.when(j == 0)
    def _():
        carry[0:2, :] = pre_ref[0]

    c = y_ref.shape[1]
    h = jnp.dot(x_ref[...].astype(BF16), w_ref[...], preferred_element_type=F32)
    qm_ref[...] = h[:, 3 * c:].astype(qm_ref.dtype)
    u = h[:, c:2 * c] * h[:, 2 * c:3 * c]
    ts = u.shape[0]
    c0 = carry[0:1, :]
    c1 = carry[1:2, :]
    row = lax.broadcasted_iota(I32, u.shape, 0)
    u1 = jnp.where(row == 0, c1, pltpu.roll(u, 1, 0))
    u2 = jnp.where(row == 0, c0, jnp.where(row == 1, c1, pltpu.roll(u, 2, 0)))
    w = cw_ref[...]
    y_ref[...] = (h[:, :c] * (w[0:1] * u2 + w[1:2] * u1 + w[2:3] * u)).astype(y_ref.dtype)
    last = u[ts - 2:ts, :]
    carry[0:2, :] = last
    st_ref[0] = last


def _proj_conv(x, w_in, prefix, cw, batch):
    m, d = x.shape
    c = cw.shape[1]
    n = w_in.shape[1]
    s = m // batch
    ts = _tile(s, ROW_TILE)
    nt = s // ts
    row = lambda width: pl.BlockSpec((ts, width), lambda b, j: (b * nt + j, 0))
    state = pl.BlockSpec((1, 2, c), lambda b, j: (b, 0, 0))
    return pl.pallas_call(
        _proj_conv_kernel,
        grid=(batch, nt),
        in_specs=[row(d), pl.BlockSpec((d, n), lambda b, j: (0, 0)), state, pl.BlockSpec((3, c), lambda b, j: (0, 0))],
        out_specs=[row(c), row(n - 3 * c), state],
        out_shape=[jax.ShapeDtypeStruct((m, c), BF16), jax.ShapeDtypeStruct((m, n - 3 * c), BF16),
                   jax.ShapeDtypeStruct((batch, 2, c), F32)],
        scratch_shapes=[pltpu.VMEM((SUBLANES, c), F32)],
        compiler_params=_cparams(("parallel", "arbitrary")),
        name="proj_conv",
    )(x, w_in, prefix, cw)


def _conv_short_kernel(gb_ref, gc_ref, hv_ref, pre_ref, w_ref, y_ref, st_ref):
    s_len = gb_ref.shape[1]
    w = w_ref[...]

    def u(k):
        if k < 0:
            return pre_ref[:, 2 + k, :]
        return gc_ref[:, k, :] * hv_ref[:, k, :]

    for s in range(s_len):
        y_ref[:, s, :] = gb_ref[:, s, :] * (w[0:1] * u(s - 2) + w[1:2] * u(s - 1) + w[2:3] * u(s))
    for k in range(2):
        st_ref[:, k, :] = u(s_len - 2 + k)


def _conv_short(gb, gc, hv, prefix, w):
    batch, s, c = gb.shape
    full = lambda shp: pl.BlockSpec(shp, lambda i: (0,) * len(shp))
    return pl.pallas_call(
        _conv_short_kernel,
        grid=(1,),
        in_specs=[full(gb.shape)] * 3 + [full(prefix.shape), full(w.shape)],
        out_specs=[full(gb.shape), full(prefix.shape)],
        out_shape=[jax.ShapeDtypeStruct(gb.shape, F32), jax.ShapeDtypeStruct(prefix.shape, F32)],
        compiler_params=_cparams(("arbitrary",)),
        name="conv_short",
    )(gb, gc, hv, prefix, w)


def _mem_attn_kernel(q_ref, k_ref, v_ref, o_ref, *, heads):
    _, tq, w = q_ref.shape
    hd = w // heads
    lane = lax.broadcasted_iota(I32, (tq, w), 1)
    own = [(lane >= h * hd) & (lane < (h + 1) * hd) for h in range(heads)]
    for i in range(q_ref.shape[0]):
        q = q_ref[i].astype(F32)
        q_heads = jnp.concatenate([jnp.where(m, q, 0.0) for m in own], axis=0).astype(BF16)
        k = k_ref[0, i].astype(BF16)
        v = v_ref[0, i].astype(BF16)
        s = lax.dot_general(q_heads, k, _NT, preferred_element_type=F32) * (hd ** -0.5)
        m = jnp.max(s, axis=-1, keepdims=True)
        p = jnp.exp(s - m)
        l = jnp.sum(p, axis=-1, keepdims=True)
        pv = jnp.dot(p.astype(BF16), v, preferred_element_type=F32) / l
        out = jnp.zeros((tq, w), F32)
        for h in range(heads):
            out = jnp.where(own[h], pv[h * tq:(h + 1) * tq], out)
        o_ref[i] = out.astype(o_ref.dtype)


def _mem_attn(q, mk, mv, layer, heads):
    b, s, w = q.shape
    nm = mk.shape[2]
    tq = _tile(s, ROW_TILE)
    bb = math.gcd(b, max(1, MEM_ATTN_ROWS // tq))
    return pl.pallas_call(
        functools.partial(_mem_attn_kernel, heads=heads),
        grid=(b // bb, s // tq),
        in_specs=[pl.BlockSpec((bb, tq, w), lambda i, j: (i, j, 0)),
                  pl.BlockSpec((1, bb, nm, w), lambda i, j: (layer, i, 0, 0)),
                  pl.BlockSpec((1, bb, nm, w), lambda i, j: (layer, i, 0, 0))],
        out_specs=pl.BlockSpec((bb, tq, w), lambda i, j: (i, j, 0)),
        out_shape=jax.ShapeDtypeStruct((b, s, w), BF16),
        compiler_params=_cparams(("parallel", "parallel")),
        name="mem_attn",
    )(q, mk, mv)


def _out_ln_route_kernel(a_ref, m_ref, x_ref, wa_ref, wm_ref, g_ref, b_ref, wt_ref, br_ref,
                         o_ref, pos_ref, gate_ref, cnt_ref, *, alpha):
    o = jnp.dot(a_ref[...].astype(BF16), wa_ref[...], preferred_element_type=F32)
    o = o + jnp.dot(m_ref[...].astype(BF16), wm_ref[...], preferred_element_type=F32)
    x = _layer_norm(alpha * x_ref[...] + o, g_ref[...], b_ref[...])
    o_ref[...] = x
    _route(x, wt_ref[...], br_ref[...], pos_ref, gate_ref, cnt_ref)


def _out_ln_route(a, mem, x, wa, wm, g, b, wt, br, alpha):
    m, d = x.shape
    ne = br.shape[0]
    tt = _tile(m, MOE_TILE)
    nt = m // tt
    row = lambda n: pl.BlockSpec((tt, n), lambda i: (i, 0))
    full = lambda arr: pl.BlockSpec(arr.shape, lambda i: (0, 0))
    return pl.pallas_call(
        functools.partial(_out_ln_route_kernel, alpha=alpha),
        grid=(nt,),
        in_specs=[row(a.shape[1]), row(mem.shape[1]), row(d), full(wa), full(wm), full(g), full(b), full(wt), full(br)],
        out_specs=[row(d),
                   pl.BlockSpec((TOP_K, tt), lambda i: (0, i)),
                   pl.BlockSpec((TOP_K, tt), lambda i: (0, i)),
                   pl.BlockSpec((1, ne, 1), lambda i: (i, 0, 0))],
        out_shape=[jax.ShapeDtypeStruct((m, d), F32),
                   jax.ShapeDtypeStruct((TOP_K, m), I32),
                   jax.ShapeDtypeStruct((TOP_K, m), F32),
                   jax.ShapeDtypeStruct((nt, ne, 1), I32)],
        compiler_params=_cparams(("parallel",)),
        name="out_ln_route",
    )(a, mem, x, wa, wm, g, b, wt, br)


def _route(x, wt2, bias, pos_ref, gate_ref, cnt_ref):
    ne = wt2.shape[0] // 2
    x_hi = x.astype(BF16)
    x_lo = (x - x_hi.astype(F32)).astype(BF16)
    by_hi = lax.dot_general(wt2, x_hi, _NT, preferred_element_type=F32)
    logits = (by_hi[:ne] + by_hi[ne:]) + lax.dot_general(wt2[:ne], x_lo, _NT, preferred_element_type=F32) + bias
    tt = logits.shape[1]
    eidx = lax.broadcasted_iota(I32, (ne, tt), 0)
    sels, vals = [], []
    l = logits
    for _ in range(TOP_K):
        m = jnp.max(l, axis=0, keepdims=True)
        first = jnp.min(jnp.where(l == m, eidx, ne), axis=0, keepdims=True)
        sel = eidx == first
        sels.append(sel)
        vals.append(m)
        l = jnp.where(sel, -jnp.inf, l)
    ex = [jnp.exp(v - vals[0]) for v in vals]
    den = ex[0]
    for e in ex[1:]:
        den = den + e
    gate_ref[...] = jnp.concatenate([e / den for e in ex], axis=0)

    chosen = jnp.zeros((ne, tt), F32)
    for sel in sels:
        chosen = jnp.where(sel, 1.0, chosen)
    cnt = jnp.sum(chosen, axis=1, keepdims=True)
    cnt_ref[0] = cnt.astype(I32)
    cnt8 = jnp.floor((cnt + (SUBLANES - 1)) * (1.0 / SUBLANES)) * SUBLANES
    er = lax.broadcasted_iota(I32, (ne, ne), 0)
    ec = lax.broadcasted_iota(I32, (ne, ne), 1)
    below = jnp.where(ec < er, 1.0, 0.0)
    seg_start = jnp.dot(below, jnp.broadcast_to(cnt8, (ne, LANES)), precision=lax.Precision.HIGHEST,
                        preferred_element_type=F32)[:, 0:1]
    tr = lax.broadcasted_iota(I32, (tt, tt), 0)
    tc = lax.broadcasted_iota(I32, (tt, tt), 1)
    earlier = jnp.where(tr < tc, 1.0, 0.0).astype(BF16)
    rank = jnp.dot(chosen.astype(BF16), earlier, preferred_element_type=F32)
    slot = seg_start + rank
    pos_ref[...] = jnp.concatenate(
        [jnp.sum(jnp.where(sel, slot, 0.0), axis=0, keepdims=True) for sel in sels], axis=0).astype(I32)


def _sorted_rows(tt):
    return TOP_K * tt + NUM_EXPERTS * SUBLANES


def _for_each_segment(n_ref, local_ref, global_ref, tile, ne, fn):
    def body(e, c):
        k = tile * ne + e
        n = pl.multiple_of(n_ref[k], SUBLANES)

        @pl.when(n > 0)
        def _():
            fn(n, pl.multiple_of(local_ref[k], SUBLANES), pl.multiple_of(global_ref[k], SUBLANES))
        return c

    lax.fori_loop(0, ne, body, 0)


def _dispatch_kernel(n_ref, local_ref, global_ref, fill_n_ref, fill_at_ref, *refs, ne, group_tiles):
    xs_ref, buf, zeros, sem = refs[-4:]
    step = pl.program_id(0)
    last = pl.num_programs(0) - 1
    tile = step
    cur = step % 2

    def sort_tile(x_ref, slot_ref):
        x = x_ref[...].astype(BF16)
        slot = slot_ref[...]
        rows = _sorted_rows(x.shape[0])
        r = lax.broadcasted_iota(I32, (rows, x.shape[0]), 0)
        onehot = jnp.zeros(r.shape, F32)
        for k in range(TOP_K):
            onehot = jnp.where(slot[k:k + 1, :] == r, 1.0, onehot)
        buf[cur, :rows] = jnp.dot(onehot.astype(BF16), x, preferred_element_type=F32)

    first = 0
    for g, nt in enumerate(group_tiles):
        pl.when((step >= first) & (step < first + nt))(
            functools.partial(sort_tile, refs[2 * g], refs[2 * g + 1]))
        first += nt

    def copy(b):
        return lambda n, lo, go: pltpu.make_async_copy(buf.at[b, pl.ds(lo, n)], xs_ref.at[pl.ds(go, n)], sem.at[b])

    _for_each_segment(n_ref, local_ref, global_ref, tile, ne, lambda *a: copy(cur)(*a).start())

    def fill_copies(act):
        def body(k, c):
            n = pl.multiple_of(fill_n_ref[k], SUBLANES)
            at = pl.multiple_of(fill_at_ref[k], SUBLANES)

            @pl.when(n > 0)
            def _():
                act(pltpu.make_async_copy(zeros.at[pl.ds(0, n)], xs_ref.at[pl.ds(at, n)], sem.at[2]))
            return c

        lax.fori_loop(0, fill_n_ref.shape[0], body, 0)

    @pl.when(step == 0)
    def _():
        zeros[...] = jnp.zeros(zeros.shape, F32)
        fill_copies(lambda c: c.start())

    @pl.when(step > 0)
    def _():
        _for_each_segment(n_ref, local_ref, global_ref, tile - 1, ne, lambda *a: copy(1 - cur)(*a).wait())

    @pl.when(step == last)
    def _():
        _for_each_segment(n_ref, local_ref, global_ref, tile, ne, lambda *a: copy(cur)(*a).wait())
        fill_copies(lambda c: c.wait())


def _dispatch(seg, fills, groups, tiles, slots, rows_total):
    d = groups[0].shape[1]
    group_tiles = [x.shape[0] // tt for x, tt in zip(groups, tiles)]
    in_specs, args, first = [], [], 0
    for x, tt, slot, nt in zip(groups, tiles, slots, group_tiles):
        own = lambda i, first=first, nt=nt: jnp.clip(i - first, 0, nt - 1)
        in_specs += [pl.BlockSpec((tt, d), lambda i, *_, own=own: (own(i), 0)),
                     pl.BlockSpec((TOP_K, tt), lambda i, *_, own=own: (0, own(i)))]
        args += [x, slot]
        first += nt
    return pl.pallas_call(
        functools.partial(_dispatch_kernel, ne=NUM_EXPERTS, group_tiles=tuple(group_tiles)),
        grid_spec=pltpu.PrefetchScalarGridSpec(
            num_scalar_prefetch=5,
            grid=(sum(group_tiles),),
            in_specs=in_specs,
            out_specs=pl.BlockSpec(memory_space=pl.ANY),
            scratch_shapes=[pltpu.VMEM((2, _sorted_rows(max(tiles)), d), F32), pltpu.VMEM((EXPERT_ROWS, d), F32),
                            pltpu.SemaphoreType.DMA((3,))]),
        out_shape=jax.ShapeDtypeStruct((rows_total, d), F32),
        compiler_params=_cparams(("arbitrary",)),
        name="moe_dispatch",
    )(*seg, *fills, *args)


def _expert_kernel(be_ref, nu_ref, nxt_ref, xs_ref, bg_ref, bu_ref, bd_ref, wg_hbm, wu_hbm, wd_hbm, ys_ref,
                   wg_f32, wu_f32, wd_f32, wg_bf, wu_bf, wd_bf, sem, *, first_weight):
    half = xs_ref.shape[0] // 2
    h0 = 2 * pl.program_id(0)
    h1 = h0 + 1
    n_halves = nu_ref[0]
    used0 = h0 < n_halves
    used1 = h1 < n_halves
    e0 = be_ref[h0]
    e1 = be_ref[h1]
    first0 = used0 & ((h0 == 0) | (e0 != be_ref[jnp.maximum(h0 - 1, 0)]))
    first1 = used1 & (e1 != e0)

    def weight_copies(e, act):
        w = first_weight + e
        act(pltpu.make_async_copy(wg_hbm.at[w], wg_f32, sem.at[0]))
        act(pltpu.make_async_copy(wu_hbm.at[w], wu_f32, sem.at[1]))
        act(pltpu.make_async_copy(wd_hbm.at[w], wd_f32, sem.at[2]))

    def stage(h):
        @pl.when(h == 0)
        def _():
            weight_copies(be_ref[h], lambda c: c.start())

        weight_copies(be_ref[h], lambda c: c.wait())
        wg_bf[...] = wg_f32[...].astype(BF16)
        wu_bf[...] = wu_f32[...].astype(BF16)
        wd_bf[...] = wd_f32[...].astype(BF16)

        @pl.when(nxt_ref[h] >= 0)
        def _():
            weight_copies(nxt_ref[h], lambda c: c.start())

    def run(rows, e):
        w = pl.ds(first_weight + e, 1)
        x = xs_ref[rows, :].astype(BF16)
        g = jnp.dot(x, wg_bf[...], preferred_element_type=F32) + bg_ref[w, :]
        u = jnp.dot(x, wu_bf[...], preferred_element_type=F32) + bu_ref[w, :]
        g = jnp.minimum(g, SWIGLU_LIMIT)
        u = jnp.clip(u, -SWIGLU_LIMIT, SWIGLU_LIMIT)
        h = (u + 1.0) * (g * jax.nn.sigmoid(SWIGLU_ALPHA * g))
        ys_ref[rows, :] = jnp.dot(h.astype(BF16), wd_bf[...], preferred_element_type=F32) + bd_ref[w, :]

    def clear(rows):
        ys_ref[rows, :] = jnp.zeros((half, ys_ref.shape[1]), F32)

    lower, upper = slice(0, half), slice(half, 2 * half)
    whole = used1 & jnp.logical_not(first1)
    pl.when(first0)(functools.partial(stage, h0))
    pl.when(whole)(functools.partial(run, slice(None), e0))
    pl.when(used0 & jnp.logical_not(whole))(functools.partial(run, lower, e0))

    @pl.when(first1)
    def _():
        stage(h1)
        run(upper, e1)

    pl.when(jnp.logical_not(used0))(functools.partial(clear, lower))
    pl.when(jnp.logical_not(used1))(functools.partial(clear, upper))


def _experts(half_expert, n_halves, next_expert, xs, layer, wg, bg, wu, bu, wd, bd):
    rows = xs.shape[0]
    depth, ne, d, f = wg.shape
    nb = rows // EXPERT_ROWS
    blk = lambda b, be, nu, nxt: (jnp.clip(b, 0, (nu[0] + 1) // 2 - 1), 0)
    stack = lambda w: w.reshape((depth * ne,) + w.shape[2:])
    bias = lambda v: pl.BlockSpec((depth * ne, v.shape[2]), lambda b, be, nu, nxt: (0, 0))
    anywhere = pl.BlockSpec(memory_space=pl.ANY)
    return pl.pallas_call(
        functools.partial(_expert_kernel, first_weight=layer * ne),
        grid_spec=pltpu.PrefetchScalarGridSpec(
            num_scalar_prefetch=3,
            grid=(nb,),
            in_specs=[pl.BlockSpec((EXPERT_ROWS, d), blk), bias(bg), bias(bu), bias(bd),
                      anywhere, anywhere, anywhere],
            out_specs=pl.BlockSpec((EXPERT_ROWS, d), lambda b, be, nu, nxt: (b, 0)),
            scratch_shapes=[pltpu.VMEM((d, f), F32), pltpu.VMEM((d, f), F32), pltpu.VMEM((f, d), F32),
                            pltpu.VMEM((d, f), BF16), pltpu.VMEM((d, f), BF16), pltpu.VMEM((f, d), BF16),
                            pltpu.SemaphoreType.DMA((3,))]),
        out_shape=jax.ShapeDtypeStruct((rows, d), F32),
        compiler_params=_cparams(("arbitrary",)),
        name="moe_experts",
    )(half_expert, n_halves, next_expert, xs, stack(bg), stack(bu), stack(bd), stack(wg), stack(wu), stack(wd))


def _combine_kernel(n_ref, local_ref, global_ref, ys_ref, slot_ref, gate_ref, x_ref, g_ref, b_ref, o_ref,
                    buf, sem, *, tile0, ne, alpha):
    step = pl.program_id(0)
    tile = step + tile0
    cur = step % 2

    def copy(b):
        return lambda n, lo, go: pltpu.make_async_copy(ys_ref.at[pl.ds(go, n)], buf.at[b, pl.ds(lo, n)], sem.at[b])

    def fetch(t, b):
        covered = TOP_K * slot_ref.shape[0]
        buf[b, covered:] = jnp.zeros((buf.shape[1] - covered, buf.shape[2]), F32)
        _for_each_segment(n_ref, local_ref, global_ref, t, ne, lambda *a: copy(b)(*a).start())

    @pl.when(step == 0)
    def _():
        fetch(tile, cur)

    @pl.when(step < pl.num_programs(0) - 1)
    def _():
        fetch(tile + 1, 1 - cur)

    slot = slot_ref[...]
    gate = gate_ref[...]
    rows = buf.shape[1]
    r = lax.broadcasted_iota(I32, (slot.shape[0], rows), 1)
    weights = jnp.zeros(r.shape, F32)
    for k in range(TOP_K):
        weights = jnp.where(slot[:, k:k + 1] == r, gate[:, k:k + 1], weights)
    _for_each_segment(n_ref, local_ref, global_ref, tile, ne, lambda *a: copy(cur)(*a).wait())
    y = jnp.dot(weights.astype(BF16), buf[cur].astype(BF16), preferred_element_type=F32)
    o_ref[...] = _layer_norm(alpha * x_ref[...] + y, g_ref[...], b_ref[...])


def _combine(seg, ys, slot_t, gate_t, x, g, b, tile0, tt, alpha):
    m, d = x.shape
    return pl.pallas_call(
        functools.partial(_combine_kernel, tile0=tile0, ne=NUM_EXPERTS, alpha=alpha),
        grid_spec=pltpu.PrefetchScalarGridSpec(
            num_scalar_prefetch=3,
            grid=(m // tt,),
            in_specs=[pl.BlockSpec(memory_space=pl.ANY),
                      pl.BlockSpec((tt, TOP_K), lambda i, *_: (i, 0)),
                      pl.BlockSpec((tt, TOP_K), lambda i, *_: (i, 0)),
                      pl.BlockSpec((tt, d), lambda i, *_: (i, 0)),
                      pl.BlockSpec((1, d), lambda i, *_: (0, 0)),
                      pl.BlockSpec((1, d), lambda i, *_: (0, 0))],
            out_specs=pl.BlockSpec((tt, d), lambda i, *_: (i, 0)),
            scratch_shapes=[pltpu.VMEM((2, _sorted_rows(tt), d), F32), pltpu.SemaphoreType.DMA((2,))]),
        out_shape=jax.ShapeDtypeStruct((m, d), F32),
        compiler_params=_cparams(("arbitrary",)),
        name="moe_combine",
    )(*seg, ys, slot_t, gate_t, x, g, b)


def _moe_ln(xs_groups, routed, layer, wg, bg, wu, bu, wd, bd, g, b, alpha):
    ne = wg.shape[1]
    assert ne == NUM_EXPERTS
    d = xs_groups[0].shape[1]
    tiles = [_tile(x.shape[0], MOE_TILE) for x in xs_groups]
    cnt = jnp.concatenate([r[2][:, :, 0] for r in routed], axis=0)
    n_tiles = cnt.shape[0]
    n8 = (cnt + (SUBLANES - 1)) // SUBLANES * SUBLANES
    local = jnp.cumsum(n8, axis=1) - n8
    per_expert = jnp.sum(n8, axis=0)
    half = EXPERT_ROWS // 2
    per_expert_pad = (per_expert + (half - 1)) // half * half
    expert_base = jnp.cumsum(per_expert_pad) - per_expert_pad
    glob = expert_base[None, :] + jnp.cumsum(n8, axis=0) - n8
    seg = tuple(a.reshape(-1).astype(I32) for a in (n8, local, glob))
    max_rows = sum(_sorted_rows(tt) * (x.shape[0] // tt) for x, tt in zip(xs_groups, tiles)) + ne * (half - 1)
    nb = -(-max_rows // EXPERT_ROWS)
    halves_per_expert = per_expert_pad // half
    n_halves = jnp.sum(halves_per_expert).astype(I32).reshape(1)
    ends = jnp.cumsum(halves_per_expert)
    halves = jnp.arange(2 * nb, dtype=I32)
    half_expert = jnp.minimum(jnp.sum((ends[:, None] <= halves[None, :]).astype(I32), axis=0), ne - 1)
    following = ends[half_expert]
    next_expert = jnp.where(following < n_halves[0], half_expert[jnp.minimum(following, 2 * nb - 1)], -1).astype(I32)
    fills = (jnp.concatenate([per_expert_pad - per_expert, jnp.where(halves >= n_halves[0], half, 0)]),
             jnp.concatenate([expert_base + per_expert, halves * half]))
    fills = tuple(a.astype(I32) for a in fills)

    xs = _dispatch(seg, fills, xs_groups, tiles, [r[0] for r in routed], nb * EXPERT_ROWS)
    ys = _experts(half_expert, n_halves, next_expert, xs, layer, wg, bg, wu, bu, wd, bd)
    outs = []
    tile0 = 0
    for x, tt, r in zip(xs_groups, tiles, routed):
        outs.append(_combine(seg, ys, r[0].T, r[1].T, x, g, b, tile0, tt, alpha))
        tile0 += x.shape[0] // tt
    return outs


def _rope_table_kernel(freq_ref, cos_ref, sin_ref, *, start):
    pos = (lax.broadcasted_iota(I32, cos_ref.shape, 0) + start).astype(F32)
    ang = pos * freq_ref[...]
    cos_ref[...] = jnp.cos(ang)
    sin_ref[...] = jnp.sin(ang)


def _rope_table(start, n, rope_dim):
    half = rope_dim // 2
    freqs = ROPE_THETA ** (-jnp.arange(half, dtype=F32) / half)
    freq_row = jnp.concatenate([freqs, freqs]).reshape(1, rope_dim)
    rows = -(-n // SUBLANES) * SUBLANES
    shp = jax.ShapeDtypeStruct((rows, rope_dim), F32)
    full = pl.BlockSpec((rows, rope_dim), lambda i: (0, 0))
    cos, sin = pl.pallas_call(
        functools.partial(_rope_table_kernel, start=start),
        grid=(1,),
        in_specs=[pl.BlockSpec((1, rope_dim), lambda i: (0, 0))],
        out_specs=[full, full],
        out_shape=[shp, shp],
        name="rope_table",
    )(freq_row)
    return cos[:n], sin[:n]


def _swap_halves_signed(w):
    half = w.shape[-1] // 2
    return jnp.concatenate([-w[..., half:], w[..., :half]], axis=-1)


def _kv_kernel(x_ref, wc_ref, wr_ref, wrs_ref, g_ref, cos_ref, sin_ref, c_ref, kr_ref):
    x = x_ref[...].astype(BF16)
    c = jnp.dot(x, wc_ref[...], preferred_element_type=F32)
    c_ref[...] = _rms_norm(c, g_ref[...])
    r = jnp.dot(x, wr_ref[...], preferred_element_type=F32)
    rs = jnp.dot(x, wrs_ref[...], preferred_element_type=F32)
    kr_ref[...] = r * cos_ref[...] + rs * sin_ref[...]


def _shared_kv(x, wc, wr, wrs, g, cos, sin):
    m, d = x.shape
    rk = wc.shape[1]
    e = wr.shape[1]
    tm = _tile(min(m, cos.shape[0]), ROW_TILE)
    period = cos.shape[0] // tm
    full = lambda arr: pl.BlockSpec(arr.shape, lambda i: (0, 0))
    tab = pl.BlockSpec((tm, e), lambda i: (i % period, 0))
    return pl.pallas_call(
        _kv_kernel,
        grid=(m // tm,),
        in_specs=[pl.BlockSpec((tm, d), lambda i: (i, 0)), full(wc), full(wr), full(wrs), full(g), tab, tab],
        out_specs=[pl.BlockSpec((tm, rk), lambda i: (i, 0)), pl.BlockSpec((tm, e), lambda i: (i, 0))],
        out_shape=[jax.ShapeDtypeStruct((m, rk), F32), jax.ShapeDtypeStruct((m, e), F32)],
        compiler_params=_cparams(("parallel",)),
        name="shared_kv",
    )(x, wc, wr, wrs, g, cos, sin)


def _kv_up_kernel(c_ref, kr_ref, wk_ref, rep_ref, wv_ref, k_ref, v_ref):
    c = c_ref[...].astype(BF16)
    k = jnp.dot(c, wk_ref[...], preferred_element_type=F32)
    k = k + jnp.dot(kr_ref[...].astype(BF16), rep_ref[...], preferred_element_type=F32)
    k_ref[...] = k.astype(BF16)
    v_ref[...] = jnp.dot(c, wv_ref[...], preferred_element_type=F32).astype(BF16)


def _kv_up(c, kr, wk_pad, rep, wv):
    m, rk = c.shape
    tm = _tile(m, ROW_TILE)
    full = lambda arr: pl.BlockSpec(arr.shape, lambda i: (0, 0))
    row = lambda n: pl.BlockSpec((tm, n), lambda i: (i, 0))
    return pl.pallas_call(
        _kv_up_kernel,
        grid=(m // tm,),
        in_specs=[row(rk), row(kr.shape[1]), full(wk_pad), full(rep), full(wv)],
        out_specs=[row(wk_pad.shape[1]), row(wv.shape[1])],
        out_shape=[jax.ShapeDtypeStruct((m, wk_pad.shape[1]), BF16), jax.ShapeDtypeStruct((m, wv.shape[1]), BF16)],
        compiler_params=_cparams(("parallel",)),
        name="kv_up",
    )(c, kr, wk_pad, rep, wv)


def _q_prompt_kernel(x_ref, wc_ref, wm_ref, g_ref, wq_ref, wqs_ref, cos_ref, sin_ref, q_ref, qm_ref):
    x = x_ref[...].astype(BF16)
    qm_ref[...] = jnp.dot(x, wm_ref[...], preferred_element_type=F32).astype(BF16)
    cq = _rms_norm(jnp.dot(x, wc_ref[...], preferred_element_type=F32), g_ref[...]).astype(BF16)
    qa = jnp.dot(cq, wq_ref[...], preferred_element_type=F32)
    qb = jnp.dot(cq, wqs_ref[...], preferred_element_type=F32)
    cos = cos_ref[...]
    sin = sin_ref[...]
    for h in range(q_ref.shape[1] // LANES):
        sl = slice(h * LANES, (h + 1) * LANES)
        q_ref[:, sl] = (qa[:, sl] * cos + qb[:, sl] * sin).astype(BF16)


def _q_prompt(x, wc, wm, g, wq_pad, wqs_pad, cos_slot, sin_slot):
    m, d = x.shape
    tm = _tile(min(m, cos_slot.shape[0]), ROW_TILE)
    period = cos_slot.shape[0] // tm
    full = lambda arr: pl.BlockSpec(arr.shape, lambda i: (0, 0))
    tab = pl.BlockSpec((tm, LANES), lambda i: (i % period, 0))
    row = lambda n: pl.BlockSpec((tm, n), lambda i: (i, 0))
    return pl.pallas_call(
        _q_prompt_kernel,
        grid=(m // tm,),
        in_specs=[row(d), full(wc), full(wm), full(g), full(wq_pad), full(wqs_pad), tab, tab],
        out_specs=[row(wq_pad.shape[1]), row(wm.shape[1])],
        out_shape=[jax.ShapeDtypeStruct((m, wq_pad.shape[1]), BF16), jax.ShapeDtypeStruct((m, wm.shape[1]), BF16)],
        compiler_params=_cparams(("parallel",)),
        name="q_prompt",
    )(x, wc, wm, g, wq_pad, wqs_pad, cos_slot, sin_slot)


def _mla_prompt_kernel(q_ref, k_ref, v_ref, o_ref, *, v_dim, tk):
    qi = pl.program_id(2)
    tq = q_ref.shape[0]
    heads = [slice(hh * LANES, (hh + 1) * LANES) for hh in range(2)]
    q = [q_ref[:, sl] for sl in heads]

    def attend(carry, start, visible):
        kb = k_ref[pl.ds(start, tk), :]
        vb = v_ref[pl.ds(start, tk), :]
        out = []
        for hh in range(2):
            m_old, l_old, acc = carry[hh]
            s = lax.dot_general(q[hh], kb[:, heads[hh]], _NT, preferred_element_type=F32)
            if visible is not None:
                s = jnp.where(visible, s, MASK_VALUE)
            m_new = jnp.maximum(m_old, jnp.max(s, axis=-1, keepdims=True))
            a = jnp.exp2(m_old - m_new)
            p = jnp.exp2(s - m_new)
            out.append((m_new, a * l_old + jnp.sum(p, axis=-1, keepdims=True),
                        a * acc + jnp.dot(p.astype(BF16), vb, preferred_element_type=F32)))
        return tuple(out)

    init = tuple((jnp.full((tq, 1), -jnp.inf, F32), jnp.zeros((tq, 1), F32), jnp.zeros((tq, 2 * v_dim), F32))
                 for _ in range(2))
    per_tile = tq // tk
    carry = lax.fori_loop(0, qi * per_tile, lambda ki, c: attend(c, pl.multiple_of(ki * tk, tk), None), init)
    row = lax.broadcasted_iota(I32, (tq, tk), 0)
    col = lax.broadcasted_iota(I32, (tq, tk), 1)
    for j in range(per_tile):
        carry = attend(carry, pl.multiple_of(qi * tq + j * tk, tk), col + j * tk <= row)
    (_, l0, acc0), (_, l1, acc1) = carry
    lane = lax.broadcasted_iota(I32, (tq, 2 * v_dim), 1)
    o_ref[...] = jnp.where(lane < v_dim, acc0 / l0, acc1 / l1).astype(o_ref.dtype)


def _mla_prompt(q, k, v, batch):
    m, hw = q.shape
    heads = hw // LANES
    v_dim = v.shape[1] // heads
    assert heads % 2 == 0 and 2 * v_dim == LANES
    s = m // batch
    tq = _tile(s, ATTN_Q_TILE)
    tk = _tile(tq, ATTN_K_TILE)
    nq = s // tq
    return pl.pallas_call(
        functools.partial(_mla_prompt_kernel, v_dim=v_dim, tk=tk),
        grid=(batch, heads // 2, nq),
        in_specs=[pl.BlockSpec((tq, 2 * LANES), lambda b, h, qi: (b * nq + qi, h)),
                  pl.BlockSpec((s, 2 * LANES), lambda b, h, qi: (b, h)),
                  pl.BlockSpec((s, 2 * v_dim), lambda b, h, qi: (b, h))],
        out_specs=pl.BlockSpec((tq, 2 * v_dim), lambda b, h, qi: (b * nq + qi, h)),
        out_shape=jax.ShapeDtypeStruct((m, heads * v_dim), BF16),
        compiler_params=_cparams(("parallel", "parallel", "arbitrary")),
        name="mla_prompt",
    )(q, k, v)


def _q_sample_kernel(x_ref, wc_ref, wm_ref, g_ref, wn_ref, wr_ref, wrs_ref, wuk_ref, cos_ref, sin_ref,
                     ql_ref, qr_ref, qm_ref):
    x = x_ref[...].astype(BF16)
    qm_ref[...] = jnp.dot(x, wm_ref[...], preferred_element_type=F32)
    cq = _rms_norm(jnp.dot(x, wc_ref[...], preferred_element_type=F32), g_ref[...]).astype(BF16)
    cos = cos_ref[...]
    sin = sin_ref[...]
    for h in range(wn_ref.shape[0]):
        qn = jnp.dot(cq, wn_ref[h], preferred_element_type=F32).astype(BF16)
        ql_ref[h] = jnp.dot(qn, wuk_ref[h], preferred_element_type=F32).astype(BF16)
        qr = jnp.dot(cq, wr_ref[h], preferred_element_type=F32) * cos
        qr_ref[h] = (qr + jnp.dot(cq, wrs_ref[h], preferred_element_type=F32) * sin).astype(BF16)


def _q_sample(x, wc, wm, g, wn, wr, wrs, wuk_t, cos, sin):
    m, d = x.shape
    heads = wn.shape[0]
    rk = wuk_t.shape[2]
    e = wr.shape[2]
    full = lambda arr: pl.BlockSpec(arr.shape, lambda i: (0,) * arr.ndim)
    return pl.pallas_call(
        _q_sample_kernel,
        grid=(1,),
        in_specs=[full(a) for a in (x, wc, wm, g, wn, wr, wrs, wuk_t, cos, sin)],
        out_specs=[pl.BlockSpec((heads, m, rk), lambda i: (0, 0, 0)),
                   pl.BlockSpec((heads, m, e), lambda i: (0, 0, 0)),
                   pl.BlockSpec((m, wm.shape[1]), lambda i: (0, 0))],
        out_shape=[jax.ShapeDtypeStruct((heads, m, rk), BF16), jax.ShapeDtypeStruct((heads, m, e), BF16),
                   jax.ShapeDtypeStruct((m, wm.shape[1]), F32)],
        compiler_params=_cparams(("arbitrary",)),
        name="q_sample",
    )(x, wc, wm, g, wn, wr, wrs, wuk_t, cos, sin)


def _mla_sample_kernel(pt_ref, ql_ref, qr_ref, cn_ref, kn_ref, cache_c, cache_kr, o_ref,
                       kv_buf, kr_buf, s_ref, sem, *, n_pages, new_len, scale, chunk):
    b = pl.program_id(0)
    cur = b % 2
    page = cache_c.shape[1]
    past = kv_buf.shape[1]

    def for_each_page(req, buf, act):
        def body(p, c):
            src = pt_ref[req * n_pages + p]
            at = pl.multiple_of(p * page, page)
            act(pltpu.make_async_copy(cache_c.at[src], kv_buf.at[buf, pl.ds(at, page)], sem.at[buf]))
            act(pltpu.make_async_copy(cache_kr.at[src], kr_buf.at[buf, :, pl.ds(at, page)], sem.at[buf]))
            return c

        lax.fori_loop(0, n_pages, body, 0, unroll=math.gcd(n_pages, 8))

    @pl.when(b == 0)
    def _():
        for_each_page(b, cur, lambda c: c.start())

    @pl.when(b + 1 < pl.num_programs(0))
    def _():
        for_each_page(b + 1, 1 - cur, lambda c: c.start())

    ql = ql_ref[0]
    qr = qr_ref[0]
    nq, nk = ql.shape[0], cn_ref.shape[1]
    cn = cn_ref[0].astype(BF16)
    s_of_row = lax.broadcasted_iota(I32, (nq, nk), 0) % new_len
    j = lax.broadcasted_iota(I32, (nq, nk), 1)
    s_new = lax.dot_general(ql, cn, _NT, preferred_element_type=F32)
    s_new = (s_new + lax.dot_general(qr, kn_ref[0].astype(BF16), _NT, preferred_element_type=F32)) * scale
    s_new = jnp.where(j <= s_of_row, s_new, MASK_VALUE)

    for_each_page(b, cur, lambda c: c.wait())
    for c in range(past // chunk):
        keys = slice(c * chunk, (c + 1) * chunk)
        s = lax.dot_general(ql, kv_buf[cur, keys, :].astype(BF16), _NT, preferred_element_type=F32)
        s = s + jnp.dot(qr, kr_buf[cur, :, keys].astype(BF16), preferred_element_type=F32)
        s_ref[:, keys] = s * scale
    s = s_ref[...]
    m = jnp.maximum(jnp.max(s, axis=-1, keepdims=True), jnp.max(s_new, axis=-1, keepdims=True))
    p = jnp.exp(s - m)
    p_new = jnp.exp(s_new - m)
    l = jnp.sum(p, axis=-1, keepdims=True) + jnp.sum(p_new, axis=-1, keepdims=True)
    acc = jnp.dot(p_new.astype(BF16), cn, preferred_element_type=F32)
    p = p.astype(BF16)
    for c in range(past // chunk):
        keys = slice(c * chunk, (c + 1) * chunk)
        acc = acc + jnp.dot(p[:, keys], kv_buf[cur, keys, :].astype(BF16), preferred_element_type=F32)
    o_ref[0] = acc / l


def _mla_sample(page_table, ql, qr, cache_c, cache_kr_t, c_new, kr_new, new_len, scale):
    bs, n_pages = page_table.shape
    _, nq, rk = ql.shape
    e = qr.shape[2]
    page = cache_c.shape[1]
    past = n_pages * page
    chunk = math.gcd(past, KEY_CHUNK)
    per_b = lambda n, w: pl.BlockSpec((1, n, w), lambda b, pt: (b, 0, 0))
    s8 = c_new.shape[1]
    return pl.pallas_call(
        functools.partial(_mla_sample_kernel, n_pages=n_pages, new_len=new_len, scale=scale, chunk=chunk),
        grid_spec=pltpu.PrefetchScalarGridSpec(
            num_scalar_prefetch=1,
            grid=(bs,),
            in_specs=[per_b(nq, rk), per_b(nq, e), per_b(s8, rk), per_b(s8, e),
                      pl.BlockSpec(memory_space=pl.ANY), pl.BlockSpec(memory_space=pl.ANY)],
            out_specs=per_b(nq, rk),
            scratch_shapes=[pltpu.VMEM((2, past, rk), F32), pltpu.VMEM((2, e, past), F32),
                            pltpu.VMEM((nq, past), F32), pltpu.SemaphoreType.DMA((2,))]),
        out_shape=jax.ShapeDtypeStruct((bs, nq, rk), F32),
        compiler_params=_cparams(("arbitrary",)),
        name="mla_sample",
    )(page_table.reshape(-1), ql, qr, c_new, kr_new, cache_c, cache_kr_t)


def _head_up_kernel(x_ref, w_ref, o_ref):
    o_ref[0] = jnp.dot(x_ref[0].astype(BF16), w_ref[0], preferred_element_type=F32)


def _head_up(lat, wv):
    h, m, rk = lat.shape
    v = wv.shape[2]
    return pl.pallas_call(
        _head_up_kernel,
        grid=(h,),
        in_specs=[pl.BlockSpec((1, m, rk), lambda i: (i, 0, 0)), pl.BlockSpec((1, rk, v), lambda i: (i, 0, 0))],
        out_specs=pl.BlockSpec((1, m, v), lambda i: (i, 0, 0)),
        out_shape=jax.ShapeDtypeStruct((h, m, v), F32),
        compiler_params=_cparams(("parallel",)),
        name="head_up",
    )(lat, wv)


def _head_slots(w, heads, width):
    k = w.shape[0]
    w = w.reshape(k, heads, width)
    return jnp.pad(w, ((0, 0), (0, 0), (0, LANES - width))).reshape(k, heads * LANES)


def kernel(x_prompt, x_sample, cache_kv_latent, cache_k_rope, cache_mem_k, cache_mem_v, state_conv, page_table, mem_prompt, w_in_a, conv_w, w_out_a, w_in_b, g_q, w_uq, w_out_b, w_kv_down, g_kv, w_uk, w_uv, w_mem_k, w_mem_v, ln1_g, ln1_b, ln2_g, ln2_b, w_router, b_router, w_gate, b_gate, w_up, b_up, w_down, b_down):
    bp, sp, d = x_prompt.shape
    bs, ss, _ = x_sample.shape
    depth = ln1_g.shape[0]
    n_a = w_in_a.shape[0]
    d_conv = conv_w.shape[2]
    assert conv_w.shape[1] == 3
    n_mem = mem_prompt.shape[1]
    mem_heads = cache_mem_k.shape[3]
    mem_w = mem_heads * cache_mem_k.shape[4]
    kv_rank, heads, nope = w_uk.shape
    v_dim = w_uv.shape[2]
    rope_dim = cache_k_rope.shape[2]
    q_rank = g_q.shape[1]
    n_past = page_table.shape[1] * cache_kv_latent.shape[1]
    alpha = (2 * depth) ** 0.25
    scale = (nope + rope_dim) ** -0.5
    half = rope_dim // 2

    xp = x_prompt.reshape(bp * sp, d)
    xs = x_sample.reshape(bs * ss, d)

    w_mem = jnp.concatenate([w_mem_k[l] for l in range(depth)] + [w_mem_v[l] for l in range(depth)],
                            axis=1).astype(BF16)
    mem_k_flat, mem_v_flat = _mem_kv(mem_prompt.reshape(bp * n_mem, d), w_mem, depth)
    mem_k_p = mem_k_flat.reshape(depth, bp, n_mem, mem_w)
    mem_v_p = mem_v_flat.reshape(depth, bp, n_mem, mem_w)
    mem_k_s = cache_mem_k.reshape(depth, bs, n_mem, mem_w)
    mem_v_s = cache_mem_v.reshape(depth, bs, n_mem, mem_w)

    conv_p, conv_s = [], []
    c_p = kr_p = c_s = kr_s = None
    k_p = v_p = None
    tables = None

    for l in range(depth):
        g1, b1 = ln1_g[l].reshape(1, d), ln1_b[l].reshape(1, d)
        wt = w_router[l].T
        wt_hi = wt.astype(BF16)
        wt2 = jnp.concatenate([wt_hi, (wt - wt_hi.astype(F32)).astype(BF16)], axis=0)
        ln_route = functools.partial(_out_ln_route, g=g1, b=b1, wt=wt2, br=b_router[l].reshape(-1, 1),
                                     alpha=alpha)
        if l < n_a:
            w_in = w_in_a[l].astype(BF16)
            w_out = w_out_a[l].astype(BF16)
            splits = [d_conv, d_conv, d_conv, mem_w]
            y, qm, st = _proj_conv(xp, w_in, jnp.zeros((bp, 2, d_conv), F32), conv_w[l], bp)
            conv_p.append(st)
            ym = _mem_attn(qm.reshape(bp, sp, mem_w), mem_k_p, mem_v_p, l, mem_heads).reshape(bp * sp, mem_w)
            xp, *route_p = ln_route(y, ym, xp, w_out[:d_conv], w_out[d_conv:])

            gb, gc, hv, qm = _proj(xs, w_in, splits, "in_proj_a")
            r3 = lambda a: a.reshape(bs, ss, d_conv)
            y, st = _conv_short(r3(gb), r3(gc), r3(hv), state_conv[l], conv_w[l])
            conv_s.append(st)
            ym = _mem_attn(qm.reshape(bs, ss, mem_w), mem_k_s, mem_v_s, l, mem_heads).reshape(bs * ss, mem_w)
            xs, *route_s = ln_route(y.reshape(bs * ss, d_conv), ym, xs, w_out[:d_conv], w_out[d_conv:])
        else:
            j = l - n_a
            if tables is None:
                cos_p, sin_p = _rope_table(0, sp, rope_dim)
                cos_s, sin_s = _rope_table(n_past, ss, rope_dim)
                cos_s, sin_s = jnp.tile(cos_s, (bs, 1)), jnp.tile(sin_s, (bs, 1))
                pad = LANES - nope - rope_dim
                c2 = scale * math.log2(math.e)
                cos_slot = c2 * jnp.concatenate([jnp.ones((sp, nope), F32), cos_p, jnp.zeros((sp, pad), F32)], axis=1)
                sin_slot = c2 * jnp.concatenate([jnp.zeros((sp, nope), F32), sin_p, jnp.zeros((sp, pad), F32)], axis=1)
                tables = True
                wc = w_kv_down[:, :kv_rank].astype(BF16)
                wr = w_kv_down[:, kv_rank:]
                wrs = _swap_halves_signed(wr).astype(BF16)
                wr = wr.astype(BF16)
                gk = g_kv.reshape(1, kv_rank)
                c_p, kr_p = _shared_kv(xp, wc, wr, wrs, gk, cos_p, sin_p)
                c_s, kr_s = _shared_kv(xs, wc, wr, wrs, gk, cos_s, sin_s)
                wk_pad = _head_slots(w_uk.reshape(kv_rank, heads * nope), heads, nope).astype(BF16)
                rep = jnp.pad(jnp.eye(rope_dim, dtype=F32), ((0, 0), (nope, pad)))
                rep = jnp.tile(rep, (1, heads)).astype(BF16)
                k_p, v_p = _kv_up(c_p, kr_p, wk_pad, rep, w_uv.reshape(kv_rank, heads * v_dim).astype(BF16))
                pad_rows = -ss % SUBLANES
                c_new = jnp.pad(c_s.reshape(bs, ss, kv_rank), ((0, 0), (0, pad_rows), (0, 0)))
                kr_new = jnp.pad(kr_s.reshape(bs, ss, rope_dim), ((0, 0), (0, pad_rows), (0, 0)))
                cache_kr_t = jnp.transpose(cache_k_rope, (0, 2, 1))
                wuk_t = jnp.transpose(w_uk, (1, 2, 0)).astype(BF16)
                wuv_h = jnp.transpose(w_uv, (1, 0, 2)).astype(BF16)

            w_in = w_in_b[j]
            wcq = w_in[:, :q_rank].astype(BF16)
            wqm = w_in[:, q_rank:].astype(BF16)
            gq = g_q[j].reshape(1, q_rank)
            wq = w_uq[j].reshape(q_rank, heads, nope + rope_dim)
            wq_nope = wq[:, :, :nope]
            wq_rope = wq[:, :, nope:]
            wq_rope_sw = _swap_halves_signed(wq_rope)
            w_out = w_out_b[j].astype(BF16)
            hv_w = heads * v_dim

            slot = lambda a, b: _head_slots(jnp.concatenate([a, b], axis=2).reshape(q_rank, -1), heads,
                                            nope + rope_dim).astype(BF16)
            wq_pad = slot(wq_nope, wq_rope)
            wqs_pad = slot(jnp.zeros_like(wq_nope), wq_rope_sw)
            q, qm = _q_prompt(xp, wcq, wqm, gq, wq_pad, wqs_pad, cos_slot, sin_slot)
            att = _mla_prompt(q, k_p, v_p, bp)
            ym = _mem_attn(qm.reshape(bp, sp, mem_w), mem_k_p, mem_v_p, l, mem_heads).reshape(bp * sp, mem_w)
            xp, *route_p = ln_route(att, ym, xp, w_out[:hv_w], w_out[hv_w:])

            th = lambda a: jnp.transpose(a, (1, 0, 2)).astype(BF16)
            ql, qr, qm = _q_sample(xs, wcq, wqm, gq, th(wq_nope), th(wq_rope), th(wq_rope_sw), wuk_t,
                                   cos_s, sin_s)
            per_req = lambda a: jnp.transpose(a.reshape(heads, bs, ss, -1), (1, 0, 2, 3)).reshape(bs, heads * ss, -1)
            lat = _mla_sample(page_table, per_req(ql), per_req(qr), cache_kv_latent, cache_kr_t,
                              c_new, kr_new, ss, scale)
            lat = jnp.transpose(lat.reshape(bs, heads, ss, kv_rank), (1, 0, 2, 3)).reshape(heads, bs * ss, kv_rank)
            att = jnp.transpose(_head_up(lat, wuv_h), (1, 0, 2)).reshape(bs * ss, hv_w)
            ym = _mem_attn(qm.reshape(bs, ss, mem_w), mem_k_s, mem_v_s, l, mem_heads).reshape(bs * ss, mem_w)
            xs, *route_s = ln_route(att, ym, xs, w_out[:hv_w], w_out[hv_w:])

        xp, xs = _moe_ln([xp, xs], [route_p, route_s], l, w_gate, b_gate, w_up, b_up, w_down, b_down,
                         ln2_g[l].reshape(1, d), ln2_b[l].reshape(1, d), alpha)

    ne_shape = (depth, bp, n_mem, mem_heads, mem_w // mem_heads)
    return (xp.reshape(bp, sp, d), xs.reshape(bs, ss, d),
            c_p.reshape(bp, sp, kv_rank), kr_p.reshape(bp, sp, rope_dim),
            c_s.reshape(bs, ss, kv_rank), kr_s.reshape(bs, ss, rope_dim),
            mem_k_p.reshape(ne_shape), mem_v_p.reshape(ne_shape),
            jnp.stack(conv_p), jnp.stack(conv_s))
```

```python
import functools
import math

import jax
import jax.numpy as jnp
from jax import lax
from jax.experimental import pallas as pl
from jax.experimental.pallas import tpu as pltpu

F32 = jnp.float32
BF16 = jnp.bfloat16
I32 = jnp.int32
U32 = jnp.uint32

TOP_K = 4
SWIGLU_LIMIT = 7.0
SWIGLU_ALPHA = 1.702
ROPE_THETA = 10000.0
LN_EPS = 1e-5
RMS_EPS = 1e-6
MASK_VALUE = -1e30

LANES = 128
SUBLANES = 8
VMEM_LIMIT = 56 * 1024 * 1024

ROW_TILE = 512
MOE_TILE = 512
EXPERT_ROWS = 512
KEY_CHUNK = 2048
MEM_ATTN_ROWS = 32
ATTN_Q_TILE = 512
ATTN_K_TILE = 512
NUM_EXPERTS = 32

_NT = (((1,), (1,)), ((), ()))


def _cparams(sem):
    return pltpu.CompilerParams(dimension_semantics=sem, vmem_limit_bytes=VMEM_LIMIT)


def _tile(n, pref):
    t = min(n, pref)
    assert n % t == 0, (n, pref)
    return t


def _layer_norm(z, g, b):
    mu = jnp.mean(z, axis=-1, keepdims=True)
    zc = z - mu
    var = jnp.mean(zc * zc, axis=-1, keepdims=True)
    return zc * lax.rsqrt(var + LN_EPS) * g + b


def _rms_norm(x, g):
    return x * lax.rsqrt(jnp.mean(x * x, axis=-1, keepdims=True) + RMS_EPS) * g


def _proj_kernel(x_ref, w_ref, *o_refs):
    h = jnp.dot(x_ref[...].astype(BF16), w_ref[...], preferred_element_type=F32)
    off = 0
    for o_ref in o_refs:
        n = o_ref.shape[-1]
        o_ref[...] = h[:, off:off + n].astype(o_ref.dtype)
        off += n


def _proj(x, w, splits, name):
    m, k = x.shape
    n = w.shape[1]
    assert sum(splits) == n
    tm = _tile(m, ROW_TILE)
    return pl.pallas_call(
        _proj_kernel,
        grid=(m // tm,),
        in_specs=[pl.BlockSpec((tm, k), lambda i: (i, 0)), pl.BlockSpec((k, n), lambda i: (0, 0))],
        out_specs=[pl.BlockSpec((tm, s), lambda i: (i, 0)) for s in splits],
        out_shape=[jax.ShapeDtypeStruct((m, s), F32) for s in splits],
        compiler_params=_cparams(("parallel",)),
        name=name,
    )(x, w)


def _memkv_kernel(x_ref, w_ref, k_ref, v_ref):
    depth, _, f = k_ref.shape
    h = jnp.dot(x_ref[...].astype(BF16), w_ref[...], preferred_element_type=F32)
    for l in range(depth):
        k_ref[l] = h[:, l * f:(l + 1) * f]
        v_ref[l] = h[:, (depth + l) * f:(depth + l + 1) * f]


def _mem_kv(mem, w_cat, depth):
    r, d = mem.shape
    f = w_cat.shape[1] // (2 * depth)
    tm = _tile(r, ROW_TILE)
    shp = jax.ShapeDtypeStruct((depth, r, f), F32)
    spec = pl.BlockSpec((depth, tm, f), lambda i: (0, i, 0))
    return pl.pallas_call(
        _memkv_kernel,
        grid=(r // tm,),
        in_specs=[pl.BlockSpec((tm, d), lambda i: (i, 0)), pl.BlockSpec(w_cat.shape, lambda i: (0, 0))],
        out_specs=[spec, spec],
        out_shape=[shp, shp],
        compiler_params=_cparams(("parallel",)),
        name="mem_kv",
    )(mem, w_cat)


def _proj_conv_kernel(x_ref, w_ref, pre_ref, cw_ref, y_ref, qm_ref, st_ref, carry):
    j = pl.program_id(1)

    @pl.when(j == 0)
    def _():
        carry[0:2, :] = pre_ref[0]

    c = y_ref.shape[1]
    h = jnp.dot(x_ref[...].astype(BF16), w_ref[...], preferred_element_type=F32)
    qm_ref[...] = h[:, 3 * c:].astype(qm_ref.dtype)
    u = h[:, c:2 * c] * h[:, 2 * c:3 * c]
    ts = u.shape[0]
    c0 = carry[0:1, :]
    c1 = carry[1:2, :]
    row = lax.broadcasted_iota(I32, u.shape, 0)
    u1 = jnp.where(row == 0, c1, pltpu.roll(u, 1, 0))
    u2 = jnp.where(row == 0, c0, jnp.where(row == 1, c1, pltpu.roll(u, 2, 0)))
    w = cw_ref[...]
    y_ref[...] = (h[:, :c] * (w[0:1] * u2 + w[1:2] * u1 + w[2:3] * u)).astype(y_ref.dtype)
    last = u[ts - 2:ts, :]
    carry[0:2, :] = last
    st_ref[0] = last


def _proj_conv(x, w_in, prefix, cw, batch):
    m, d = x.shape
    c = cw.shape[1]
    n = w_in.shape[1]
    s = m // batch
    ts = _tile(s, ROW_TILE)
    nt = s // ts
    row = lambda width: pl.BlockSpec((ts, width), lambda b, j: (b * nt + j, 0))
    state = pl.BlockSpec((1, 2, c), lambda b, j: (b, 0, 0))
    return pl.pallas_call(
        _proj_conv_kernel,
        grid=(batch, nt),
        in_specs=[row(d), pl.BlockSpec((d, n), lambda b, j: (0, 0)), state, pl.BlockSpec((3, c), lambda b, j: (0, 0))],
        out_specs=[row(c), row(n - 3 * c), state],
        out_shape=[jax.ShapeDtypeStruct((m, c), BF16), jax.ShapeDtypeStruct((m, n - 3 * c), BF16),
                   jax.ShapeDtypeStruct((batch, 2, c), F32)],
        scratch_shapes=[pltpu.VMEM((SUBLANES, c), F32)],
        compiler_params=_cparams(("parallel", "arbitrary")),
        name="proj_conv",
    )(x, w_in, prefix, cw)


def _conv_short_kernel(gb_ref, gc_ref, hv_ref, pre_ref, w_ref, y_ref, st_ref):
    s_len = gb_ref.shape[1]
    w = w_ref[...]

    def u(k):
        if k < 0:
            return pre_ref[:, 2 + k, :]
        return gc_ref[:, k, :] * hv_ref[:, k, :]

    for s in range(s_len):
        y_ref[:, s, :] = gb_ref[:, s, :] * (w[0:1] * u(s - 2) + w[1:2] * u(s - 1) + w[2:3] * u(s))
    for k in range(2):
        st_ref[:, k, :] = u(s_len - 2 + k)


def _conv_short(gb, gc, hv, prefix, w):
    batch, s, c = gb.shape
    full = lambda shp: pl.BlockSpec(shp, lambda i: (0,) * len(shp))
    return pl.pallas_call(
        _conv_short_kernel,
        grid=(1,),
        in_specs=[full(gb.shape)] * 3 + [full(prefix.shape), full(w.shape)],
        out_specs=[full(gb.shape), full(prefix.shape)],
        out_shape=[jax.ShapeDtypeStruct(gb.shape, F32), jax.ShapeDtypeStruct(prefix.shape, F32)],
        compiler_params=_cparams(("arbitrary",)),
        name="conv_short",
    )(gb, gc, hv, prefix, w)


def _mem_attn_kernel(q_ref, k_ref, v_ref, o_ref, *, heads):
    _, tq, w = q_ref.shape
    hd = w // heads
    lane = lax.broadcasted_iota(I32, (tq, w), 1)
    own = [(lane >= h * hd) & (lane < (h + 1) * hd) for h in range(heads)]
    for i in range(q_ref.shape[0]):
        q = q_ref[i].astype(F32)
        q_heads = jnp.concatenate([jnp.where(m, q, 0.0) for m in own], axis=0).astype(BF16)
        k = k_ref[0, i].astype(BF16)
        v = v_ref[0, i].astype(BF16)
        s = lax.dot_general(q_heads, k, _NT, preferred_element_type=F32) * (hd ** -0.5)
        m = jnp.max(s, axis=-1, keepdims=True)
        p = jnp.exp(s - m)
        l = jnp.sum(p, axis=-1, keepdims=True)
        pv = jnp.dot(p.astype(BF16), v, preferred_element_type=F32) / l
        out = jnp.zeros((tq, w), F32)
        for h in range(heads):
            out = jnp.where(own[h], pv[h * tq:(h + 1) * tq], out)
        o_ref[i] = out.astype(o_ref.dtype)


def _mem_attn(q, mk, mv, layer, heads):
    b, s, w = q.shape
    nm = mk.shape[2]
    tq = _tile(s, ROW_TILE)
    bb = math.gcd(b, max(1, MEM_ATTN_ROWS // tq))
    return pl.pallas_call(
        functools.partial(_mem_attn_kernel, heads=heads),
        grid=(b // bb, s // tq),
        in_specs=[pl.BlockSpec((bb, tq, w), lambda i, j: (i, j, 0)),
                  pl.BlockSpec((1, bb, nm, w), lambda i, j: (layer, i, 0, 0)),
                  pl.BlockSpec((1, bb, nm, w), lambda i, j: (layer, i, 0, 0))],
        out_specs=pl.BlockSpec((bb, tq, w), lambda i, j: (i, j, 0)),
        out_shape=jax.ShapeDtypeStruct((b, s, w), BF16),
        compiler_params=_cparams(("parallel", "parallel")),
        name="mem_attn",
    )(q, mk, mv)


def _out_ln_route_kernel(a_ref, m_ref, x_ref, wa_ref, wm_ref, g_ref, b_ref, wt_ref, br_ref,
                         o_ref, pos_ref, gate_ref, cnt_ref, *, alpha):
    o = jnp.dot(a_ref[...].astype(BF16), wa_ref[...], preferred_element_type=F32)
    o = o + jnp.dot(m_ref[...].astype(BF16), wm_ref[...], preferred_element_type=F32)
    x = _layer_norm(alpha * x_ref[...] + o, g_ref[...], b_ref[...])
    o_ref[...] = x
    _route(x, wt_ref[...], br_ref[...], pos_ref, gate_ref, cnt_ref)


def _out_ln_route(a, mem, x, wa, wm, g, b, wt, br, alpha):
    m, d = x.shape
    ne = br.shape[0]
    tt = _tile(m, MOE_TILE)
    nt = m // tt
    row = lambda n: pl.BlockSpec((tt, n), lambda i: (i, 0))
    full = lambda arr: pl.BlockSpec(arr.shape, lambda i: (0, 0))
    return pl.pallas_call(
        functools.partial(_out_ln_route_kernel, alpha=alpha),
        grid=(nt,),
        in_specs=[row(a.shape[1]), row(mem.shape[1]), row(d), full(wa), full(wm), full(g), full(b), full(wt), full(br)],
        out_specs=[row(d),
                   pl.BlockSpec((TOP_K, tt), lambda i: (0, i)),
                   pl.BlockSpec((TOP_K, tt), lambda i: (0, i)),
                   pl.BlockSpec((1, ne, 1), lambda i: (i, 0, 0))],
        out_shape=[jax.ShapeDtypeStruct((m, d), F32),
                   jax.ShapeDtypeStruct((TOP_K, m), I32),
                   jax.ShapeDtypeStruct((TOP_K, m), F32),
                   jax.ShapeDtypeStruct((nt, ne, 1), I32)],
        compiler_params=_cparams(("parallel",)),
        name="out_ln_route",
    )(a, mem, x, wa, wm, g, b, wt, br)


def _route(x, wt2, bias, pos_ref, gate_ref, cnt_ref):
    ne = wt2.shape[0] // 2
    x_hi = x.astype(BF16)
    x_lo = (x - x_hi.astype(F32)).astype(BF16)
    by_hi = lax.dot_general(wt2, x_hi, _NT, preferred_element_type=F32)
    logits = (by_hi[:ne] + by_hi[ne:]) + lax.dot_general(wt2[:ne], x_lo, _NT, preferred_element_type=F32) + bias
    tt = logits.shape[1]
    eidx = lax.broadcasted_iota(I32, (ne, tt), 0)
    sels, vals = [], []
    l = logits
    for _ in range(TOP_K):
        m = jnp.max(l, axis=0, keepdims=True)
        first = jnp.min(jnp.where(l == m, eidx, ne), axis=0, keepdims=True)
        sel = eidx == first
        sels.append(sel)
        vals.append(m)
        l = jnp.where(sel, -jnp.inf, l)
    ex = [jnp.exp(v - vals[0]) for v in vals]
    den = ex[0]
    for e in ex[1:]:
        den = den + e
    gate_ref[...] = jnp.concatenate([e / den for e in ex], axis=0)

    chosen = jnp.zeros((ne, tt), F32)
    for sel in sels:
        chosen = jnp.where(sel, 1.0, chosen)
    cnt = jnp.sum(chosen, axis=1, keepdims=True)
    cnt_ref[0] = cnt.astype(I32)
    cnt8 = jnp.floor((cnt + (SUBLANES - 1)) * (1.0 / SUBLANES)) * SUBLANES
    er = lax.broadcasted_iota(I32, (ne, ne), 0)
    ec = lax.broadcasted_iota(I32, (ne, ne), 1)
    below = jnp.where(ec < er, 1.0, 0.0)
    seg_start = jnp.dot(below, jnp.broadcast_to(cnt8, (ne, LANES)), precision=lax.Precision.HIGHEST,
                        preferred_element_type=F32)[:, 0:1]
    tr = lax.broadcasted_iota(I32, (tt, tt), 0)
    tc = lax.broadcasted_iota(I32, (tt, tt), 1)
    earlier = jnp.where(tr < tc, 1.0, 0.0).astype(BF16)
    rank = jnp.dot(chosen.astype(BF16), earlier, preferred_element_type=F32)
    slot = seg_start + rank
    pos_ref[...] = jnp.concatenate(
        [jnp.sum(jnp.where(sel, slot, 0.0), axis=0, keepdims=True) for sel in sels], axis=0).astype(I32)


def _sorted_rows(tt):
    return TOP_K * tt + NUM_EXPERTS * SUBLANES


def _for_each_segment(n_ref, local_ref, global_ref, tile, ne, fn, queues=1):
    def body(i, c):
        for q in range(queues):
            k = tile * ne + i * queues + q
            n = pl.multiple_of(n_ref[k], SUBLANES)
            pl.when(n > 0)(functools.partial(
                fn, n, pl.multiple_of(local_ref[k], SUBLANES), pl.multiple_of(global_ref[k], SUBLANES), q))
        return c

    lax.fori_loop(0, ne // queues, body, 0)


def _dispatch_kernel(n_ref, local_ref, global_ref, fill_n_ref, fill_at_ref, *refs, ne, group_tiles):
    xs_ref, buf, zeros, sem = refs[-4:]
    step = pl.program_id(0)
    last = pl.num_programs(0) - 1
    tile = step
    cur = step % 2

    def sort_tile(x_ref, slot_ref):
        x = x_ref[...].astype(BF16)
        slot = slot_ref[...]
        rows = _sorted_rows(x.shape[0])
        r = lax.broadcasted_iota(I32, (rows, x.shape[0]), 0)
        onehot = jnp.zeros(r.shape, F32)
        for k in range(TOP_K):
            onehot = jnp.where(slot[k:k + 1, :] == r, 1.0, onehot)
        buf[cur, :rows] = jnp.dot(onehot.astype(BF16), x, preferred_element_type=F32)

    first = 0
    for g, nt in enumerate(group_tiles):
        pl.when((step >= first) & (step < first + nt))(
            functools.partial(sort_tile, refs[2 * g], refs[2 * g + 1]))
        first += nt

    def copy(b):
        return lambda n, lo, go: pltpu.make_async_copy(buf.at[b, pl.ds(lo, n)], xs_ref.at[pl.ds(go, n)], sem.at[b])

    _for_each_segment(n_ref, local_ref, global_ref, tile, ne,
                      lambda n, lo, go, q: copy(cur)(n, lo, go).start(priority=q), queues=2)

    def fill_copies(act):
        def body(k, c):
            n = pl.multiple_of(fill_n_ref[k], SUBLANES)
            at = pl.multiple_of(fill_at_ref[k], SUBLANES)

            @pl.when(n > 0)
            def _():
                act(pltpu.make_async_copy(zeros.at[pl.ds(0, n)], xs_ref.at[pl.ds(at, n)], sem.at[2]))
            return c

        lax.fori_loop(0, fill_n_ref.shape[0], body, 0)

    @pl.when(step == 0)
    def _():
        zeros[...] = jnp.zeros(zeros.shape, F32)
        fill_copies(lambda c: c.start())

    @pl.when(step > 0)
    def _():
        _for_each_segment(n_ref, local_ref, global_ref, tile - 1, ne,
                          lambda n, lo, go, q: copy(1 - cur)(n, lo, go).wait())

    @pl.when(step == last)
    def _():
        _for_each_segment(n_ref, local_ref, global_ref, tile, ne, lambda n, lo, go, q: copy(cur)(n, lo, go).wait())
        fill_copies(lambda c: c.wait())


def _dispatch(seg, fills, groups, tiles, slots, rows_total):
    d = groups[0].shape[1]
    group_tiles = [x.shape[0] // tt for x, tt in zip(groups, tiles)]
    in_specs, args, first = [], [], 0
    for x, tt, slot, nt in zip(groups, tiles, slots, group_tiles):
        own = lambda i, first=first, nt=nt: jnp.clip(i - first, 0, nt - 1)
        in_specs += [pl.BlockSpec((tt, d), lambda i, *_, own=own: (own(i), 0)),
                     pl.BlockSpec((TOP_K, tt), lambda i, *_, own=own: (0, own(i)))]
        args += [x, slot]
        first += nt
    return pl.pallas_call(
        functools.partial(_dispatch_kernel, ne=NUM_EXPERTS, group_tiles=tuple(group_tiles)),
        grid_spec=pltpu.PrefetchScalarGridSpec(
            num_scalar_prefetch=5,
            grid=(sum(group_tiles),),
            in_specs=in_specs,
            out_specs=pl.BlockSpec(memory_space=pl.ANY),
            scratch_shapes=[pltpu.VMEM((2, _sorted_rows(max(tiles)), d), F32), pltpu.VMEM((EXPERT_ROWS, d), F32),
                            pltpu.SemaphoreType.DMA((3,))]),
        out_shape=jax.ShapeDtypeStruct((rows_total, d), F32),
        compiler_params=_cparams(("arbitrary",)),
        name="moe_dispatch",
    )(*seg, *fills, *args)


def _expert_kernel(be_ref, nu_ref, nxt_ref, xs_ref, bg_ref, bu_ref, bd_ref, wg_hbm, wu_hbm, wd_hbm, ys_ref,
                   wg_f32, wu_f32, wd_f32, wg_bf, wu_bf, wd_bf, sem, *, first_weight):
    half = xs_ref.shape[0] // 2
    h0 = 2 * pl.program_id(0)
    h1 = h0 + 1
    n_halves = nu_ref[0]
    used0 = h0 < n_halves
    used1 = h1 < n_halves
    e0 = be_ref[h0]
    e1 = be_ref[h1]
    first0 = used0 & ((h0 == 0) | (e0 != be_ref[jnp.maximum(h0 - 1, 0)]))
    first1 = used1 & (e1 != e0)

    def weight_copies(e, act):
        w = first_weight + e
        act(pltpu.make_async_copy(wg_hbm.at[w], wg_f32, sem.at[0]))
        act(pltpu.make_async_copy(wu_hbm.at[w], wu_f32, sem.at[1]))
        act(pltpu.make_async_copy(wd_hbm.at[w], wd_f32, sem.at[2]))

    def stage(h):
        @pl.when(h == 0)
        def _():
            weight_copies(be_ref[h], lambda c: c.start())

        weight_copies(be_ref[h], lambda c: c.wait())
        wg_bf[...] = wg_f32[...].astype(BF16)
        wu_bf[...] = wu_f32[...].astype(BF16)
        wd_bf[...] = wd_f32[...].astype(BF16)

        @pl.when(nxt_ref[h] >= 0)
        def _():
            weight_copies(nxt_ref[h], lambda c: c.start())

    def run(rows, e):
        w = pl.ds(first_weight + e, 1)
        x = xs_ref[rows, :].astype(BF16)
        g = jnp.dot(x, wg_bf[...], preferred_element_type=F32) + bg_ref[w, :]
        u = jnp.dot(x, wu_bf[...], preferred_element_type=F32) + bu_ref[w, :]
        g = jnp.minimum(g, SWIGLU_LIMIT)
        u = jnp.clip(u, -SWIGLU_LIMIT, SWIGLU_LIMIT)
        h = (u + 1.0) * (g * jax.nn.sigmoid(SWIGLU_ALPHA * g))
        ys_ref[rows, :] = jnp.dot(h.astype(BF16), wd_bf[...], preferred_element_type=F32) + bd_ref[w, :]

    def clear(rows):
        ys_ref[rows, :] = jnp.zeros((half, ys_ref.shape[1]), F32)

    lower, upper = slice(0, half), slice(half, 2 * half)
    whole = used1 & jnp.logical_not(first1)
    pl.when(first0)(functools.partial(stage, h0))
    pl.when(whole)(functools.partial(run, slice(None), e0))
    pl.when(used0 & jnp.logical_not(whole))(functools.partial(run, lower, e0))

    @pl.when(first1)
    def _():
        stage(h1)
        run(upper, e1)

    pl.when(jnp.logical_not(used0))(functools.partial(clear, lower))
    pl.when(jnp.logical_not(used1))(functools.partial(clear, upper))


def _experts(half_expert, n_halves, next_expert, xs, layer, wg, bg, wu, bu, wd, bd):
    rows = xs.shape[0]
    depth, ne, d, f = wg.shape
    nb = rows // EXPERT_ROWS
    blk = lambda b, be, nu, nxt: (jnp.clip(b, 0, (nu[0] + 1) // 2 - 1), 0)
    stack = lambda w: w.reshape((depth * ne,) + w.shape[2:])
    bias = lambda v: pl.BlockSpec((depth * ne, v.shape[2]), lambda b, be, nu, nxt: (0, 0))
    anywhere = pl.BlockSpec(memory_space=pl.ANY)
    return pl.pallas_call(
        functools.partial(_expert_kernel, first_weight=layer * ne),
        grid_spec=pltpu.PrefetchScalarGridSpec(
            num_scalar_prefetch=3,
            grid=(nb,),
            in_specs=[pl.BlockSpec((EXPERT_ROWS, d), blk), bias(bg), bias(bu), bias(bd),
                      anywhere, anywhere, anywhere],
            out_specs=pl.BlockSpec((EXPERT_ROWS, d), lambda b, be, nu, nxt: (b, 0)),
            scratch_shapes=[pltpu.VMEM((d, f), F32), pltpu.VMEM((d, f), F32), pltpu.VMEM((f, d), F32),
                            pltpu.VMEM((d, f), BF16), pltpu.VMEM((d, f), BF16), pltpu.VMEM((f, d), BF16),
                            pltpu.SemaphoreType.DMA((3,))]),
        out_shape=jax.ShapeDtypeStruct((rows, d), F32),
        compiler_params=_cparams(("arbitrary",)),
        name="moe_experts",
    )(half_expert, n_halves, next_expert, xs, stack(bg), stack(bu), stack(bd), stack(wg), stack(wu), stack(wd))


def _combine_kernel(n_ref, local_ref, global_ref, ys_ref, slot_ref, gate_ref, x_ref, g_ref, b_ref, o_ref,
                    buf, sem, *, tile0, ne, alpha):
    step = pl.program_id(0)
    tile = step + tile0
    cur = step % 2

    def copy(b):
        return lambda n, lo, go: pltpu.make_async_copy(ys_ref.at[pl.ds(go, n)], buf.at[b, pl.ds(lo, n)], sem.at[b])

    def fetch(t, b):
        covered = TOP_K * slot_ref.shape[0]
        buf[b, covered:] = jnp.zeros((buf.shape[1] - covered, buf.shape[2]), F32)
        _for_each_segment(n_ref, local_ref, global_ref, t, ne,
                          lambda n, lo, go, q: copy(b)(n, lo, go).start(priority=q), queues=2)

    @pl.when(step == 0)
    def _():
        fetch(tile, cur)

    @pl.when(step < pl.num_programs(0) - 1)
    def _():
        fetch(tile + 1, 1 - cur)

    slot = slot_ref[...]
    gate = gate_ref[...]
    rows = buf.shape[1]
    r = lax.broadcasted_iota(I32, (slot.shape[0], rows), 1)
    weights = jnp.zeros(r.shape, F32)
    for k in range(TOP_K):
        weights = jnp.where(slot[:, k:k + 1] == r, gate[:, k:k + 1], weights)
    _for_each_segment(n_ref, local_ref, global_ref, tile, ne, lambda n, lo, go, q: copy(cur)(n, lo, go).wait())
    y = jnp.dot(weights.astype(BF16), buf[cur].astype(BF16), preferred_element_type=F32)
    o_ref[...] = _layer_norm(alpha * x_ref[...] + y, g_ref[...], b_ref[...])


def _combine(seg, ys, slot_t, gate_t, x, g, b, tile0, tt, alpha):
    m, d = x.shape
    return pl.pallas_call(
        functools.partial(_combine_kernel, tile0=tile0, ne=NUM_EXPERTS, alpha=alpha),
        grid_spec=pltpu.PrefetchScalarGridSpec(
            num_scalar_prefetch=3,
            grid=(m // tt,),
            in_specs=[pl.BlockSpec(memory_space=pl.ANY),
                      pl.BlockSpec((tt, TOP_K), lambda i, *_: (i, 0)),
                      pl.BlockSpec((tt, TOP_K), lambda i, *_: (i, 0)),
                      pl.BlockSpec((tt, d), lambda i, *_: (i, 0)),
                      pl.BlockSpec((1, d), lambda i, *_: (0, 0)),
                      pl.BlockSpec((1, d), lambda i, *_: (0, 0))],
            out_specs=pl.BlockSpec((tt, d), lambda i, *_: (i, 0)),
            scratch_shapes=[pltpu.VMEM((2, _sorted_rows(tt), d), F32), pltpu.SemaphoreType.DMA((2,))]),
        out_shape=jax.ShapeDtypeStruct((m, d), F32),
        compiler_params=_cparams(("arbitrary",)),
        name="moe_combine",
    )(*seg, ys, slot_t, gate_t, x, g, b)


def _moe_ln(xs_groups, routed, layer, wg, bg, wu, bu, wd, bd, g, b, alpha):
    ne = wg.shape[1]
    assert ne == NUM_EXPERTS
    d = xs_groups[0].shape[1]
    tiles = [_tile(x.shape[0], MOE_TILE) for x in xs_groups]
    cnt = jnp.concatenate([r[2][:, :, 0] for r in routed], axis=0)
    n_tiles = cnt.shape[0]
    n8 = (cnt + (SUBLANES - 1)) // SUBLANES * SUBLANES
    local = jnp.cumsum(n8, axis=1) - n8
    per_expert = jnp.sum(n8, axis=0)
    half = EXPERT_ROWS // 2
    per_expert_pad = (per_expert + (half - 1)) // half * half
    expert_base = jnp.cumsum(per_expert_pad) - per_expert_pad
    glob = expert_base[None, :] + jnp.cumsum(n8, axis=0) - n8
    seg = tuple(a.reshape(-1).astype(I32) for a in (n8, local, glob))
    max_rows = sum(_sorted_rows(tt) * (x.shape[0] // tt) for x, tt in zip(xs_groups, tiles)) + ne * (half - 1)
    nb = -(-max_rows // EXPERT_ROWS)
    halves_per_expert = per_expert_pad // half
    n_halves = jnp.sum(halves_per_expert).astype(I32).reshape(1)
    ends = jnp.cumsum(halves_per_expert)
    halves = jnp.arange(2 * nb, dtype=I32)
    half_expert = jnp.minimum(jnp.sum((ends[:, None] <= halves[None, :]).astype(I32), axis=0), ne - 1)
    following = ends[half_expert]
    next_expert = jnp.where(following < n_halves[0], half_expert[jnp.minimum(following, 2 * nb - 1)], -1).astype(I32)
    fills = (jnp.concatenate([per_expert_pad - per_expert, jnp.where(halves >= n_halves[0], half, 0)]),
             jnp.concatenate([expert_base + per_expert, halves * half]))
    fills = tuple(a.astype(I32) for a in fills)

    xs = _dispatch(seg, fills, xs_groups, tiles, [r[0] for r in routed], nb * EXPERT_ROWS)
    ys = _experts(half_expert, n_halves, next_expert, xs, layer, wg, bg, wu, bu, wd, bd)
    outs = []
    tile0 = 0
    for x, tt, r in zip(xs_groups, tiles, routed):
        outs.append(_combine(seg, ys, r[0].T, r[1].T, x, g, b, tile0, tt, alpha))
        tile0 += x.shape[0] // tt
    return outs


def _rope_table_kernel(freq_ref, cos_ref, sin_ref, *, start):
    pos = (lax.broadcasted_iota(I32, cos_ref.shape, 0) + start).astype(F32)
    ang = pos * freq_ref[...]
    cos_ref[...] = jnp.cos(ang)
    sin_ref[...] = jnp.sin(ang)


def _rope_table(start, n, rope_dim):
    half = rope_dim // 2
    freqs = ROPE_THETA ** (-jnp.arange(half, dtype=F32) / half)
    freq_row = jnp.concatenate([freqs, freqs]).reshape(1, rope_dim)
    rows = -(-n // SUBLANES) * SUBLANES
    shp = jax.ShapeDtypeStruct((rows, rope_dim), F32)
    full = pl.BlockSpec((rows, rope_dim), lambda i: (0, 0))
    cos, sin = pl.pallas_call(
        functools.partial(_rope_table_kernel, start=start),
        grid=(1,),
        in_specs=[pl.BlockSpec((1, rope_dim), lambda i: (0, 0))],
        out_specs=[full, full],
        out_shape=[shp, shp],
        name="rope_table",
    )(freq_row)
    return cos[:n], sin[:n]


def _swap_halves_signed(w):
    half = w.shape[-1] // 2
    return jnp.concatenate([-w[..., half:], w[..., :half]], axis=-1)


def _kv_kernel(x_ref, wc_ref, wr_ref, wrs_ref, g_ref, cos_ref, sin_ref, c_ref, kr_ref):
    x = x_ref[...].astype(BF16)
    c = jnp.dot(x, wc_ref[...], preferred_element_type=F32)
    c_ref[...] = _rms_norm(c, g_ref[...])
    r = jnp.dot(x, wr_ref[...], preferred_element_type=F32)
    rs = jnp.dot(x, wrs_ref[...], preferred_element_type=F32)
    kr_ref[...] = r * cos_ref[...] + rs * sin_ref[...]


def _shared_kv(x, wc, wr, wrs, g, cos, sin):
    m, d = x.shape
    rk = wc.shape[1]
    e = wr.shape[1]
    tm = _tile(min(m, cos.shape[0]), ROW_TILE)
    period = cos.shape[0] // tm
    full = lambda arr: pl.BlockSpec(arr.shape, lambda i: (0, 0))
    tab = pl.BlockSpec((tm, e), lambda i: (i % period, 0))
    return pl.pallas_call(
        _kv_kernel,
        grid=(m // tm,),
        in_specs=[pl.BlockSpec((tm, d), lambda i: (i, 0)), full(wc), full(wr), full(wrs), full(g), tab, tab],
        out_specs=[pl.BlockSpec((tm, rk), lambda i: (i, 0)), pl.BlockSpec((tm, e), lambda i: (i, 0))],
        out_shape=[jax.ShapeDtypeStruct((m, rk), F32), jax.ShapeDtypeStruct((m, e), F32)],
        compiler_params=_cparams(("parallel",)),
        name="shared_kv",
    )(x, wc, wr, wrs, g, cos, sin)


def _kv_up_kernel(c_ref, kr_ref, wk_ref, rep_ref, wv_ref, k_ref, v_ref):
    c = c_ref[...].astype(BF16)
    k = jnp.dot(c, wk_ref[...], preferred_element_type=F32)
    k = k + jnp.dot(kr_ref[...].astype(BF16), rep_ref[...], preferred_element_type=F32)
    k_ref[...] = k.astype(BF16)
    v_ref[...] = jnp.dot(c, wv_ref[...], preferred_element_type=F32).astype(BF16)


def _kv_up(c, kr, wk_pad, rep, wv):
    m, rk = c.shape
    tm = _tile(m, ROW_TILE)
    full = lambda arr: pl.BlockSpec(arr.shape, lambda i: (0, 0))
    row = lambda n: pl.BlockSpec((tm, n), lambda i: (i, 0))
    return pl.pallas_call(
        _kv_up_kernel,
        grid=(m // tm,),
        in_specs=[row(rk), row(kr.shape[1]), full(wk_pad), full(rep), full(wv)],
        out_specs=[row(wk_pad.shape[1]), row(wv.shape[1])],
        out_shape=[jax.ShapeDtypeStruct((m, wk_pad.shape[1]), BF16), jax.ShapeDtypeStruct((m, wv.shape[1]), BF16)],
        compiler_params=_cparams(("parallel",)),
        name="kv_up",
    )(c, kr, wk_pad, rep, wv)


def _q_prompt_kernel(x_ref, wc_ref, wm_ref, g_ref, wq_ref, wqs_ref, cos_ref, sin_ref, q_ref, qm_ref):
    x = x_ref[...].astype(BF16)
    qm_ref[...] = jnp.dot(x, wm_ref[...], preferred_element_type=F32).astype(BF16)
    cq = _rms_norm(jnp.dot(x, wc_ref[...], preferred_element_type=F32), g_ref[...]).astype(BF16)
    qa = jnp.dot(cq, wq_ref[...], preferred_element_type=F32)
    qb = jnp.dot(cq, wqs_ref[...], preferred_element_type=F32)
    cos = cos_ref[...]
    sin = sin_ref[...]
    for h in range(q_ref.shape[1] // LANES):
        sl = slice(h * LANES, (h + 1) * LANES)
        q_ref[:, sl] = (qa[:, sl] * cos + qb[:, sl] * sin).astype(BF16)


def _q_prompt(x, wc, wm, g, wq_pad, wqs_pad, cos_slot, sin_slot):
    m, d = x.shape
    tm = _tile(min(m, cos_slot.shape[0]), ROW_TILE)
    period = cos_slot.shape[0] // tm
    full = lambda arr: pl.BlockSpec(arr.shape, lambda i: (0, 0))
    tab = pl.BlockSpec((tm, LANES), lambda i: (i % period, 0))
    row = lambda n: pl.BlockSpec((tm, n), lambda i: (i, 0))
    return pl.pallas_call(
        _q_prompt_kernel,
        grid=(m // tm,),
        in_specs=[row(d), full(wc), full(wm), full(g), full(wq_pad), full(wqs_pad), tab, tab],
        out_specs=[row(wq_pad.shape[1]), row(wm.shape[1])],
        out_shape=[jax.ShapeDtypeStruct((m, wq_pad.shape[1]), BF16), jax.ShapeDtypeStruct((m, wm.shape[1]), BF16)],
        compiler_params=_cparams(("parallel",)),
        name="q_prompt",
    )(x, wc, wm, g, wq_pad, wqs_pad, cos_slot, sin_slot)


def _mla_prompt_kernel(q_ref, k_ref, v_ref, o_ref, *, v_dim, tk):
    qi = pl.program_id(2)
    tq = q_ref.shape[0]
    heads = [slice(hh * LANES, (hh + 1) * LANES) for hh in range(2)]
    q = [q_ref[:, sl] for sl in heads]

    def attend(carry, start, visible):
        kb = k_ref[pl.ds(start, tk), :]
        vb = v_ref[pl.ds(start, tk), :]
        out = []
        for hh in range(2):
            m_old, l_old, acc = carry[hh]
            s = lax.dot_general(q[hh], kb[:, heads[hh]], _NT, preferred_element_type=F32)
            if visible is not None:
                s = jnp.where(visible, s, MASK_VALUE)
            m_new = jnp.maximum(m_old, jnp.max(s, axis=-1, keepdims=True))
            a = jnp.exp2(m_old - m_new)
            p = jnp.exp2(s - m_new)
            out.append((m_new, a * l_old + jnp.sum(p, axis=-1, keepdims=True),
                        a * acc + jnp.dot(p.astype(BF16), vb, preferred_element_type=F32)))
        return tuple(out)

    init = tuple((jnp.full((tq, 1), -jnp.inf, F32), jnp.zeros((tq, 1), F32), jnp.zeros((tq, 2 * v_dim), F32))
                 for _ in range(2))
    per_tile = tq // tk
    carry = lax.fori_loop(0, qi * per_tile, lambda ki, c: attend(c, pl.multiple_of(ki * tk, tk), None), init)
    row = lax.broadcasted_iota(I32, (tq, tk), 0)
    col = lax.broadcasted_iota(I32, (tq, tk), 1)
    for j in range(per_tile):
        carry = attend(carry, pl.multiple_of(qi * tq + j * tk, tk), col + j * tk <= row)
    (_, l0, acc0), (_, l1, acc1) = carry
    lane = lax.broadcasted_iota(I32, (tq, 2 * v_dim), 1)
    o_ref[...] = jnp.where(lane < v_dim, acc0 / l0, acc1 / l1).astype(o_ref.dtype)


def _mla_prompt(q, k, v, batch):
    m, hw = q.shape
    heads = hw // LANES
    v_dim = v.shape[1] // heads
    assert heads % 2 == 0 and 2 * v_dim == LANES
    s = m // batch
    tq = _tile(s, ATTN_Q_TILE)
    tk = _tile(tq, ATTN_K_TILE)
    nq = s // tq
    return pl.pallas_call(
        functools.partial(_mla_prompt_kernel, v_dim=v_dim, tk=tk),
        grid=(batch, heads // 2, nq),
        in_specs=[pl.BlockSpec((tq, 2 * LANES), lambda b, h, qi: (b * nq + qi, h)),
                  pl.BlockSpec((s, 2 * LANES), lambda b, h, qi: (b, h)),
                  pl.BlockSpec((s, 2 * v_dim), lambda b, h, qi: (b, h))],
        out_specs=pl.BlockSpec((tq, 2 * v_dim), lambda b, h, qi: (b * nq + qi, h)),
        out_shape=jax.ShapeDtypeStruct((m, heads * v_dim), BF16),
        compiler_params=_cparams(("parallel", "parallel", "arbitrary")),
        name="mla_prompt",
    )(q, k, v)


def _q_sample_kernel(x_ref, wc_ref, wm_ref, g_ref, wn_ref, wr_ref, wrs_ref, wuk_ref, cos_ref, sin_ref,
                     ql_ref, qr_ref, qm_ref):
    x = x_ref[...].astype(BF16)
    qm_ref[...] = jnp.dot(x, wm_ref[...], preferred_element_type=F32)
    cq = _rms_norm(jnp.dot(x, wc_ref[...], preferred_element_type=F32), g_ref[...]).astype(BF16)
    cos = cos_ref[...]
    sin = sin_ref[...]
    for h in range(wn_ref.shape[0]):
        qn = jnp.dot(cq, wn_ref[h], preferred_element_type=F32).astype(BF16)
        ql_ref[h] = jnp.dot(qn, wuk_ref[h], preferred_element_type=F32).astype(BF16)
        qr = jnp.dot(cq, wr_ref[h], preferred_element_type=F32) * cos
        qr_ref[h] = (qr + jnp.dot(cq, wrs_ref[h], preferred_element_type=F32) * sin).astype(BF16)


def _q_sample(x, wc, wm, g, wn, wr, wrs, wuk_t, cos, sin):
    m, d = x.shape
    heads = wn.shape[0]
    rk = wuk_t.shape[2]
    e = wr.shape[2]
    full = lambda arr: pl.BlockSpec(arr.shape, lambda i: (0,) * arr.ndim)
    return pl.pallas_call(
        _q_sample_kernel,
        grid=(1,),
        in_specs=[full(a) for a in (x, wc, wm, g, wn, wr, wrs, wuk_t, cos, sin)],
        out_specs=[pl.BlockSpec((heads, m, rk), lambda i: (0, 0, 0)),
                   pl.BlockSpec((heads, m, e), lambda i: (0, 0, 0)),
                   pl.BlockSpec((m, wm.shape[1]), lambda i: (0, 0))],
        out_shape=[jax.ShapeDtypeStruct((heads, m, rk), BF16), jax.ShapeDtypeStruct((heads, m, e), BF16),
                   jax.ShapeDtypeStruct((m, wm.shape[1]), F32)],
        compiler_params=_cparams(("arbitrary",)),
        name="q_sample",
    )(x, wc, wm, g, wn, wr, wrs, wuk_t, cos, sin)


def _mla_sample_kernel(pt_ref, ql_ref, qr_ref, cn_ref, kn_ref, cache_c, cache_kr, o_ref,
                       kv_buf, kr_buf, s_ref, sem, *, n_pages, new_len, scale, chunk):
    b = pl.program_id(0)
    cur = b % 2
    page = cache_c.shape[1]
    past = kv_buf.shape[1]

    def for_each_page(req, buf, act):
        def body(p, c):
            src = pt_ref[req * n_pages + p]
            at = pl.multiple_of(p * page, page)
            act(pltpu.make_async_copy(cache_c.at[src], kv_buf.at[buf, pl.ds(at, page)], sem.at[buf]))
            act(pltpu.make_async_copy(cache_kr.at[src], kr_buf.at[buf, :, pl.ds(at, page)], sem.at[buf]))
            return c

        lax.fori_loop(0, n_pages, body, 0, unroll=math.gcd(n_pages, 8))

    @pl.when(b == 0)
    def _():
        for_each_page(b, cur, lambda c: c.start())

    @pl.when(b + 1 < pl.num_programs(0))
    def _():
        for_each_page(b + 1, 1 - cur, lambda c: c.start())

    ql = ql_ref[0]
    qr = qr_ref[0]
    nq, nk = ql.shape[0], cn_ref.shape[1]
    cn = cn_ref[0].astype(BF16)
    s_of_row = lax.broadcasted_iota(I32, (nq, nk), 0) % new_len
    j = lax.broadcasted_iota(I32, (nq, nk), 1)
    s_new = lax.dot_general(ql, cn, _NT, preferred_element_type=F32)
    s_new = (s_new + lax.dot_general(qr, kn_ref[0].astype(BF16), _NT, preferred_element_type=F32)) * scale
    s_new = jnp.where(j <= s_of_row, s_new, MASK_VALUE)

    for_each_page(b, cur, lambda c: c.wait())
    for c in range(past // chunk):
        keys = slice(c * chunk, (c + 1) * chunk)
        s = lax.dot_general(ql, kv_buf[cur, keys, :].astype(BF16), _NT, preferred_element_type=F32)
        s = s + jnp.dot(qr, kr_buf[cur, :, keys].astype(BF16), preferred_element_type=F32)
        s_ref[:, keys] = s * scale
    s = s_ref[...]
    m = jnp.maximum(jnp.max(s, axis=-1, keepdims=True), jnp.max(s_new, axis=-1, keepdims=True))
    p = jnp.exp(s - m)
    p_new = jnp.exp(s_new - m)
    l = jnp.sum(p, axis=-1, keepdims=True) + jnp.sum(p_new, axis=-1, keepdims=True)
    acc = jnp.dot(p_new.astype(BF16), cn, preferred_element_type=F32)
    p = p.astype(BF16)
    for c in range(past // chunk):
        keys = slice(c * chunk, (c + 1) * chunk)
        acc = acc + jnp.dot(p[:, keys], kv_buf[cur, keys, :].astype(BF16), preferred_element_type=F32)
    o_ref[0] = acc / l


def _mla_sample(page_table, ql, qr, cache_c, cache_kr_t, c_new, kr_new, new_len, scale):
    bs, n_pages = page_table.shape
    _, nq, rk = ql.shape
    e = qr.shape[2]
    page = cache_c.shape[1]
    past = n_pages * page
    chunk = math.gcd(past, KEY_CHUNK)
    per_b = lambda n, w: pl.BlockSpec((1, n, w), lambda b, pt: (b, 0, 0))
    s8 = c_new.shape[1]
    return pl.pallas_call(
        functools.partial(_mla_sample_kernel, n_pages=n_pages, new_len=new_len, scale=scale, chunk=chunk),
        grid_spec=pltpu.PrefetchScalarGridSpec(
            num_scalar_prefetch=1,
            grid=(bs,),
            in_specs=[per_b(nq, rk), per_b(nq, e), per_b(s8, rk), per_b(s8, e),
                      pl.BlockSpec(memory_space=pl.ANY), pl.BlockSpec(memory_space=pl.ANY)],
            out_specs=per_b(nq, rk),
            scratch_shapes=[pltpu.VMEM((2, past, rk), F32), pltpu.VMEM((2, e, past), F32),
                            pltpu.VMEM((nq, past), F32), pltpu.SemaphoreType.DMA((2,))]),
        out_shape=jax.ShapeDtypeStruct((bs, nq, rk), F32),
        compiler_params=_cparams(("arbitrary",)),
        name="mla_sample",
    )(page_table.reshape(-1), ql, qr, c_new, kr_new, cache_c, cache_kr_t)


def _head_up_kernel(x_ref, w_ref, o_ref):
    o_ref[0] = jnp.dot(x_ref[0].astype(BF16), w_ref[0], preferred_element_type=F32)


def _head_up(lat, wv):
    h, m, rk = lat.shape
    v = wv.shape[2]
    return pl.pallas_call(
        _head_up_kernel,
        grid=(h,),
        in_specs=[pl.BlockSpec((1, m, rk), lambda i: (i, 0, 0)), pl.BlockSpec((1, rk, v), lambda i: (i, 0, 0))],
        out_specs=pl.BlockSpec((1, m, v), lambda i: (i, 0, 0)),
        out_shape=jax.ShapeDtypeStruct((h, m, v), F32),
        compiler_params=_cparams(("parallel",)),
        name="head_up",
    )(lat, wv)


def _head_slots(w, heads, width):
    k = w.shape[0]
    w = w.reshape(k, heads, width)
    return jnp.pad(w, ((0, 0), (0, 0), (0, LANES - width))).reshape(k, heads * LANES)


def kernel(x_prompt, x_sample, cache_kv_latent, cache_k_rope, cache_mem_k, cache_mem_v, state_conv, page_table, mem_prompt, w_in_a, conv_w, w_out_a, w_in_b, g_q, w_uq, w_out_b, w_kv_down, g_kv, w_uk, w_uv, w_mem_k, w_mem_v, ln1_g, ln1_b, ln2_g, ln2_b, w_router, b_router, w_gate, b_gate, w_up, b_up, w_down, b_down):
    bp, sp, d = x_prompt.shape
    bs, ss, _ = x_sample.shape
    depth = ln1_g.shape[0]
    n_a = w_in_a.shape[0]
    d_conv = conv_w.shape[2]
    assert conv_w.shape[1] == 3
    n_mem = mem_prompt.shape[1]
    mem_heads = cache_mem_k.shape[3]
    mem_w = mem_heads * cache_mem_k.shape[4]
    kv_rank, heads, nope = w_uk.shape
    v_dim = w_uv.shape[2]
    rope_dim = cache_k_rope.shape[2]
    q_rank = g_q.shape[1]
    n_past = page_table.shape[1] * cache_kv_latent.shape[1]
    alpha = (2 * depth) ** 0.25
    scale = (nope + rope_dim) ** -0.5
    half = rope_dim // 2

    xp = x_prompt.reshape(bp * sp, d)
    xs = x_sample.reshape(bs * ss, d)

    w_mem = jnp.concatenate([w_mem_k[l] for l in range(depth)] + [w_mem_v[l] for l in range(depth)],
                            axis=1).astype(BF16)
    mem_k_flat, mem_v_flat = _mem_kv(mem_prompt.reshape(bp * n_mem, d), w_mem, depth)
    mem_k_p = mem_k_flat.reshape(depth, bp, n_mem, mem_w)
    mem_v_p = mem_v_flat.reshape(depth, bp, n_mem, mem_w)
    mem_k_s = cache_mem_k.reshape(depth, bs, n_mem, mem_w)
    mem_v_s = cache_mem_v.reshape(depth, bs, n_mem, mem_w)

    conv_p, conv_s = [], []
    c_p = kr_p = c_s = kr_s = None
    k_p = v_p = None
    tables = None

    for l in range(depth):
        g1, b1 = ln1_g[l].reshape(1, d), ln1_b[l].reshape(1, d)
        wt = w_router[l].T
        wt_hi = wt.astype(BF16)
        wt2 = jnp.concatenate([wt_hi, (wt - wt_hi.astype(F32)).astype(BF16)], axis=0)
        ln_route = functools.partial(_out_ln_route, g=g1, b=b1, wt=wt2, br=b_router[l].reshape(-1, 1),
                                     alpha=alpha)
        if l < n_a:
            w_in = w_in_a[l].astype(BF16)
            w_out = w_out_a[l].astype(BF16)
            splits = [d_conv, d_conv, d_conv, mem_w]
            y, qm, st = _proj_conv(xp, w_in, jnp.zeros((bp, 2, d_conv), F32), conv_w[l], bp)
            conv_p.append(st)
            ym = _mem_attn(qm.reshape(bp, sp, mem_w), mem_k_p, mem_v_p, l, mem_heads).reshape(bp * sp, mem_w)
            xp, *route_p = ln_route(y, ym, xp, w_out[:d_conv], w_out[d_conv:])

            gb, gc, hv, qm = _proj(xs, w_in, splits, "in_proj_a")
            r3 = lambda a: a.reshape(bs, ss, d_conv)
            y, st = _conv_short(r3(gb), r3(gc), r3(hv), state_conv[l], conv_w[l])
            conv_s.append(st)
            ym = _mem_attn(qm.reshape(bs, ss, mem_w), mem_k_s, mem_v_s, l, mem_heads).reshape(bs * ss, mem_w)
            xs, *route_s = ln_route(y.reshape(bs * ss, d_conv), ym, xs, w_out[:d_conv], w_out[d_conv:])
        else:
            j = l - n_a
            if tables is None:
                cos_p, sin_p = _rope_table(0, sp, rope_dim)
                cos_s, sin_s = _rope_table(n_past, ss, rope_dim)
                cos_s, sin_s = jnp.tile(cos_s, (bs, 1)), jnp.tile(sin_s, (bs, 1))
                pad = LANES - nope - rope_dim
                c2 = scale * math.log2(math.e)
                cos_slot = c2 * jnp.concatenate([jnp.ones((sp, nope), F32), cos_p, jnp.zeros((sp, pad), F32)], axis=1)
                sin_slot = c2 * jnp.concatenate([jnp.zeros((sp, nope), F32), sin_p, jnp.zeros((sp, pad), F32)], axis=1)
                tables = True
                wc = w_kv_down[:, :kv_rank].astype(BF16)
                wr = w_kv_down[:, kv_rank:]
                wrs = _swap_halves_signed(wr).astype(BF16)
                wr = wr.astype(BF16)
                gk = g_kv.reshape(1, kv_rank)
                c_p, kr_p = _shared_kv(xp, wc, wr, wrs, gk, cos_p, sin_p)
                c_s, kr_s = _shared_kv(xs, wc, wr, wrs, gk, cos_s, sin_s)
                wk_pad = _head_slots(w_uk.reshape(kv_rank, heads * nope), heads, nope).astype(BF16)
                rep = jnp.pad(jnp.eye(rope_dim, dtype=F32), ((0, 0), (nope, pad)))
                rep = jnp.tile(rep, (1, heads)).astype(BF16)
                k_p, v_p = _kv_up(c_p, kr_p, wk_pad, rep, w_uv.reshape(kv_rank, heads * v_dim).astype(BF16))
                pad_rows = -ss % SUBLANES
                c_new = jnp.pad(c_s.reshape(bs, ss, kv_rank), ((0, 0), (0, pad_rows), (0, 0)))
                kr_new = jnp.pad(kr_s.reshape(bs, ss, rope_dim), ((0, 0), (0, pad_rows), (0, 0)))
                cache_kr_t = jnp.transpose(cache_k_rope, (0, 2, 1))
                wuk_t = jnp.transpose(w_uk, (1, 2, 0)).astype(BF16)
                wuv_h = jnp.transpose(w_uv, (1, 0, 2)).astype(BF16)

            w_in = w_in_b[j]
            wcq = w_in[:, :q_rank].astype(BF16)
            wqm = w_in[:, q_rank:].astype(BF16)
            gq = g_q[j].reshape(1, q_rank)
            wq = w_uq[j].reshape(q_rank, heads, nope + rope_dim)
            wq_nope = wq[:, :, :nope]
            wq_rope = wq[:, :, nope:]
            wq_rope_sw = _swap_halves_signed(wq_rope)
            w_out = w_out_b[j].astype(BF16)
            hv_w = heads * v_dim

            slot = lambda a, b: _head_slots(jnp.concatenate([a, b], axis=2).reshape(q_rank, -1), heads,
                                            nope + rope_dim).astype(BF16)
            wq_pad = slot(wq_nope, wq_rope)
            wqs_pad = slot(jnp.zeros_like(wq_nope), wq_rope_sw)
            q, qm = _q_prompt(xp, wcq, wqm, gq, wq_pad, wqs_pad, cos_slot, sin_slot)
            att = _mla_prompt(q, k_p, v_p, bp)
            ym = _mem_attn(qm.reshape(bp, sp, mem_w), mem_k_p, mem_v_p, l, mem_heads).reshape(bp * sp, mem_w)
            xp, *route_p = ln_route(att, ym, xp, w_out[:hv_w], w_out[hv_w:])

            th = lambda a: jnp.transpose(a, (1, 0, 2)).astype(BF16)
            ql, qr, qm = _q_sample(xs, wcq, wqm, gq, th(wq_nope), th(wq_rope), th(wq_rope_sw), wuk_t,
                                   cos_s, sin_s)
            per_req = lambda a: jnp.transpose(a.reshape(heads, bs, ss, -1), (1, 0, 2, 3)).reshape(bs, heads * ss, -1)
            lat = _mla_sample(page_table, per_req(ql), per_req(qr), cache_kv_latent, cache_kr_t,
                              c_new, kr_new, ss, scale)
            lat = jnp.transpose(lat.reshape(bs, heads, ss, kv_rank), (1, 0, 2, 3)).reshape(heads, bs * ss, kv_rank)
            att = jnp.transpose(_head_up(lat, wuv_h), (1, 0, 2)).reshape(bs * ss, hv_w)
            ym = _mem_attn(qm.reshape(bs, ss, mem_w), mem_k_s, mem_v_s, l, mem_heads).reshape(bs * ss, mem_w)
            xs, *route_s = ln_route(att, ym, xs, w_out[:hv_w], w_out[hv_w:])

        xp, xs = _moe_ln([xp, xs], [route_p, route_s], l, w_gate, b_gate, w_up, b_up, w_down, b_down,
                         ln2_g[l].reshape(1, d), ln2_b[l].reshape(1, d), alpha)

    ne_shape = (depth, bp, n_mem, mem_heads, mem_w // mem_heads)
    return (xp.reshape(bp, sp, d), xs.reshape(bs, ss, d),
            c_p.reshape(bp, sp, kv_rank), kr_p.reshape(bp, sp, rope_dim),
            c_s.reshape(bs, ss, kv_rank), kr_s.reshape(bs, ss, rope_dim),
            mem_k_p.reshape(ne_shape), mem_v_p.reshape(ne_shape),
            jnp.stack(conv_p), jnp.stack(conv_s))
```
